```python
import math
import jax, jax.numpy as jnp
from jax import lax
import numpy as np

D_MODEL = 1024
BATCH = 8
SEQ = 2048
DEPTH = 2

CHUNK = 64
Q_BLOCK = 128
EPS = 1e-6

MLA_HEADS = 8
MLA_NOPE = 64
MLA_ROPE = 32
MLA_V = 64
MLA_Q_RANK = 256
MLA_KV_RANK = 128
MLA_WIDTH = MLA_HEADS * MLA_V
MLA_QK_DIM = MLA_NOPE + MLA_ROPE
ROPE_THETA = 10000.0

FOX_HEADS = 8
FOX_HEAD_DIM = 64
FOX_WIDTH = FOX_HEADS * FOX_HEAD_DIM

S5_WIDTH = 512
S5_GROUP = 16
S5_GROUPS = S5_WIDTH // S5_GROUP
S5_STATE = 64
DT_MIN = 1e-3
DT_MAX = 1e-1

BRANCH_WIDTH = MLA_WIDTH + FOX_WIDTH + S5_WIDTH
IN_SPLITS = (MLA_Q_RANK, MLA_KV_RANK, MLA_ROPE,
             FOX_WIDTH, FOX_WIDTH, FOX_WIDTH, FOX_HEADS,
             S5_WIDTH,
             MLA_WIDTH, FOX_WIDTH, S5_WIDTH,
             D_MODEL, D_MODEL, D_MODEL)
IN_WIDTH = MLA_Q_RANK + MLA_KV_RANK + MLA_ROPE + 3 * FOX_WIDTH + FOX_HEADS + S5_WIDTH + BRANCH_WIDTH + 3 * D_MODEL

kernel_name = "hybrid_mla_fox_s5_gated_trunk"


def rmsnorm(x, g):
    xf = x.astype(jnp.float32)
    y = xf * lax.rsqrt(jnp.mean(xf * xf, axis=-1, keepdims=True) + EPS)
    return (y * g.astype(jnp.float32)).astype(x.dtype)


def rope_tables(positions, dtype):
    inv = 1.0 / (ROPE_THETA ** (jnp.arange(0, MLA_ROPE, 2, dtype=jnp.float32) / MLA_ROPE))
    ang = positions.astype(jnp.float32)[..., None] * inv
    return jnp.cos(ang).astype(dtype)[:, :, None, :], jnp.sin(ang).astype(dtype)[:, :, None, :]


def apply_rope(x, cos, sin):
    x1, x2 = jnp.split(x, 2, axis=-1)
    return jnp.concatenate([x1 * cos - x2 * sin, x1 * sin + x2 * cos], axis=-1)


def swept_attention(q, k, v, scale, chunk_causal, cum_logf=None):
    S = q.shape[1]
    outs = []
    for i in range(S // Q_BLOCK):
        q0, q1 = i * Q_BLOCK, (i + 1) * Q_BLOCK
        kv_end = q1
        logits = jnp.einsum('bqhd,bkhd->bhqk', q[:, q0:q1], k[:, :kv_end]).astype(jnp.float32) * scale
        qpos = jnp.arange(q0, q1)[:, None]
        kpos = jnp.arange(kv_end)[None, :]
        if chunk_causal:
            allowed = (kpos // CHUNK) <= (qpos // CHUNK)
        else:
            allowed = kpos <= qpos
        if cum_logf is not None:
            c_q = jnp.transpose(cum_logf[:, q0:q1], (0, 2, 1))[..., :, None]
            c_k = jnp.transpose(cum_logf[:, :kv_end], (0, 2, 1))[..., None, :]
            logits = logits + (c_q - c_k)
        logits = jnp.where(allowed, logits, -jnp.inf)
        p = jax.nn.softmax(logits, axis=-1).astype(v.dtype)
        outs.append(jnp.einsum('bhqk,bkhd->bqhd', p, v[:, :kv_end]))
    return jnp.concatenate(outs, axis=1)


def mla_branch(cq, ckv, kpe, cos, sin, q_a_norm, w_q_up, kv_a_norm, w_kv_up, q_norm, k_norm):
    Bsz, S, _ = cq.shape
    q = (rmsnorm(cq, q_a_norm) @ w_q_up).reshape(Bsz, S, MLA_HEADS, MLA_QK_DIM)
    q_nope, q_pe = jnp.split(q, [MLA_NOPE], axis=-1)
    kv = (rmsnorm(ckv, kv_a_norm) @ w_kv_up).reshape(Bsz, S, MLA_HEADS, MLA_NOPE + MLA_V)
    k_nope, v = jnp.split(kv, [MLA_NOPE], axis=-1)
    k_pe = apply_rope(kpe[:, :, None, :], cos, sin)
    q = jnp.concatenate([q_nope, apply_rope(q_pe, cos, sin)], axis=-1)
    k = jnp.concatenate([k_nope, jnp.broadcast_to(k_pe, (Bsz, S, MLA_HEADS, MLA_ROPE))], axis=-1)
    q = rmsnorm(q, q_norm)
    k = rmsnorm(k, k_norm)
    y = swept_attention(q, k, v, 1.0 / math.sqrt(MLA_QK_DIM), chunk_causal=True)
    return y.reshape(Bsz, S, MLA_WIDTH)


def fox_branch(fq, fk, fv, ff, b_f, q_norm, k_norm):
    Bsz, S, _ = fq.shape
    q = rmsnorm(fq.reshape(Bsz, S, FOX_HEADS, FOX_HEAD_DIM), q_norm)
    k = rmsnorm(fk.reshape(Bsz, S, FOX_HEADS, FOX_HEAD_DIM), k_norm)
    v = fv.reshape(Bsz, S, FOX_HEADS, FOX_HEAD_DIM)
    log_f = jax.nn.log_sigmoid(ff.astype(jnp.float32) + b_f.astype(jnp.float32))
    cum = jnp.cumsum(log_f, axis=1)
    y = swept_attention(q, k, v, 1.0 / math.sqrt(FOX_HEAD_DIM), chunk_causal=False, cum_logf=cum)
    return y.reshape(Bsz, S, FOX_WIDTH)


def _ssm_combine(e1, e2):
    a1r, a1i, b1r, b1i = e1
    a2r, a2i, b2r, b2i = e2
    return (a2r * a1r - a2i * a1i,
            a2r * a1i + a2i * a1r,
            a2r * b1r - a2i * b1i + b2r,
            a2r * b1i + a2i * b1r + b2i)


def s5_branch(u, lam_re, lam_im, log_dt, b_re, b_im, c_re, c_im, d_skip, w_glu, b_glu):
    Bsz, S, _ = u.shape
    uf = u.astype(jnp.float32).reshape(Bsz, S, S5_GROUPS, S5_GROUP)
    dt = jnp.exp(log_dt.astype(jnp.float32))[:, None]
    lr = lam_re.astype(jnp.float32)
    li = lam_im.astype(jnp.float32)
    mag = jnp.exp(lr * dt)
    a_re = mag * jnp.cos(li * dt)
    a_im = mag * jnp.sin(li * dt)
    den = lr * lr + li * li
    f_re = ((a_re - 1.0) * lr + a_im * li) / den
    f_im = (a_im * lr - (a_re - 1.0) * li) / den
    br = b_re.astype(jnp.float32)
    bi = b_im.astype(jnp.float32)
    bb_re = f_re[..., None] * br - f_im[..., None] * bi
    bb_im = f_re[..., None] * bi + f_im[..., None] * br
    bu_re = jnp.einsum('bsgc,gnc->bsgn', uf, bb_re)
    bu_im = jnp.einsum('bsgc,gnc->bsgn', uf, bb_im)
    a_re_full = jnp.broadcast_to(a_re, bu_re.shape)
    a_im_full = jnp.broadcast_to(a_im, bu_re.shape)
    _, _, x_re, x_im = lax.associative_scan(_ssm_combine, (a_re_full, a_im_full, bu_re, bu_im), axis=1)
    y = (jnp.einsum('bsgn,gcn->bsgc', x_re, c_re.astype(jnp.float32))
         - jnp.einsum('bsgn,gcn->bsgc', x_im, c_im.astype(jnp.float32)))
    y = (y + d_skip.astype(jnp.float32).reshape(S5_GROUPS, S5_GROUP) * uf).reshape(Bsz, S, S5_WIDTH)
    z = jax.nn.gelu(y)
    z = z * jax.nn.sigmoid(z @ w_glu.astype(jnp.float32) + b_glu.astype(jnp.float32))
    return z.astype(u.dtype)


def hybrid_layer(x, cos, sin, norm_g, w_in,
                 mla_q_a_norm, mla_w_q_up, mla_kv_a_norm, mla_w_kv_up, mla_q_norm, mla_k_norm,
                 fox_b_f, fox_q_norm, fox_k_norm,
                 s5_lambda_re, s5_lambda_im, s5_log_dt, s5_b_re, s5_b_im, s5_c_re, s5_c_im,
                 s5_d, s5_w_glu, s5_b_glu, w_branch_out, w_out):
    h = rmsnorm(x, norm_g)
    proj = h @ w_in
    offsets = np.cumsum(IN_SPLITS)[:-1].tolist()
    (cq, ckv, kpe, fq, fk, fv, ff, s5u,
     g_mla, g_fox, g_s5, m_mla, m_fox, m_s5) = jnp.split(proj, offsets, axis=-1)

    y_mla = mla_branch(cq, ckv, kpe, cos, sin, mla_q_a_norm, mla_w_q_up,
                       mla_kv_a_norm, mla_w_kv_up, mla_q_norm, mla_k_norm)
    y_fox = fox_branch(fq, fk, fv, ff, fox_b_f, fox_q_norm, fox_k_norm)
    y_s5 = s5_branch(s5u, s5_lambda_re, s5_lambda_im, s5_log_dt, s5_b_re, s5_b_im,
                     s5_c_re, s5_c_im, s5_d, s5_w_glu, s5_b_glu)

    wo_mla, wo_fox, wo_s5 = jnp.split(w_branch_out, [MLA_WIDTH, MLA_WIDTH + FOX_WIDTH], axis=0)
    o_mla = (y_mla * jax.nn.silu(g_mla)) @ wo_mla
    o_fox = (y_fox * jax.nn.silu(g_fox)) @ wo_fox
    o_s5 = (y_s5 * jax.nn.silu(g_s5)) @ wo_s5
    merged = jax.nn.sigmoid(m_mla) * o_mla + jax.nn.sigmoid(m_fox) * o_fox + jax.nn.sigmoid(m_s5) * o_s5
    return x + merged @ w_out


def setup_inputs(seed: int = 0) -> dict:
    key = jax.random.key(seed)
    ks = jax.random.split(key, 32)
    f32 = jnp.float32

    def nrm(k, shape, scale):
        return jax.random.normal(k, shape, f32) * scale

    def gain(k, shape):
        return 1.0 + 0.02 * jax.random.normal(k, shape, f32)

    L = DEPTH
    x = jax.random.normal(ks[0], (BATCH, SEQ, D_MODEL), f32)
    start = jax.random.randint(ks[1], (BATCH, 1), 0, 4096, dtype=jnp.int32)
    positions = start + jnp.arange(SEQ, dtype=jnp.int32)[None, :]
    n_idx = jnp.arange(S5_STATE, dtype=f32)
    log_dt = jax.random.uniform(ks[17], (L, S5_GROUPS), f32, math.log(DT_MIN), math.log(DT_MAX))
    return {
        "x": x,
        "positions": positions,
        "norm_g": gain(ks[2], (L, D_MODEL)),
        "w_in": nrm(ks[3], (L, D_MODEL, IN_WIDTH), D_MODEL ** -0.5),
        "mla_q_a_norm": gain(ks[4], (L, MLA_Q_RANK)),
        "mla_w_q_up": nrm(ks[5], (L, MLA_Q_RANK, MLA_HEADS * MLA_QK_DIM), MLA_Q_RANK ** -0.5),
        "mla_kv_a_norm": gain(ks[6], (L, MLA_KV_RANK)),
        "mla_w_kv_up": nrm(ks[7], (L, MLA_KV_RANK, MLA_HEADS * (MLA_NOPE + MLA_V)), MLA_KV_RANK ** -0.5),
        "mla_q_norm": gain(ks[8], (L, MLA_QK_DIM)),
        "mla_k_norm": gain(ks[9], (L, MLA_QK_DIM)),
        "fox_b_f": 1.0 + 0.5 * jax.random.normal(ks[10], (L, FOX_HEADS), f32),
        "fox_q_norm": gain(ks[11], (L, FOX_HEAD_DIM)),
        "fox_k_norm": gain(ks[12], (L, FOX_HEAD_DIM)),
        "s5_lambda_re": -0.5 * (1.0 + 0.02 * jax.random.normal(ks[13], (L, S5_GROUPS, S5_STATE), f32)),
        "s5_lambda_im": math.pi * n_idx + 0.01 * jax.random.normal(ks[14], (L, S5_GROUPS, S5_STATE), f32),
        "s5_log_dt": log_dt,
        "s5_b_re": nrm(ks[15], (L, S5_GROUPS, S5_STATE, S5_GROUP), (2.0 * S5_GROUP) ** -0.5),
        "s5_b_im": nrm(ks[16], (L, S5_GROUPS, S5_STATE, S5_GROUP), (2.0 * S5_GROUP) ** -0.5),
        "s5_c_re": nrm(ks[18], (L, S5_GROUPS, S5_GROUP, S5_STATE), (2.0 * S5_STATE) ** -0.5),
        "s5_c_im": nrm(ks[19], (L, S5_GROUPS, S5_GROUP, S5_STATE), (2.0 * S5_STATE) ** -0.5),
        "s5_d": nrm(ks[20], (L, S5_WIDTH), 1.0),
        "s5_w_glu": nrm(ks[21], (L, S5_WIDTH, S5_WIDTH), S5_WIDTH ** -0.5),
        "s5_b_glu": nrm(ks[22], (L, S5_WIDTH), 0.02),
        "w_branch_out": nrm(ks[23], (L, BRANCH_WIDTH, D_MODEL), MLA_WIDTH ** -0.5),
        "w_out": nrm(ks[24], (L, D_MODEL, D_MODEL), D_MODEL ** -0.5),
    }


def reference(x, positions, norm_g, w_in,
              mla_q_a_norm, mla_w_q_up, mla_kv_a_norm, mla_w_kv_up, mla_q_norm, mla_k_norm,
              fox_b_f, fox_q_norm, fox_k_norm,
              s5_lambda_re, s5_lambda_im, s5_log_dt, s5_b_re, s5_b_im, s5_c_re, s5_c_im,
              s5_d, s5_w_glu, s5_b_glu, w_branch_out, w_out):
    cos, sin = rope_tables(positions, x.dtype)
    h = x
    for l in range(DEPTH):
        h = hybrid_layer(h, cos, sin, norm_g[l], w_in[l],
                         mla_q_a_norm[l], mla_w_q_up[l], mla_kv_a_norm[l], mla_w_kv_up[l],
                         mla_q_norm[l], mla_k_norm[l],
                         fox_b_f[l], fox_q_norm[l], fox_k_norm[l],
                         s5_lambda_re[l], s5_lambda_im[l], s5_log_dt[l], s5_b_re[l], s5_b_im[l],
                         s5_c_re[l], s5_c_im[l], s5_d[l], s5_w_glu[l], s5_b_glu[l],
                         w_branch_out[l], w_out[l])
    return h
```

```python
import functools
import math

import numpy as np
import jax
import jax.numpy as jnp
from jax import lax
from jax.experimental import pallas as pl
from jax.experimental.pallas import tpu as pltpu

F32 = jnp.float32
BF16 = jnp.bfloat16

D_MODEL = 1024
CHUNK = 64
EPS = 1e-6

MLA_HEADS = 8
MLA_NOPE = 64
MLA_ROPE = 32
MLA_V = 64
MLA_Q_RANK = 256
MLA_KV_RANK = 128
MLA_WIDTH = MLA_HEADS * MLA_V
MLA_QK_DIM = MLA_NOPE + MLA_ROPE
ROPE_THETA = 10000.0

FOX_HEADS = 8
FOX_HEAD_DIM = 64
FOX_WIDTH = FOX_HEADS * FOX_HEAD_DIM

S5_WIDTH = 512
S5_GROUP = 16
S5_GROUPS = S5_WIDTH // S5_GROUP
S5_STATE = 64
S5_STATES = S5_GROUPS * S5_STATE

BRANCH_WIDTH = MLA_WIDTH + FOX_WIDTH + S5_WIDTH

LANE = 128
HEAD_PAD = 128
PAIR = 2 * HEAD_PAD
N_PAIRS = 4
N_SPLIT = 3

_OFF = np.cumsum([0, MLA_Q_RANK, MLA_KV_RANK, MLA_ROPE, FOX_WIDTH, FOX_WIDTH, FOX_WIDTH, FOX_HEADS,
                  S5_WIDTH, MLA_WIDTH, FOX_WIDTH, S5_WIDTH, D_MODEL, D_MODEL, D_MODEL]).tolist()
(O_CQ, O_CKV, O_KPE, O_FQ, O_FK, O_FV, O_FF, O_S5U, O_GATE, _o1, _o2, O_MERGE, _o3, _o4, O_END) = _OFF

C_CQ = 0
C_CKV = C_CQ + MLA_Q_RANK
C_KPA = C_CKV + MLA_KV_RANK
C_KPB = C_KPA + LANE
C_FQ = C_KPB + LANE
C_FK = C_FQ + FOX_WIDTH
C_FV = C_FK + FOX_WIDTH
C_FF = C_FV + FOX_WIDTH
C_S5 = C_FF + LANE
C_END = C_S5 + S5_WIDTH

VMEM_LIMIT = 56 * 1024 * 1024


def _cparams(sem):
    return pltpu.CompilerParams(dimension_semantics=sem, vmem_limit_bytes=VMEM_LIMIT)


def _full(shape):
    n = len(shape)
    return pl.BlockSpec(shape, lambda *_: (0,) * n)


def _rms(x, g):
    return x * lax.rsqrt(jnp.mean(x * x, axis=-1, keepdims=True) + EPS) * g


def _dot(a, b):
    return jnp.dot(a, b, preferred_element_type=F32)


def _rope_kernel(pos_ref, inv_ref, sign_ref, ca_ref, sb_ref):
    ang = pos_ref[0].astype(F32) * inv_ref[...]
    ca_ref[0] = jnp.cos(ang)
    sb_ref[0] = jnp.sin(ang) * sign_ref[...]


def _rope_tables(positions):
    B, S = positions.shape
    ts = min(S, 1024)
    inv = 1.0 / (ROPE_THETA ** (jnp.arange(0, MLA_ROPE, 2, dtype=F32) / MLA_ROPE))
    half = MLA_ROPE // 2
    inv_row = jnp.zeros((1, LANE), F32).at[0, MLA_NOPE:MLA_NOPE + half].set(inv)
    inv_row = inv_row.at[0, MLA_NOPE + half:MLA_NOPE + MLA_ROPE].set(inv)
    sign = np.zeros((1, LANE), np.float32)
    sign[0, MLA_NOPE:MLA_NOPE + half] = -1.0
    sign[0, MLA_NOPE + half:MLA_NOPE + MLA_ROPE] = 1.0
    blk = pl.BlockSpec((1, ts, LANE), lambda b, s: (b, s, 0))
    return pl.pallas_call(
        _rope_kernel,
        grid=(B, S // ts),
        in_specs=[pl.BlockSpec((1, ts, 1), lambda b, s: (b, s, 0)), _full((1, LANE)), _full((1, LANE))],
        out_specs=[blk, blk],
        out_shape=[jax.ShapeDtypeStruct((B, S, LANE), F32)] * 2,
        compiler_params=_cparams(("parallel", "parallel")),
        name="rope_tables",
    )(positions.reshape(B, S, 1), inv_row, jnp.asarray(sign))


def _prep_kernel(x_ref, ca_ref, sb_ref, ng_ref, wcat_ref, qan_ref, wqa_ref, wqb_ref, kvn_ref, wk_ref, wv_ref,
                 gqm_ref, gkm_ref, ones_ref, bf_ref, gqf_ref, gkf_ref, bd_ref, tri_ref,
                 eq_ref, ek_ref, oq_ref, ok_ref,
                 qm_ref, km_ref, vm_ref, qf_ref, kf_ref, vf_ref, u_ref, carry_ref):
    @pl.when(pl.program_id(1) == 0)
    def _():
        carry_ref[...] = jnp.zeros_like(carry_ref)

    x = x_ref[0]
    h = _rms(x, ng_ref[...]).astype(BF16)
    proj = _dot(h, wcat_ref[...])
    ca = ca_ref[0]
    sb = sb_ref[0]

    cqn = _rms(proj[:, C_CQ:C_CQ + MLA_Q_RANK], qan_ref[...]).astype(BF16)
    qa = _dot(cqn, wqa_ref[...])
    qb = _dot(cqn, wqb_ref[...])
    ckvn = _rms(proj[:, C_CKV:C_CKV + MLA_KV_RANK], kvn_ref[...]).astype(BF16)
    kc = _dot(ckvn, wk_ref[...])
    vm_ref[0] = _dot(ckvn, wv_ref[...]).astype(BF16)
    kpe = proj[:, C_KPA:C_KPA + LANE] * ca + proj[:, C_KPB:C_KPB + LANE] * sb
    ones = ones_ref[...]
    inv_d = 1.0 / MLA_QK_DIM
    for hd in range(MLA_HEADS):
        sl = slice(hd * HEAD_PAD, (hd + 1) * HEAD_PAD)
        qh = qa[:, sl] * ca + qb[:, sl] * sb
        ss = _dot((qh * qh).astype(BF16), ones)
        qm_ref[0, :, sl] = (qh * lax.rsqrt(ss * inv_d + EPS) * gqm_ref[...]).astype(BF16)
        kh = kc[:, sl] + kpe
        ss = _dot((kh * kh).astype(BF16), ones)
        km_ref[0, :, sl] = (kh * lax.rsqrt(ss * inv_d + EPS) * gkm_ref[...]).astype(BF16)

    bd = bd_ref[...]
    inv_d = 1.0 / FOX_HEAD_DIM
    fq = proj[:, C_FQ:C_FQ + FOX_WIDTH]
    ss = _dot((fq * fq).astype(BF16), bd)
    fqn = (fq * lax.rsqrt(ss * inv_d + EPS) * gqf_ref[...]).astype(BF16)
    fk = proj[:, C_FK:C_FK + FOX_WIDTH]
    ss = _dot((fk * fk).astype(BF16), bd)
    fkn = (fk * lax.rsqrt(ss * inv_d + EPS) * gkf_ref[...]).astype(BF16)
    vf_ref[0] = proj[:, C_FV:C_FV + FOX_WIDTH].astype(BF16)

    z = proj[:, C_FF:C_FF + LANE] + bf_ref[...]
    log_f = jnp.minimum(z, 0.0) - jnp.log1p(jnp.exp(-jnp.abs(z)))
    cum = jnp.dot(tri_ref[...], log_f, preferred_element_type=F32,
                  precision=lax.Precision.HIGHEST) + carry_ref[...]
    carry_ref[...] = cum[cum.shape[0] - 1:, :]
    hi = cum.astype(BF16)
    r1 = cum - hi.astype(F32)
    mid = r1.astype(BF16)
    lo = (r1 - mid.astype(F32)).astype(BF16)
    pieces = jnp.concatenate([hi, mid, lo], axis=1)
    auxq = (_dot(pieces, eq_ref[...]) + oq_ref[...]).astype(BF16)
    auxk = (_dot(pieces, ek_ref[...]) + ok_ref[...]).astype(BF16)
    for p in range(N_PAIRS):
        src = slice(p * LANE, (p + 1) * LANE)
        qf_ref[0, :, p * PAIR:p * PAIR + LANE] = fqn[:, src]
        qf_ref[0, :, p * PAIR + LANE:(p + 1) * PAIR] = auxq[:, src]
        kf_ref[0, :, p * PAIR:p * PAIR + LANE] = fkn[:, src]
        kf_ref[0, :, p * PAIR + LANE:(p + 1) * PAIR] = auxk[:, src]

    u_ref[0] = proj[:, C_S5:C_S5 + S5_WIDTH]


def _prep_weights(lw):
    w_in = lw["w_in"]
    zpad = jnp.zeros((D_MODEL, MLA_NOPE), F32)
    kpe = w_in[:, O_KPE:O_KPE + MLA_ROPE]
    half = MLA_ROPE // 2
    kpe_sw = jnp.concatenate([kpe[:, half:], kpe[:, :half]], axis=1)
    tail = jnp.zeros((D_MODEL, LANE - MLA_NOPE - MLA_ROPE), F32)
    ff = jnp.concatenate([w_in[:, O_FF:O_FF + FOX_HEADS], jnp.zeros((D_MODEL, LANE - FOX_HEADS), F32)], axis=1)
    wcat = jnp.concatenate([
        w_in[:, O_CQ:O_CQ + MLA_Q_RANK], w_in[:, O_CKV:O_CKV + MLA_KV_RANK],
        zpad, kpe, tail, zpad, kpe_sw, tail,
        w_in[:, O_FQ:O_FQ + 3 * FOX_WIDTH], ff, w_in[:, O_S5U:O_S5U + S5_WIDTH]], axis=1).astype(BF16)

    wq = lw["mla_w_q_up"].reshape(MLA_Q_RANK, MLA_HEADS, MLA_QK_DIM)
    nope, pe = wq[:, :, :MLA_NOPE], wq[:, :, MLA_NOPE:]
    pe_sw = jnp.concatenate([pe[:, :, half:], pe[:, :, :half]], axis=2)
    z32 = jnp.zeros((MLA_Q_RANK, MLA_HEADS, HEAD_PAD - MLA_QK_DIM), F32)
    wqa = jnp.concatenate([nope, pe, z32], axis=2).reshape(MLA_Q_RANK, MLA_HEADS * HEAD_PAD).astype(BF16)
    wqb = jnp.concatenate([jnp.zeros_like(nope), pe_sw, z32], axis=2).reshape(MLA_Q_RANK, -1).astype(BF16)

    wkv = lw["mla_w_kv_up"].reshape(MLA_KV_RANK, MLA_HEADS, MLA_NOPE + MLA_V)
    wk = jnp.concatenate([wkv[:, :, :MLA_NOPE], jnp.zeros((MLA_KV_RANK, MLA_HEADS, HEAD_PAD - MLA_NOPE), F32)],
                         axis=2).reshape(MLA_KV_RANK, -1).astype(BF16)
    wv = wkv[:, :, MLA_NOPE:].reshape(MLA_KV_RANK, MLA_WIDTH).astype(BF16)

    pad = jnp.zeros((HEAD_PAD - MLA_QK_DIM,), F32)
    gqm = (jnp.concatenate([lw["mla_q_norm"], pad]) * (1.0 / math.sqrt(MLA_QK_DIM)))[None]
    gkm = jnp.concatenate([lw["mla_k_norm"], pad])[None]
    bf = jnp.concatenate([lw["fox_b_f"], jnp.zeros((LANE - FOX_HEADS,), F32)])[None]
    gqf = (jnp.tile(lw["fox_q_norm"], FOX_HEADS) * (1.0 / math.sqrt(FOX_HEAD_DIM)))[None]
    gkf = jnp.tile(lw["fox_k_norm"], FOX_HEADS)[None]
    return dict(ng=lw["norm_g"][None], wcat=wcat, qan=lw["mla_q_a_norm"][None], wqa=wqa, wqb=wqb,
                kvn=lw["mla_kv_a_norm"][None], wk=wk, wv=wv, gqm=gqm, gkm=gkm, bf=bf, gqf=gqf, gkf=gkf)


def _prep_consts(tm):
    ones = np.ones((HEAD_PAD, HEAD_PAD), np.float32)
    bd = np.kron(np.eye(FOX_HEADS, dtype=np.float32), np.ones((FOX_HEAD_DIM, FOX_HEAD_DIM), np.float32))
    tri = np.tril(np.ones((tm, tm), np.float32))
    eq = np.zeros((N_SPLIT * LANE, N_PAIRS * LANE), np.float32)
    ek = np.zeros_like(eq)
    oq = np.zeros((1, N_PAIRS * LANE), np.float32)
    ok = np.zeros_like(oq)
    for hd in range(FOX_HEADS):
        p, a = divmod(hd, 2)
        base = p * LANE + 2 * N_SPLIT * a
        for i in range(N_SPLIT):
            ek[i * LANE + hd, base + i] = -1.0
            oq[0, base + i] = 1.0
            eq[i * LANE + hd, base + N_SPLIT + i] = 1.0
            ok[0, base + N_SPLIT + i] = 1.0
    return dict(ones=jnp.asarray(ones, BF16), bd=jnp.asarray(bd, BF16), tri=jnp.asarray(tri),
                eq=jnp.asarray(eq, BF16), ek=jnp.asarray(ek, BF16), oq=jnp.asarray(oq), ok=jnp.asarray(ok))


def _prep_call(x, ca, sb, w, c, tm):
    B, S, _ = x.shape
    tok = lambda width: pl.BlockSpec((1, tm, width), lambda b, s: (b, s, 0))
    args = [x, ca, sb, w["ng"], w["wcat"], w["qan"], w["wqa"], w["wqb"], w["kvn"], w["wk"], w["wv"],
            w["gqm"], w["gkm"], c["ones"], w["bf"], w["gqf"], w["gkf"], c["bd"], c["tri"],
            c["eq"], c["ek"], c["oq"], c["ok"]]
    in_specs = [tok(D_MODEL), tok(LANE), tok(LANE)] + [_full(a.shape) for a in args[3:]]
    widths = [MLA_HEADS * HEAD_PAD, MLA_HEADS * HEAD_PAD, MLA_WIDTH, N_PAIRS * PAIR, N_PAIRS * PAIR, FOX_WIDTH]
    out_shape = [jax.ShapeDtypeStruct((B, S, wd), BF16) for wd in widths]
    out_shape.append(jax.ShapeDtypeStruct((B, S, S5_WIDTH), F32))
    return pl.pallas_call(
        _prep_kernel,
        grid=(B, S // tm),
        in_specs=in_specs,
        out_specs=[tok(wd) for wd in widths] + [tok(S5_WIDTH)],
        out_shape=out_shape,
        scratch_shapes=[pltpu.VMEM((1, LANE), F32)],
        compiler_params=_cparams(("parallel", "arbitrary")),
        name="prep",
    )(*args)


def _attn_kernel(q_ref, k_ref, v_ref, hm_ref, o_ref, *, tq, chunk_causal):
    S = q_ref.shape[1]
    nq = S // tq
    row = lax.broadcasted_iota(jnp.int32, (tq, tq), 0)
    col = lax.broadcasted_iota(jnp.int32, (tq, tq), 1)
    if chunk_causal:
        allowed = (col // CHUNK) <= (row // CHUNK)
    else:
        allowed = col <= row
    lane = lax.broadcasted_iota(jnp.int32, (tq, LANE), 1)

    def q_block(qi, _):
        q0 = pl.multiple_of(qi * tq, tq)
        q = q_ref[0, pl.ds(q0, tq), :]
        outs = []
        for a in range(2):
            qa = q * hm_ref[a:a + 1, :]

            def step(j, carry, masked):
                m, l, acc = carry
                k0 = pl.multiple_of(j * tq, tq)
                k = k_ref[0, pl.ds(k0, tq), :]
                v = v_ref[0, pl.ds(k0, tq), :]
                s = lax.dot_general(qa, k, (((1,), (1,)), ((), ())), preferred_element_type=F32)
                if masked:
                    s = jnp.where(allowed, s, -jnp.inf)
                m_new = jnp.maximum(m, jnp.max(s, axis=-1, keepdims=True))
                alpha = jnp.exp(m - m_new)
                p = jnp.exp(s - m_new)
                l = alpha * l + jnp.sum(p, axis=-1, keepdims=True)
                acc = alpha * acc + _dot(p.astype(BF16), v)
                return m_new, l, acc

            init = (jnp.full((tq, 1), -jnp.inf, F32), jnp.zeros((tq, 1), F32), jnp.zeros((tq, LANE), F32))
            carry = lax.fori_loop(0, qi, functools.partial(step, masked=False), init)
            _, l, acc = step(qi, carry, True)
            outs.append(acc / l)
        o_ref[0, pl.ds(q0, tq), :] = jnp.where(lane < LANE // 2, outs[0], outs[1]).astype(o_ref.dtype)
        return 0

    lax.fori_loop(0, nq, q_block, 0)


def _attn_call(q, k, v, head_mask, chunk_causal, tq, name):
    B, S, _ = q.shape
    kern = functools.partial(_attn_kernel, tq=tq, chunk_causal=chunk_causal)
    return pl.pallas_call(
        kern,
        grid=(B, N_PAIRS),
        in_specs=[pl.BlockSpec((1, S, PAIR), lambda b, p: (b, 0, p)),
                  pl.BlockSpec((1, S, PAIR), lambda b, p: (b, 0, p)),
                  pl.BlockSpec((1, S, LANE), lambda b, p: (b, 0, p)),
                  _full((2, PAIR))],
        out_specs=pl.BlockSpec((1, S, LANE), lambda b, p: (b, 0, p)),
        out_shape=jax.ShapeDtypeStruct((B, S, N_PAIRS * LANE), BF16),
        compiler_params=_cparams(("parallel", "parallel")),
        name=name,
    )(q, k, v, head_mask)


def _head_masks():
    mla = np.zeros((2, PAIR), np.float32)
    fox = np.zeros((2, PAIR), np.float32)
    for a in range(2):
        mla[a, a * HEAD_PAD:(a + 1) * HEAD_PAD] = 1.0
        fox[a, a * FOX_HEAD_DIM:(a + 1) * FOX_HEAD_DIM] = 1.0
        fox[a, LANE + 2 * N_SPLIT * a:LANE + 2 * N_SPLIT * (a + 1)] = 1.0
    return jnp.asarray(mla, BF16), jnp.asarray(fox, BF16)


def _s5_param_kernel(lr_ref, li_ref, ldt_ref, br_ref, bi_ref, are_ref, aim_ref, bbr_ref, bbi_ref):
    lr = lr_ref[...]
    li = li_ref[...]
    dt = jnp.exp(ldt_ref[...])
    mag = jnp.exp(lr * dt)
    a_re = mag * jnp.cos(li * dt)
    a_im = mag * jnp.sin(li * dt)
    den = lr * lr + li * li
    f_re = ((a_re - 1.0) * lr + a_im * li) / den
    f_im = (a_im * lr - (a_re - 1.0) * li) / den
    br = br_ref[...]
    bi = bi_ref[...]
    are_ref[...] = a_re
    aim_ref[...] = a_im
    bbr_ref[...] = f_re * br - f_im * bi
    bbi_ref[...] = f_re * bi + f_im * br


def _s5_params(lw, n_batch):
    G, N, C = S5_GROUPS, S5_STATE, S5_GROUP
    rep = lambda a: jnp.repeat(a, C, axis=0)
    ldt = jnp.broadcast_to(lw["s5_log_dt"][:, None], (G, N))
    tr = lambda b: jnp.transpose(b, (0, 2, 1)).reshape(G * C, N)
    shp = jax.ShapeDtypeStruct((G * C, N), F32)
    a_re, a_im, bb_re, bb_im = pl.pallas_call(
        _s5_param_kernel, out_shape=[shp] * 4, name="s5_params",
    )(rep(lw["s5_lambda_re"]), rep(lw["s5_lambda_im"]), rep(ldt), tr(lw["s5_b_re"]), tr(lw["s5_b_im"]))
    eye = jnp.eye(G, dtype=F32)

    def blockdiag_in(bb):
        return (bb.reshape(G, C, 1, N) * eye[:, None, :, None]).reshape(G * C, G * N)

    def blockdiag_out(cc):
        return (jnp.transpose(cc, (0, 2, 1)).reshape(G, N, 1, C) * eye[:, None, :, None]).reshape(G * N, G * C)

    w_b = jnp.concatenate([blockdiag_in(bb_re), blockdiag_in(bb_im)], axis=1).astype(BF16)
    w_cre = blockdiag_out(lw["s5_c_re"]).astype(BF16)
    w_cim = blockdiag_out(lw["s5_c_im"]).astype(BF16)
    a_row = jnp.concatenate([a_re[::C].reshape(1, G * N), a_im[::C].reshape(1, G * N)], axis=1)
    a_rep = jnp.broadcast_to(a_row, (n_batch, 2 * G * N))
    return dict(w_b=w_b, w_cre=w_cre, w_cim=w_cim, a=a_rep, d=lw["s5_d"][None],
                w_glu=lw["s5_w_glu"].astype(BF16), b_glu=lw["s5_b_glu"][None])


def _s5_kernel(u_ref, a_ref, wb_ref, wcr_ref, wci_ref, d_ref, wg_ref, bg_ref, o_ref, st_ref, xs_ref, *, nb, ts):
    @pl.when(pl.program_id(0) == 0)
    def _():
        st_ref[...] = jnp.zeros_like(st_ref)

    N = S5_STATES
    u = u_ref[...]
    xs_ref[...] = _dot(u.astype(BF16), wb_ref[...])
    ar = a_ref[:, :N]
    ai = a_ref[:, N:]

    def step(t, carry):
        xr, xi = carry
        r = pl.multiple_of(t * nb, nb)
        br = xs_ref[pl.ds(r, nb), :N]
        bi = xs_ref[pl.ds(r, nb), N:]
        nr = ar * xr - ai * xi + br
        ni = ar * xi + ai * xr + bi
        xs_ref[pl.ds(r, nb), :N] = nr
        xs_ref[pl.ds(r, nb), N:] = ni
        return nr, ni

    xr, xi = lax.fori_loop(0, ts, step, (st_ref[:, :N], st_ref[:, N:]))
    st_ref[:, :N] = xr
    st_ref[:, N:] = xi

    y = _dot(xs_ref[:, :N].astype(BF16), wcr_ref[...]) - _dot(xs_ref[:, N:].astype(BF16), wci_ref[...])
    y = y + d_ref[...] * u
    z = jax.nn.gelu(y, approximate=True)
    gate = jax.nn.sigmoid(_dot(z.astype(BF16), wg_ref[...]) + bg_ref[...])
    o_ref[...] = (z * gate).astype(o_ref.dtype)


def _s5_call(u_tm, p, nb, ts):
    rows = u_tm.shape[0]
    blk = ts * nb
    args = [u_tm, p["a"], p["w_b"], p["w_cre"], p["w_cim"], p["d"], p["w_glu"], p["b_glu"]]
    kern = functools.partial(_s5_kernel, nb=nb, ts=ts)
    return pl.pallas_call(
        kern,
        grid=(rows // blk,),
        in_specs=[pl.BlockSpec((blk, S5_WIDTH), lambda i: (i, 0))] + [_full(a.shape) for a in args[1:]],
        out_specs=pl.BlockSpec((blk, S5_WIDTH), lambda i: (i, 0)),
        out_shape=jax.ShapeDtypeStruct((rows, S5_WIDTH), BF16),
        scratch_shapes=[pltpu.VMEM((nb, 2 * S5_STATES), F32), pltpu.VMEM((blk, 2 * S5_STATES), F32)],
        compiler_params=_cparams(("arbitrary",)),
        name="s5",
    )(*args)


def _merge_kernel(x_ref, ym_ref, yf_ref, ys_ref, ng_ref, wg_ref, wm_ref, wo_ref, wout_ref, o_ref):
    x = x_ref[0]
    h = _rms(x, ng_ref[...]).astype(BF16)
    merged = None
    for b, y_ref in enumerate((ym_ref, yf_ref, ys_ref)):
        g = _dot(h, wg_ref[:, b * MLA_WIDTH:(b + 1) * MLA_WIDTH])
        gated = (y_ref[0].astype(F32) * (g * jax.nn.sigmoid(g))).astype(BF16)
        o = _dot(gated, wo_ref[b * MLA_WIDTH:(b + 1) * MLA_WIDTH, :])
        m = _dot(h, wm_ref[:, b * D_MODEL:(b + 1) * D_MODEL])
        term = jax.nn.sigmoid(m) * o
        merged = term if merged is None else merged + term
    o_ref[0] = x + _dot(merged.astype(BF16), wout_ref[...])


def _merge_call(x, y_mla, y_fox, y_s5, lw, tm):
    B, S, _ = x.shape
    w_in = lw["w_in"]
    wg = w_in[:, O_GATE:O_GATE + BRANCH_WIDTH].astype(BF16)
    wm = w_in[:, O_MERGE:O_END].astype(BF16)
    wo = lw["w_branch_out"].astype(BF16)
    wout = lw["w_out"].astype(BF16)
    tok = lambda width: pl.BlockSpec((1, tm, width), lambda b, s: (b, s, 0))
    args = [x, y_mla, y_fox, y_s5, lw["norm_g"][None], wg, wm, wo, wout]
    return pl.pallas_call(
        _merge_kernel,
        grid=(B, S // tm),
        in_specs=[tok(D_MODEL), tok(MLA_WIDTH), tok(FOX_WIDTH), tok(S5_WIDTH)] + [_full(a.shape) for a in args[4:]],
        out_specs=tok(D_MODEL),
        out_shape=jax.ShapeDtypeStruct(x.shape, x.dtype),
        compiler_params=_cparams(("parallel", "parallel")),
        name="merge",
    )(*args)


_LAYER_KEYS = ("norm_g", "w_in", "mla_q_a_norm", "mla_w_q_up", "mla_kv_a_norm", "mla_w_kv_up", "mla_q_norm",
               "mla_k_norm", "fox_b_f", "fox_q_norm", "fox_k_norm", "s5_lambda_re", "s5_lambda_im", "s5_log_dt",
               "s5_b_re", "s5_b_im", "s5_c_re", "s5_c_im", "s5_d", "s5_w_glu", "s5_b_glu", "w_branch_out", "w_out")


def _forward(x, positions, params):
    B, S, _ = x.shape
    tm = min(S, 512)
    tq = min(S, 256)
    ts = min(S, 64)
    depth = params["w_in"].shape[0]
    ca, sb = _rope_tables(positions)
    consts = _prep_consts(tm)
    mask_mla, mask_fox = _head_masks()
    h = x
    for l in range(depth):
        lw = {k: params[k][l] for k in _LAYER_KEYS}
        qm, km, vm, qf, kf, vf, u = _prep_call(h, ca, sb, _prep_weights(lw), consts, tm)
        y_mla = _attn_call(qm, km, vm, mask_mla, True, tq, "attn_mla")
        y_fox = _attn_call(qf, kf, vf, mask_fox, False, tq, "attn_fox")
        u_tm = jnp.transpose(u, (1, 0, 2)).reshape(S * B, S5_WIDTH)
        z_tm = _s5_call(u_tm, _s5_params(lw, B), B, ts)
        y_s5 = jnp.transpose(z_tm.reshape(S, B, S5_WIDTH), (1, 0, 2))
        h = _merge_call(h, y_mla, y_fox, y_s5, lw, tm)
    return h


def kernel(x, positions, norm_g, w_in, mla_q_a_norm, mla_w_q_up, mla_kv_a_norm, mla_w_kv_up, mla_q_norm, mla_k_norm, fox_b_f, fox_q_norm, fox_k_norm, s5_lambda_re, s5_lambda_im, s5_log_dt, s5_b_re, s5_b_im, s5_c_re, s5_c_im, s5_d, s5_w_glu, s5_b_glu, w_branch_out, w_out):
    params = dict(zip(_LAYER_KEYS, (norm_g, w_in, mla_q_a_norm, mla_w_q_up, mla_kv_a_norm, mla_w_kv_up, mla_q_norm,
                                    mla_k_norm, fox_b_f, fox_q_norm, fox_k_norm, s5_lambda_re, s5_lambda_im,
                                    s5_log_dt, s5_b_re, s5_b_im, s5_c_re, s5_c_im, s5_d, s5_w_glu, s5_b_glu,
                                    w_branch_out, w_out)))
    return _forward(x, positions, params)
```

```python
import functools
import math

import numpy as np
import jax
import jax.numpy as jnp
from jax import lax
from jax.experimental import pallas as pl
from jax.experimental.pallas import tpu as pltpu

F32 = jnp.float32
BF16 = jnp.bfloat16

D_MODEL = 1024
CHUNK = 64
EPS = 1e-6

MLA_HEADS = 8
MLA_NOPE = 64
MLA_ROPE = 32
MLA_V = 64
MLA_Q_RANK = 256
MLA_KV_RANK = 128
MLA_WIDTH = MLA_HEADS * MLA_V
MLA_QK_DIM = MLA_NOPE + MLA_ROPE
ROPE_THETA = 10000.0

FOX_HEADS = 8
FOX_HEAD_DIM = 64
FOX_WIDTH = FOX_HEADS * FOX_HEAD_DIM

S5_WIDTH = 512
S5_GROUP = 16
S5_GROUPS = S5_WIDTH // S5_GROUP
S5_STATE = 64
S5_STATES = S5_GROUPS * S5_STATE

BRANCH_WIDTH = MLA_WIDTH + FOX_WIDTH + S5_WIDTH

LANE = 128
HEAD_PAD = 128
PAIR = 2 * HEAD_PAD
N_PAIRS = 4
N_SPLIT = 3

_OFF = np.cumsum([0, MLA_Q_RANK, MLA_KV_RANK, MLA_ROPE, FOX_WIDTH, FOX_WIDTH, FOX_WIDTH, FOX_HEADS,
                  S5_WIDTH, MLA_WIDTH, FOX_WIDTH, S5_WIDTH, D_MODEL, D_MODEL, D_MODEL]).tolist()
(O_CQ, O_CKV, O_KPE, O_FQ, O_FK, O_FV, O_FF, O_S5U, O_GATE, _o1, _o2, O_MERGE, _o3, _o4, O_END) = _OFF

C_CQ = 0
C_CKV = C_CQ + MLA_Q_RANK
C_KPA = C_CKV + MLA_KV_RANK
C_KPB = C_KPA + LANE
C_FQ = C_KPB + LANE
C_FK = C_FQ + FOX_WIDTH
C_FV = C_FK + FOX_WIDTH
C_FF = C_FV + FOX_WIDTH
C_S5 = C_FF + LANE
C_END = C_S5 + S5_WIDTH

VMEM_LIMIT = 56 * 1024 * 1024


def _cparams(sem):
    return pltpu.CompilerParams(dimension_semantics=sem, vmem_limit_bytes=VMEM_LIMIT)


def _full(shape):
    n = len(shape)
    return pl.BlockSpec(shape, lambda *_: (0,) * n)


def _rms(x, g):
    return x * lax.rsqrt(jnp.mean(x * x, axis=-1, keepdims=True) + EPS) * g


def _dot(a, b):
    return jnp.dot(a, b, preferred_element_type=F32)


def _rope_kernel(pos_ref, inv_ref, sign_ref, ca_ref, sb_ref):
    ang = pos_ref[0].astype(F32) * inv_ref[...]
    ca_ref[0] = jnp.cos(ang)
    sb_ref[0] = jnp.sin(ang) * sign_ref[...]


def _rope_tables(positions):
    B, S = positions.shape
    ts = min(S, 1024)
    inv = 1.0 / (ROPE_THETA ** (jnp.arange(0, MLA_ROPE, 2, dtype=F32) / MLA_ROPE))
    half = MLA_ROPE // 2
    inv_row = jnp.zeros((1, LANE), F32).at[0, MLA_NOPE:MLA_NOPE + half].set(inv)
    inv_row = inv_row.at[0, MLA_NOPE + half:MLA_NOPE + MLA_ROPE].set(inv)
    sign = np.zeros((1, LANE), np.float32)
    sign[0, MLA_NOPE:MLA_NOPE + half] = -1.0
    sign[0, MLA_NOPE + half:MLA_NOPE + MLA_ROPE] = 1.0
    blk = pl.BlockSpec((1, ts, LANE), lambda b, s: (b, s, 0))
    return pl.pallas_call(
        _rope_kernel,
        grid=(B, S // ts),
        in_specs=[pl.BlockSpec((1, ts, 1), lambda b, s: (b, s, 0)), _full((1, LANE)), _full((1, LANE))],
        out_specs=[blk, blk],
        out_shape=[jax.ShapeDtypeStruct((B, S, LANE), F32)] * 2,
        compiler_params=_cparams(("parallel", "parallel")),
        name="rope_tables",
    )(positions.reshape(B, S, 1), inv_row, jnp.asarray(sign))


def _prep_kernel(x_ref, ca_ref, sb_ref, ng_ref, wcat_ref, qan_ref, wqa_ref, wqb_ref, kvn_ref, wk_ref, wv_ref,
                 gqm_ref, gkm_ref, ones_ref, bf_ref, gqf_ref, gkf_ref, bd_ref, tri_ref,
                 eq_ref, ek_ref, oq_ref, ok_ref,
                 qm_ref, km_ref, vm_ref, qf_ref, kf_ref, vf_ref, u_ref, carry_ref):
    @pl.when(pl.program_id(1) == 0)
    def _():
        carry_ref[...] = jnp.zeros_like(carry_ref)

    x = x_ref[0]
    h = _rms(x, ng_ref[...]).astype(BF16)
    proj = _dot(h, wcat_ref[...])
    ca = ca_ref[0]
    sb = sb_ref[0]

    cqn = _rms(proj[:, C_CQ:C_CQ + MLA_Q_RANK], qan_ref[...]).astype(BF16)
    qa = _dot(cqn, wqa_ref[...])
    qb = _dot(cqn, wqb_ref[...])
    ckvn = _rms(proj[:, C_CKV:C_CKV + MLA_KV_RANK], kvn_ref[...]).astype(BF16)
    kc = _dot(ckvn, wk_ref[...])
    vm_ref[0] = _dot(ckvn, wv_ref[...]).astype(BF16)
    kpe = proj[:, C_KPA:C_KPA + LANE] * ca + proj[:, C_KPB:C_KPB + LANE] * sb
    ones = ones_ref[...]
    inv_d = 1.0 / MLA_QK_DIM
    for hd in range(MLA_HEADS):
        sl = slice(hd * HEAD_PAD, (hd + 1) * HEAD_PAD)
        qh = qa[:, sl] * ca + qb[:, sl] * sb
        ss = _dot((qh * qh).astype(BF16), ones)
        qm_ref[0, :, sl] = (qh * lax.rsqrt(ss * inv_d + EPS) * gqm_ref[...]).astype(BF16)
        kh = kc[:, sl] + kpe
        ss = _dot((kh * kh).astype(BF16), ones)
        km_ref[0, :, sl] = (kh * lax.rsqrt(ss * inv_d + EPS) * gkm_ref[...]).astype(BF16)

    bd = bd_ref[...]
    inv_d = 1.0 / FOX_HEAD_DIM
    fq = proj[:, C_FQ:C_FQ + FOX_WIDTH]
    ss = _dot((fq * fq).astype(BF16), bd)
    fqn = (fq * lax.rsqrt(ss * inv_d + EPS) * gqf_ref[...]).astype(BF16)
    fk = proj[:, C_FK:C_FK + FOX_WIDTH]
    ss = _dot((fk * fk).astype(BF16), bd)
    fkn = (fk * lax.rsqrt(ss * inv_d + EPS) * gkf_ref[...]).astype(BF16)
    vf_ref[0] = proj[:, C_FV:C_FV + FOX_WIDTH].astype(BF16)

    z = proj[:, C_FF:C_FF + LANE] + bf_ref[...]
    log_f = jnp.minimum(z, 0.0) - jnp.log1p(jnp.exp(-jnp.abs(z)))
    cum = jnp.dot(tri_ref[...], log_f, preferred_element_type=F32,
                  precision=lax.Precision.HIGHEST) + carry_ref[...]
    carry_ref[...] = cum[cum.shape[0] - 1:, :]
    hi = cum.astype(BF16)
    r1 = cum - hi.astype(F32)
    mid = r1.astype(BF16)
    lo = (r1 - mid.astype(F32)).astype(BF16)
    pieces = jnp.concatenate([hi, mid, lo], axis=1)
    auxq = (_dot(pieces, eq_ref[...]) + oq_ref[...]).astype(BF16)
    auxk = (_dot(pieces, ek_ref[...]) + ok_ref[...]).astype(BF16)
    for p in range(N_PAIRS):
        src = slice(p * LANE, (p + 1) * LANE)
        qf_ref[0, :, p * PAIR:p * PAIR + LANE] = fqn[:, src]
        qf_ref[0, :, p * PAIR + LANE:(p + 1) * PAIR] = auxq[:, src]
        kf_ref[0, :, p * PAIR:p * PAIR + LANE] = fkn[:, src]
        kf_ref[0, :, p * PAIR + LANE:(p + 1) * PAIR] = auxk[:, src]

    u_ref[0] = proj[:, C_S5:C_S5 + S5_WIDTH]


def _prep_weights(lw):
    w_in = lw["w_in"]
    zpad = jnp.zeros((D_MODEL, MLA_NOPE), F32)
    kpe = w_in[:, O_KPE:O_KPE + MLA_ROPE]
    half = MLA_ROPE // 2
    kpe_sw = jnp.concatenate([kpe[:, half:], kpe[:, :half]], axis=1)
    tail = jnp.zeros((D_MODEL, LANE - MLA_NOPE - MLA_ROPE), F32)
    ff = jnp.concatenate([w_in[:, O_FF:O_FF + FOX_HEADS], jnp.zeros((D_MODEL, LANE - FOX_HEADS), F32)], axis=1)
    wcat = jnp.concatenate([
        w_in[:, O_CQ:O_CQ + MLA_Q_RANK], w_in[:, O_CKV:O_CKV + MLA_KV_RANK],
        zpad, kpe, tail, zpad, kpe_sw, tail,
        w_in[:, O_FQ:O_FQ + 3 * FOX_WIDTH], ff, w_in[:, O_S5U:O_S5U + S5_WIDTH]], axis=1).astype(BF16)

    wq = lw["mla_w_q_up"].reshape(MLA_Q_RANK, MLA_HEADS, MLA_QK_DIM)
    nope, pe = wq[:, :, :MLA_NOPE], wq[:, :, MLA_NOPE:]
    pe_sw = jnp.concatenate([pe[:, :, half:], pe[:, :, :half]], axis=2)
    z32 = jnp.zeros((MLA_Q_RANK, MLA_HEADS, HEAD_PAD - MLA_QK_DIM), F32)
    wqa = jnp.concatenate([nope, pe, z32], axis=2).reshape(MLA_Q_RANK, MLA_HEADS * HEAD_PAD).astype(BF16)
    wqb = jnp.concatenate([jnp.zeros_like(nope), pe_sw, z32], axis=2).reshape(MLA_Q_RANK, -1).astype(BF16)

    wkv = lw["mla_w_kv_up"].reshape(MLA_KV_RANK, MLA_HEADS, MLA_NOPE + MLA_V)
    wk = jnp.concatenate([wkv[:, :, :MLA_NOPE], jnp.zeros((MLA_KV_RANK, MLA_HEADS, HEAD_PAD - MLA_NOPE), F32)],
                         axis=2).reshape(MLA_KV_RANK, -1).astype(BF16)
    wv = wkv[:, :, MLA_NOPE:].reshape(MLA_KV_RANK, MLA_WIDTH).astype(BF16)

    pad = jnp.zeros((HEAD_PAD - MLA_QK_DIM,), F32)
    gqm = (jnp.concatenate([lw["mla_q_norm"], pad]) * (1.0 / math.sqrt(MLA_QK_DIM)))[None]
    gkm = jnp.concatenate([lw["mla_k_norm"], pad])[None]
    bf = jnp.concatenate([lw["fox_b_f"], jnp.zeros((LANE - FOX_HEADS,), F32)])[None]
    gqf = (jnp.tile(lw["fox_q_norm"], FOX_HEADS) * (1.0 / math.sqrt(FOX_HEAD_DIM)))[None]
    gkf = jnp.tile(lw["fox_k_norm"], FOX_HEADS)[None]
    return dict(ng=lw["norm_g"][None], wcat=wcat, qan=lw["mla_q_a_norm"][None], wqa=wqa, wqb=wqb,
                kvn=lw["mla_kv_a_norm"][None], wk=wk, wv=wv, gqm=gqm, gkm=gkm, bf=bf, gqf=gqf, gkf=gkf)


def _prep_consts(tm):
    ones = np.ones((HEAD_PAD, HEAD_PAD), np.float32)
    bd = np.kron(np.eye(FOX_HEADS, dtype=np.float32), np.ones((FOX_HEAD_DIM, FOX_HEAD_DIM), np.float32))
    tri = np.tril(np.ones((tm, tm), np.float32))
    eq = np.zeros((N_SPLIT * LANE, N_PAIRS * LANE), np.float32)
    ek = np.zeros_like(eq)
    oq = np.zeros((1, N_PAIRS * LANE), np.float32)
    ok = np.zeros_like(oq)
    for hd in range(FOX_HEADS):
        p, a = divmod(hd, 2)
        base = p * LANE + 2 * N_SPLIT * a
        for i in range(N_SPLIT):
            ek[i * LANE + hd, base + i] = -1.0
            oq[0, base + i] = 1.0
            eq[i * LANE + hd, base + N_SPLIT + i] = 1.0
            ok[0, base + N_SPLIT + i] = 1.0
    return dict(ones=jnp.asarray(ones, BF16), bd=jnp.asarray(bd, BF16), tri=jnp.asarray(tri),
                eq=jnp.asarray(eq, BF16), ek=jnp.asarray(ek, BF16), oq=jnp.asarray(oq), ok=jnp.asarray(ok))


def _prep_call(x, ca, sb, w, c, tm):
    B, S, _ = x.shape
    tok = lambda width: pl.BlockSpec((1, tm, width), lambda b, s: (b, s, 0))
    args = [x, ca, sb, w["ng"], w["wcat"], w["qan"], w["wqa"], w["wqb"], w["kvn"], w["wk"], w["wv"],
            w["gqm"], w["gkm"], c["ones"], w["bf"], w["gqf"], w["gkf"], c["bd"], c["tri"],
            c["eq"], c["ek"], c["oq"], c["ok"]]
    in_specs = [tok(D_MODEL), tok(LANE), tok(LANE)] + [_full(a.shape) for a in args[3:]]
    widths = [MLA_HEADS * HEAD_PAD, MLA_HEADS * HEAD_PAD, MLA_WIDTH, N_PAIRS * PAIR, N_PAIRS * PAIR, FOX_WIDTH]
    out_shape = [jax.ShapeDtypeStruct((B, S, wd), BF16) for wd in widths]
    out_shape.append(jax.ShapeDtypeStruct((B, S, S5_WIDTH), F32))
    return pl.pallas_call(
        _prep_kernel,
        grid=(B, S // tm),
        in_specs=in_specs,
        out_specs=[tok(wd) for wd in widths] + [tok(S5_WIDTH)],
        out_shape=out_shape,
        scratch_shapes=[pltpu.VMEM((1, LANE), F32)],
        compiler_params=_cparams(("parallel", "arbitrary")),
        name="prep",
    )(*args)


def _attn_kernel(q_ref, k_ref, v_ref, hm_ref, o_ref, m_ref, l_ref, acc_ref, *, tq, tk, chunk_causal):
    S = q_ref.shape[1]
    nq = S // tq
    nd = tq // tk
    row = lax.broadcasted_iota(jnp.int32, (tq, tk), 0)
    col = lax.broadcasted_iota(jnp.int32, (tq, tk), 1)
    shift = int(math.log2(CHUNK)) if chunk_causal else 0
    delta = (col >> shift) - (row >> shift)
    lane = lax.broadcasted_iota(jnp.int32, (tq, LANE), 1)

    def q_block(qi, _):
        q0 = pl.multiple_of(qi * tq, tq)
        q = q_ref[0, pl.ds(q0, tq), :]
        qs = [q * hm_ref[a:a + 1, :] for a in range(2)]
        m_ref[...] = jnp.full(m_ref.shape, -jnp.inf, F32)
        l_ref[...] = jnp.zeros(l_ref.shape, F32)
        acc_ref[...] = jnp.zeros(acc_ref.shape, F32)

        def step(j, _, masked):
            k0 = pl.multiple_of(j * tk, tk)
            k = k_ref[0, pl.ds(k0, tk), :]
            v = v_ref[0, pl.ds(k0, tk), :]
            for a in range(2):
                s = lax.dot_general(qs[a], k, (((1,), (1,)), ((), ())), preferred_element_type=F32)
                if masked:
                    s = jnp.where(delta <= ((q0 - k0) >> shift), s, -jnp.inf)
                m_prev = m_ref[a]
                m_new = jnp.maximum(m_prev, jnp.max(s, axis=-1, keepdims=True))
                alpha = jnp.exp(m_prev - m_new)
                p = jnp.exp(s - pltpu.repeat(m_new, tk // LANE, axis=1))
                l_ref[a] = alpha * l_ref[a] + jnp.sum(p, axis=-1, keepdims=True)
                acc_ref[a] = alpha * acc_ref[a] + _dot(p.astype(BF16), v)
                m_ref[a] = m_new
            return 0

        n_full = qi * nd
        lax.fori_loop(0, n_full, functools.partial(step, masked=False), 0)
        lax.fori_loop(n_full, n_full + nd, functools.partial(step, masked=True), 0)
        out = jnp.where(lane < LANE // 2, acc_ref[0] / l_ref[0], acc_ref[1] / l_ref[1])
        o_ref[0, pl.ds(q0, tq), :] = out.astype(o_ref.dtype)
        return 0

    lax.fori_loop(0, nq, q_block, 0)


def _attn_call(q, k, v, head_mask, chunk_causal, tq, tk, name):
    B, S, _ = q.shape
    kern = functools.partial(_attn_kernel, tq=tq, tk=tk, chunk_causal=chunk_causal)
    return pl.pallas_call(
        kern,
        grid=(B, N_PAIRS),
        in_specs=[pl.BlockSpec((1, S, PAIR), lambda b, p: (b, 0, p)),
                  pl.BlockSpec((1, S, PAIR), lambda b, p: (b, 0, p)),
                  pl.BlockSpec((1, S, LANE), lambda b, p: (b, 0, p)),
                  _full((2, PAIR))],
        out_specs=pl.BlockSpec((1, S, LANE), lambda b, p: (b, 0, p)),
        out_shape=jax.ShapeDtypeStruct((B, S, N_PAIRS * LANE), BF16),
        scratch_shapes=[pltpu.VMEM((2, tq, LANE), F32)] * 3,
        compiler_params=_cparams(("parallel", "parallel")),
        name=name,
    )(q, k, v, head_mask)


def _head_masks():
    mla = np.zeros((2, PAIR), np.float32)
    fox = np.zeros((2, PAIR), np.float32)
    for a in range(2):
        mla[a, a * HEAD_PAD:(a + 1) * HEAD_PAD] = 1.0
        fox[a, a * FOX_HEAD_DIM:(a + 1) * FOX_HEAD_DIM] = 1.0
        fox[a, LANE + 2 * N_SPLIT * a:LANE + 2 * N_SPLIT * (a + 1)] = 1.0
    return jnp.asarray(mla, BF16), jnp.asarray(fox, BF16)


def _s5_param_kernel(lr_ref, li_ref, ldt_ref, br_ref, bi_ref, are_ref, aim_ref, bbr_ref, bbi_ref):
    lr = lr_ref[...]
    li = li_ref[...]
    dt = jnp.exp(ldt_ref[...])
    mag = jnp.exp(lr * dt)
    a_re = mag * jnp.cos(li * dt)
    a_im = mag * jnp.sin(li * dt)
    den = lr * lr + li * li
    f_re = ((a_re - 1.0) * lr + a_im * li) / den
    f_im = (a_im * lr - (a_re - 1.0) * li) / den
    br = br_ref[...]
    bi = bi_ref[...]
    are_ref[...] = a_re
    aim_ref[...] = a_im
    bbr_ref[...] = f_re * br - f_im * bi
    bbi_ref[...] = f_re * bi + f_im * br


def _s5_params(lw, n_batch):
    G, N, C = S5_GROUPS, S5_STATE, S5_GROUP
    rep = lambda a: jnp.repeat(a, C, axis=0)
    ldt = jnp.broadcast_to(lw["s5_log_dt"][:, None], (G, N))
    tr = lambda b: jnp.transpose(b, (0, 2, 1)).reshape(G * C, N)
    shp = jax.ShapeDtypeStruct((G * C, N), F32)
    a_re, a_im, bb_re, bb_im = pl.pallas_call(
        _s5_param_kernel, out_shape=[shp] * 4, name="s5_params",
    )(rep(lw["s5_lambda_re"]), rep(lw["s5_lambda_im"]), rep(ldt), tr(lw["s5_b_re"]), tr(lw["s5_b_im"]))
    eye = jnp.eye(G, dtype=F32)

    def blockdiag_in(bb):
        return (bb.reshape(G, C, 1, N) * eye[:, None, :, None]).reshape(G * C, G * N)

    def blockdiag_out(cc):
        return (jnp.transpose(cc, (0, 2, 1)).reshape(G, N, 1, C) * eye[:, None, :, None]).reshape(G * N, G * C)

    w_b = jnp.concatenate([blockdiag_in(bb_re), blockdiag_in(bb_im)], axis=1).astype(BF16)
    w_cre = blockdiag_out(lw["s5_c_re"]).astype(BF16)
    w_cim = blockdiag_out(lw["s5_c_im"]).astype(BF16)
    a_row = jnp.concatenate([a_re[::C].reshape(1, G * N), a_im[::C].reshape(1, G * N)], axis=1)
    a_rep = jnp.broadcast_to(a_row, (n_batch, 2 * G * N))
    return dict(w_b=w_b, w_cre=w_cre, w_cim=w_cim, a=a_rep, d=lw["s5_d"][None],
                w_glu=lw["s5_w_glu"].astype(BF16), b_glu=lw["s5_b_glu"][None])


def _s5_kernel(u_ref, a_ref, wb_ref, wcr_ref, wci_ref, d_ref, wg_ref, bg_ref, o_ref, st_ref, xs_ref, *, nb, ts):
    @pl.when(pl.program_id(0) == 0)
    def _():
        st_ref[...] = jnp.zeros_like(st_ref)

    N = S5_STATES
    u = u_ref[...]
    xs_ref[...] = _dot(u.astype(BF16), wb_ref[...])
    ar = a_ref[:, :N]
    ai = a_ref[:, N:]

    def step(t, carry):
        xr, xi = carry
        r = pl.multiple_of(t * nb, nb)
        br = xs_ref[pl.ds(r, nb), :N]
        bi = xs_ref[pl.ds(r, nb), N:]
        nr = ar * xr - ai * xi + br
        ni = ar * xi + ai * xr + bi
        xs_ref[pl.ds(r, nb), :N] = nr
        xs_ref[pl.ds(r, nb), N:] = ni
        return nr, ni

    xr, xi = lax.fori_loop(0, ts, step, (st_ref[:, :N], st_ref[:, N:]))
    st_ref[:, :N] = xr
    st_ref[:, N:] = xi

    y = _dot(xs_ref[:, :N].astype(BF16), wcr_ref[...]) - _dot(xs_ref[:, N:].astype(BF16), wci_ref[...])
    y = y + d_ref[...] * u
    z = jax.nn.gelu(y, approximate=True)
    gate = jax.nn.sigmoid(_dot(z.astype(BF16), wg_ref[...]) + bg_ref[...])
    o_ref[...] = (z * gate).astype(o_ref.dtype)


def _s5_call(u_tm, p, nb, ts):
    rows = u_tm.shape[0]
    blk = ts * nb
    args = [u_tm, p["a"], p["w_b"], p["w_cre"], p["w_cim"], p["d"], p["w_glu"], p["b_glu"]]
    kern = functools.partial(_s5_kernel, nb=nb, ts=ts)
    return pl.pallas_call(
        kern,
        grid=(rows // blk,),
        in_specs=[pl.BlockSpec((blk, S5_WIDTH), lambda i: (i, 0))] + [_full(a.shape) for a in args[1:]],
        out_specs=pl.BlockSpec((blk, S5_WIDTH), lambda i: (i, 0)),
        out_shape=jax.ShapeDtypeStruct((rows, S5_WIDTH), BF16),
        scratch_shapes=[pltpu.VMEM((nb, 2 * S5_STATES), F32), pltpu.VMEM((blk, 2 * S5_STATES), F32)],
        compiler_params=_cparams(("arbitrary",)),
        name="s5",
    )(*args)


def _merge_kernel(x_ref, ym_ref, yf_ref, ys_ref, ng_ref, wg_ref, wm_ref, wo_ref, wout_ref, o_ref):
    x = x_ref[0]
    h = _rms(x, ng_ref[...]).astype(BF16)
    merged = None
    for b, y_ref in enumerate((ym_ref, yf_ref, ys_ref)):
        g = _dot(h, wg_ref[:, b * MLA_WIDTH:(b + 1) * MLA_WIDTH])
        gated = (y_ref[0].astype(F32) * (g * jax.nn.sigmoid(g))).astype(BF16)
        o = _dot(gated, wo_ref[b * MLA_WIDTH:(b + 1) * MLA_WIDTH, :])
        m = _dot(h, wm_ref[:, b * D_MODEL:(b + 1) * D_MODEL])
        term = jax.nn.sigmoid(m) * o
        merged = term if merged is None else merged + term
    o_ref[0] = x + _dot(merged.astype(BF16), wout_ref[...])


def _merge_call(x, y_mla, y_fox, y_s5, lw, tm):
    B, S, _ = x.shape
    w_in = lw["w_in"]
    wg = w_in[:, O_GATE:O_GATE + BRANCH_WIDTH].astype(BF16)
    wm = w_in[:, O_MERGE:O_END].astype(BF16)
    wo = lw["w_branch_out"].astype(BF16)
    wout = lw["w_out"].astype(BF16)
    tok = lambda width: pl.BlockSpec((1, tm, width), lambda b, s: (b, s, 0))
    args = [x, y_mla, y_fox, y_s5, lw["norm_g"][None], wg, wm, wo, wout]
    return pl.pallas_call(
        _merge_kernel,
        grid=(B, S // tm),
        in_specs=[tok(D_MODEL), tok(MLA_WIDTH), tok(FOX_WIDTH), tok(S5_WIDTH)] + [_full(a.shape) for a in args[4:]],
        out_specs=tok(D_MODEL),
        out_shape=jax.ShapeDtypeStruct(x.shape, x.dtype),
        compiler_params=_cparams(("parallel", "parallel")),
        name="merge",
    )(*args)


_LAYER_KEYS = ("norm_g", "w_in", "mla_q_a_norm", "mla_w_q_up", "mla_kv_a_norm", "mla_w_kv_up", "mla_q_norm",
               "mla_k_norm", "fox_b_f", "fox_q_norm", "fox_k_norm", "s5_lambda_re", "s5_lambda_im", "s5_log_dt",
               "s5_b_re", "s5_b_im", "s5_c_re", "s5_c_im", "s5_d", "s5_w_glu", "s5_b_glu", "w_branch_out", "w_out")


def _forward(x, positions, params):
    B, S, _ = x.shape
    tm = min(S, 512)
    tq = min(S, 512)
    tk = min(S, 512)
    ts = min(S, 64)
    depth = params["w_in"].shape[0]
    ca, sb = _rope_tables(positions)
    consts = _prep_consts(tm)
    mask_mla, mask_fox = _head_masks()
    h = x
    for l in range(depth):
        lw = {k: params[k][l] for k in _LAYER_KEYS}
        qm, km, vm, qf, kf, vf, u = _prep_call(h, ca, sb, _prep_weights(lw), consts, tm)
        y_mla = _attn_call(qm, km, vm, mask_mla, True, tq, tk, "attn_mla")
        y_fox = _attn_call(qf, kf, vf, mask_fox, False, tq, tk, "attn_fox")
        u_tm = jnp.transpose(u, (1, 0, 2)).reshape(S * B, S5_WIDTH)
        z_tm = _s5_call(u_tm, _s5_params(lw, B), B, ts)
        y_s5 = jnp.transpose(z_tm.reshape(S, B, S5_WIDTH), (1, 0, 2))
        h = _merge_call(h, y_mla, y_fox, y_s5, lw, tm)
    return h


def kernel(x, positions, norm_g, w_in, mla_q_a_norm, mla_w_q_up, mla_kv_a_norm, mla_w_kv_up, mla_q_norm, mla_k_norm, fox_b_f, fox_q_norm, fox_k_norm, s5_lambda_re, s5_lambda_im, s5_log_dt, s5_b_re, s5_b_im, s5_c_re, s5_c_im, s5_d, s5_w_glu, s5_b_glu, w_branch_out, w_out):
    params = dict(zip(_LAYER_KEYS, (norm_g, w_in, mla_q_a_norm, mla_w_q_up, mla_kv_a_norm, mla_w_kv_up, mla_q_norm,
                                    mla_k_norm, fox_b_f, fox_q_norm, fox_k_norm, s5_lambda_re, s5_lambda_im,
                                    s5_log_dt, s5_b_re, s5_b_im, s5_c_re, s5_c_im, s5_d, s5_w_glu, s5_b_glu,
                                    w_branch_out, w_out)))
    return _forward(x, positions, params)
```

```python
import functools
import math

import numpy as np
import jax
import jax.numpy as jnp
from jax import lax
from jax.experimental import pallas as pl
from jax.experimental.pallas import tpu as pltpu

F32 = jnp.float32
BF16 = jnp.bfloat16

D_MODEL = 1024
CHUNK = 64
EPS = 1e-6

MLA_HEADS = 8
MLA_NOPE = 64
MLA_ROPE = 32
MLA_V = 64
MLA_Q_RANK = 256
MLA_KV_RANK = 128
MLA_WIDTH = MLA_HEADS * MLA_V
MLA_QK_DIM = MLA_NOPE + MLA_ROPE
ROPE_THETA = 10000.0

FOX_HEADS = 8
FOX_HEAD_DIM = 64
FOX_WIDTH = FOX_HEADS * FOX_HEAD_DIM

S5_WIDTH = 512
S5_GROUP = 16
S5_GROUPS = S5_WIDTH // S5_GROUP
S5_STATE = 64
S5_STATES = S5_GROUPS * S5_STATE

BRANCH_WIDTH = MLA_WIDTH + FOX_WIDTH + S5_WIDTH

LANE = 128
HEAD_PAD = 128
PAIR = 2 * HEAD_PAD
N_PAIRS = 4
N_SPLIT = 3
LOG2E = math.log2(math.e)

_OFF = np.cumsum([0, MLA_Q_RANK, MLA_KV_RANK, MLA_ROPE, FOX_WIDTH, FOX_WIDTH, FOX_WIDTH, FOX_HEADS,
                  S5_WIDTH, MLA_WIDTH, FOX_WIDTH, S5_WIDTH, D_MODEL, D_MODEL, D_MODEL]).tolist()
(O_CQ, O_CKV, O_KPE, O_FQ, O_FK, O_FV, O_FF, O_S5U, O_GATE, _o1, _o2, O_MERGE, _o3, _o4, O_END) = _OFF

C_CQ = 0
C_CKV = C_CQ + MLA_Q_RANK
C_KPA = C_CKV + MLA_KV_RANK
C_KPB = C_KPA + LANE
C_FQ = C_KPB + LANE
C_FK = C_FQ + FOX_WIDTH
C_FV = C_FK + FOX_WIDTH
C_FF = C_FV + FOX_WIDTH
C_S5 = C_FF + LANE
C_END = C_S5 + S5_WIDTH

VMEM_LIMIT = 56 * 1024 * 1024


def _cparams(sem):
    return pltpu.CompilerParams(dimension_semantics=sem, vmem_limit_bytes=VMEM_LIMIT)


def _full(shape):
    n = len(shape)
    return pl.BlockSpec(shape, lambda *_: (0,) * n)


def _rms(x, g):
    return x * lax.rsqrt(jnp.mean(x * x, axis=-1, keepdims=True) + EPS) * g


def _dot(a, b):
    return jnp.dot(a, b, preferred_element_type=F32)


def _store_v_with_ones(v_ref, v):
    lane = lax.broadcasted_iota(jnp.int32, (v.shape[0], LANE), 1)
    for p in range(N_PAIRS):
        vp = v[:, p * LANE:(p + 1) * LANE]
        v_ref[0, :, p * PAIR:p * PAIR + LANE] = jnp.where(lane < LANE // 2, vp, 1.0).astype(v_ref.dtype)
        v_ref[0, :, p * PAIR + LANE:(p + 1) * PAIR] = jnp.where(lane < LANE // 2, 1.0, vp).astype(v_ref.dtype)


def _rope_kernel(pos_ref, inv_ref, sign_ref, ca_ref, sb_ref):
    ang = pos_ref[0].astype(F32) * inv_ref[...]
    ca_ref[0] = jnp.cos(ang)
    sb_ref[0] = jnp.sin(ang) * sign_ref[...]


def _rope_tables(positions):
    B, S = positions.shape
    ts = min(S, 512)
    inv = 1.0 / (ROPE_THETA ** (jnp.arange(0, MLA_ROPE, 2, dtype=F32) / MLA_ROPE))
    half = MLA_ROPE // 2
    inv_row = jnp.zeros((1, LANE), F32).at[0, MLA_NOPE:MLA_NOPE + half].set(inv)
    inv_row = inv_row.at[0, MLA_NOPE + half:MLA_NOPE + MLA_ROPE].set(inv)
    sign = np.zeros((1, LANE), np.float32)
    sign[0, MLA_NOPE:MLA_NOPE + half] = -1.0
    sign[0, MLA_NOPE + half:MLA_NOPE + MLA_ROPE] = 1.0
    blk = pl.BlockSpec((1, ts, LANE), lambda b, s: (b, s, 0))
    return pl.pallas_call(
        _rope_kernel,
        grid=(B, S // ts),
        in_specs=[pl.BlockSpec((1, ts, 1), lambda b, s: (b, s, 0)), _full((1, LANE)), _full((1, LANE))],
        out_specs=[blk, blk],
        out_shape=[jax.ShapeDtypeStruct((B, S, LANE), F32)] * 2,
        compiler_params=_cparams(("parallel", "parallel")),
        name="rope_tables",
    )(positions.reshape(B, S, 1), inv_row, jnp.asarray(sign))


def _prep_kernel(x_ref, ca_ref, sb_ref, ng_ref, wcat_ref, qan_ref, wqa_ref, wqb_ref, kvn_ref, wk_ref, wv_ref,
                 gqm_ref, gkm_ref, ones_ref, bf_ref, gqf_ref, gkf_ref, bd_ref, tri_ref,
                 eq_ref, ek_ref, oq_ref, ok_ref,
                 qm_ref, km_ref, vm_ref, qf_ref, kf_ref, vf_ref, u_ref, carry_ref):
    @pl.when(pl.program_id(1) == 0)
    def _():
        carry_ref[...] = jnp.zeros_like(carry_ref)

    x = x_ref[0]
    h = _rms(x, ng_ref[...]).astype(BF16)
    proj = _dot(h, wcat_ref[...])
    ca = ca_ref[0]
    sb = sb_ref[0]

    cqn = _rms(proj[:, C_CQ:C_CQ + MLA_Q_RANK], qan_ref[...]).astype(BF16)
    qa = _dot(cqn, wqa_ref[...])
    qb = _dot(cqn, wqb_ref[...])
    ckvn = _rms(proj[:, C_CKV:C_CKV + MLA_KV_RANK], kvn_ref[...]).astype(BF16)
    kc = _dot(ckvn, wk_ref[...])
    _store_v_with_ones(vm_ref, _dot(ckvn, wv_ref[...]))
    kpe = proj[:, C_KPA:C_KPA + LANE] * ca + proj[:, C_KPB:C_KPB + LANE] * sb
    ones = ones_ref[...]
    inv_d = 1.0 / MLA_QK_DIM
    for hd in range(MLA_HEADS):
        sl = slice(hd * HEAD_PAD, (hd + 1) * HEAD_PAD)
        qh = qa[:, sl] * ca + qb[:, sl] * sb
        ss = _dot((qh * qh).astype(BF16), ones)
        qm_ref[0, :, sl] = (qh * lax.rsqrt(ss * inv_d + EPS) * gqm_ref[...]).astype(BF16)
        kh = kc[:, sl] + kpe
        ss = _dot((kh * kh).astype(BF16), ones)
        km_ref[0, :, sl] = (kh * lax.rsqrt(ss * inv_d + EPS) * gkm_ref[...]).astype(BF16)

    bd = bd_ref[...]
    inv_d = 1.0 / FOX_HEAD_DIM
    fq = proj[:, C_FQ:C_FQ + FOX_WIDTH]
    ss = _dot((fq * fq).astype(BF16), bd)
    fqn = (fq * lax.rsqrt(ss * inv_d + EPS) * gqf_ref[...]).astype(BF16)
    fk = proj[:, C_FK:C_FK + FOX_WIDTH]
    ss = _dot((fk * fk).astype(BF16), bd)
    fkn = (fk * lax.rsqrt(ss * inv_d + EPS) * gkf_ref[...]).astype(BF16)
    _store_v_with_ones(vf_ref, proj[:, C_FV:C_FV + FOX_WIDTH])

    z = proj[:, C_FF:C_FF + LANE] + bf_ref[...]
    log_f = jnp.minimum(z, 0.0) - jnp.log1p(jnp.exp(-jnp.abs(z)))
    cum = jnp.dot(tri_ref[...], log_f, preferred_element_type=F32,
                  precision=lax.Precision.HIGHEST) + carry_ref[...]
    carry_ref[...] = cum[cum.shape[0] - 1:, :]
    cum2 = cum * LOG2E
    hi = cum2.astype(BF16)
    r1 = cum2 - hi.astype(F32)
    mid = r1.astype(BF16)
    lo = (r1 - mid.astype(F32)).astype(BF16)
    pieces = jnp.concatenate([hi, mid, lo], axis=1)
    auxq = (_dot(pieces, eq_ref[...]) + oq_ref[...]).astype(BF16)
    auxk = (_dot(pieces, ek_ref[...]) + ok_ref[...]).astype(BF16)
    for p in range(N_PAIRS):
        src = slice(p * LANE, (p + 1) * LANE)
        qf_ref[0, :, p * PAIR:p * PAIR + LANE] = fqn[:, src]
        qf_ref[0, :, p * PAIR + LANE:(p + 1) * PAIR] = auxq[:, src]
        kf_ref[0, :, p * PAIR:p * PAIR + LANE] = fkn[:, src]
        kf_ref[0, :, p * PAIR + LANE:(p + 1) * PAIR] = auxk[:, src]

    u_ref[0] = proj[:, C_S5:C_S5 + S5_WIDTH]


def _prep_weights(lw):
    w_in = lw["w_in"]
    zpad = jnp.zeros((D_MODEL, MLA_NOPE), F32)
    kpe = w_in[:, O_KPE:O_KPE + MLA_ROPE]
    half = MLA_ROPE // 2
    kpe_sw = jnp.concatenate([kpe[:, half:], kpe[:, :half]], axis=1)
    tail = jnp.zeros((D_MODEL, LANE - MLA_NOPE - MLA_ROPE), F32)
    ff = jnp.concatenate([w_in[:, O_FF:O_FF + FOX_HEADS], jnp.zeros((D_MODEL, LANE - FOX_HEADS), F32)], axis=1)
    wcat = jnp.concatenate([
        w_in[:, O_CQ:O_CQ + MLA_Q_RANK], w_in[:, O_CKV:O_CKV + MLA_KV_RANK],
        zpad, kpe, tail, zpad, kpe_sw, tail,
        w_in[:, O_FQ:O_FQ + 3 * FOX_WIDTH], ff, w_in[:, O_S5U:O_S5U + S5_WIDTH]], axis=1).astype(BF16)

    wq = lw["mla_w_q_up"].reshape(MLA_Q_RANK, MLA_HEADS, MLA_QK_DIM)
    nope, pe = wq[:, :, :MLA_NOPE], wq[:, :, MLA_NOPE:]
    pe_sw = jnp.concatenate([pe[:, :, half:], pe[:, :, :half]], axis=2)
    z32 = jnp.zeros((MLA_Q_RANK, MLA_HEADS, HEAD_PAD - MLA_QK_DIM), F32)
    wqa = jnp.concatenate([nope, pe, z32], axis=2).reshape(MLA_Q_RANK, MLA_HEADS * HEAD_PAD).astype(BF16)
    wqb = jnp.concatenate([jnp.zeros_like(nope), pe_sw, z32], axis=2).reshape(MLA_Q_RANK, -1).astype(BF16)

    wkv = lw["mla_w_kv_up"].reshape(MLA_KV_RANK, MLA_HEADS, MLA_NOPE + MLA_V)
    wk = jnp.concatenate([wkv[:, :, :MLA_NOPE], jnp.zeros((MLA_KV_RANK, MLA_HEADS, HEAD_PAD - MLA_NOPE), F32)],
                         axis=2).reshape(MLA_KV_RANK, -1).astype(BF16)
    wv = wkv[:, :, MLA_NOPE:].reshape(MLA_KV_RANK, MLA_WIDTH).astype(BF16)

    pad = jnp.zeros((HEAD_PAD - MLA_QK_DIM,), F32)
    gqm = (jnp.concatenate([lw["mla_q_norm"], pad]) * (LOG2E / math.sqrt(MLA_QK_DIM)))[None]
    gkm = jnp.concatenate([lw["mla_k_norm"], pad])[None]
    bf = jnp.concatenate([lw["fox_b_f"], jnp.zeros((LANE - FOX_HEADS,), F32)])[None]
    gqf = (jnp.tile(lw["fox_q_norm"], FOX_HEADS) * (LOG2E / math.sqrt(FOX_HEAD_DIM)))[None]
    gkf = jnp.tile(lw["fox_k_norm"], FOX_HEADS)[None]
    return dict(ng=lw["norm_g"][None], wcat=wcat, qan=lw["mla_q_a_norm"][None], wqa=wqa, wqb=wqb,
                kvn=lw["mla_kv_a_norm"][None], wk=wk, wv=wv, gqm=gqm, gkm=gkm, bf=bf, gqf=gqf, gkf=gkf)


def _prep_consts(tm):
    ones = np.ones((HEAD_PAD, HEAD_PAD), np.float32)
    bd = np.kron(np.eye(FOX_HEADS, dtype=np.float32), np.ones((FOX_HEAD_DIM, FOX_HEAD_DIM), np.float32))
    tri = np.tril(np.ones((tm, tm), np.float32))
    eq = np.zeros((N_SPLIT * LANE, N_PAIRS * LANE), np.float32)
    ek = np.zeros_like(eq)
    oq = np.zeros((1, N_PAIRS * LANE), np.float32)
    ok = np.zeros_like(oq)
    for hd in range(FOX_HEADS):
        p, a = divmod(hd, 2)
        base = p * LANE + 2 * N_SPLIT * a
        for i in range(N_SPLIT):
            ek[i * LANE + hd, base + i] = -1.0
            oq[0, base + i] = 1.0
            eq[i * LANE + hd, base + N_SPLIT + i] = 1.0
            ok[0, base + N_SPLIT + i] = 1.0
    return dict(ones=jnp.asarray(ones, BF16), bd=jnp.asarray(bd, BF16), tri=jnp.asarray(tri),
                eq=jnp.asarray(eq, BF16), ek=jnp.asarray(ek, BF16), oq=jnp.asarray(oq), ok=jnp.asarray(ok))


def _prep_call(x, ca, sb, w, c, tm):
    B, S, _ = x.shape
    tok = lambda width: pl.BlockSpec((1, tm, width), lambda b, s: (b, s, 0))
    args = [x, ca, sb, w["ng"], w["wcat"], w["qan"], w["wqa"], w["wqb"], w["kvn"], w["wk"], w["wv"],
            w["gqm"], w["gkm"], c["ones"], w["bf"], w["gqf"], w["gkf"], c["bd"], c["tri"],
            c["eq"], c["ek"], c["oq"], c["ok"]]
    in_specs = [tok(D_MODEL), tok(LANE), tok(LANE)] + [_full(a.shape) for a in args[3:]]
    widths = [MLA_HEADS * HEAD_PAD, MLA_HEADS * HEAD_PAD, N_PAIRS * PAIR, N_PAIRS * PAIR, N_PAIRS * PAIR, N_PAIRS * PAIR]
    out_shape = [jax.ShapeDtypeStruct((B, S, wd), BF16) for wd in widths]
    out_shape.append(jax.ShapeDtypeStruct((B, S, S5_WIDTH), F32))
    return pl.pallas_call(
        _prep_kernel,
        grid=(B, S // tm),
        in_specs=in_specs,
        out_specs=[tok(wd) for wd in widths] + [tok(S5_WIDTH)],
        out_shape=out_shape,
        scratch_shapes=[pltpu.VMEM((1, LANE), F32)],
        compiler_params=_cparams(("parallel", "arbitrary")),
        name="prep",
    )(*args)


def _attn_kernel(q_ref, k_ref, v_ref, hm_ref, o_ref, qm_ref, sa_ref, sb_ref, bias_ref, m_ref, acc_ref, *,
                 tq, chunk_causal):
    S = q_ref.shape[1]
    nq = S // tq
    n_steps = nq * (nq + 1) // 2
    assert n_steps % 2 == 0 or n_steps == 1
    row = lax.broadcasted_iota(jnp.int32, (tq, tq), 0)
    col = lax.broadcasted_iota(jnp.int32, (tq, tq), 1)
    shift = int(math.log2(CHUNK)) if chunk_causal else 0
    bias_ref[0] = jnp.zeros((tq, tq), F32)
    bias_ref[1] = jnp.where((col >> shift) <= (row >> shift), 0.0, -jnp.inf)
    for a in range(2):
        qm_ref[a] = q_ref[0] * hm_ref[a:a + 1, :]
    acc_ref[...] = jnp.zeros(acc_ref.shape, F32)
    lane = lax.broadcasted_iota(jnp.int32, (tq, LANE), 1)

    def qk(qi, j, s_ref):
        k = k_ref[0, pl.ds(pl.multiple_of(j * tq, tq), tq), :]
        for a in range(2):
            q = qm_ref[a, pl.ds(pl.multiple_of(qi * tq, tq), tq), :]
            s_ref[a] = lax.dot_general(q, k, (((1,), (1,)), ((), ())), preferred_element_type=F32)

    def step(carry, s_cur, s_next):
        qi, j = carry
        last = j == qi
        qi_next = jnp.minimum(qi + last.astype(jnp.int32), nq - 1)
        j_next = jnp.where(last, 0, j + 1)
        qk(qi_next, j_next, s_next)

        bias = bias_ref[last.astype(jnp.int32)]
        k0 = pl.multiple_of(j * tq, tq)
        outs = []
        for a in range(2):
            s = s_cur[a] + bias
            m_prev = jnp.where(j == 0, -jnp.inf, m_ref[a])
            m_new = jnp.maximum(m_prev, jnp.max(s, axis=-1, keepdims=True))
            alpha = jnp.exp2(m_prev - m_new)
            p = jnp.exp2(s - jnp.concatenate([m_new] * (tq // LANE), axis=1))
            v = v_ref[0, pl.ds(k0, tq), a * LANE:(a + 1) * LANE]
            acc = alpha * acc_ref[a] + _dot(p.astype(BF16), v)
            acc_ref[a] = acc
            m_ref[a] = m_new
            outs.append(acc / pltpu.roll(acc, LANE // 2, axis=1))
        q0 = pl.multiple_of(qi * tq, tq)
        o_ref[0, pl.ds(q0, tq), :] = jnp.where(lane < LANE // 2, outs[0], outs[1]).astype(o_ref.dtype)
        return qi_next, j_next

    qk(0, 0, sa_ref)
    start = (jnp.int32(0), jnp.int32(0))
    if n_steps == 1:
        step(start, sa_ref, sb_ref)
    else:
        lax.fori_loop(0, n_steps // 2, lambda _, c: step(step(c, sa_ref, sb_ref), sb_ref, sa_ref), start)


def _attn_call(q, k, v, head_mask, chunk_causal, tq, name):
    B, S, _ = q.shape
    kern = functools.partial(_attn_kernel, tq=tq, chunk_causal=chunk_causal)
    return pl.pallas_call(
        kern,
        grid=(B, N_PAIRS),
        in_specs=[pl.BlockSpec((1, S, PAIR), lambda b, p: (b, 0, p)),
                  pl.BlockSpec((1, S, PAIR), lambda b, p: (b, 0, p)),
                  pl.BlockSpec((1, S, PAIR), lambda b, p: (b, 0, p)),
                  _full((2, PAIR))],
        out_specs=pl.BlockSpec((1, S, LANE), lambda b, p: (b, 0, p)),
        out_shape=jax.ShapeDtypeStruct((B, S, N_PAIRS * LANE), BF16),
        scratch_shapes=[pltpu.VMEM((2, S, PAIR), BF16), pltpu.VMEM((2, tq, tq), F32), pltpu.VMEM((2, tq, tq), F32),
                        pltpu.VMEM((2, tq, tq), F32), pltpu.VMEM((2, tq, LANE), F32), pltpu.VMEM((2, tq, LANE), F32)],
        compiler_params=_cparams(("parallel", "parallel")),
        name=name,
    )(q, k, v, head_mask)


def _head_masks():
    mla = np.zeros((2, PAIR), np.float32)
    fox = np.zeros((2, PAIR), np.float32)
    for a in range(2):
        mla[a, a * HEAD_PAD:(a + 1) * HEAD_PAD] = 1.0
        fox[a, a * FOX_HEAD_DIM:(a + 1) * FOX_HEAD_DIM] = 1.0
        fox[a, LANE + 2 * N_SPLIT * a:LANE + 2 * N_SPLIT * (a + 1)] = 1.0
    return jnp.asarray(mla, BF16), jnp.asarray(fox, BF16)


def _s5_param_kernel(lr_ref, li_ref, ldt_ref, br_ref, bi_ref, are_ref, aim_ref, bbr_ref, bbi_ref):
    lr = lr_ref[...]
    li = li_ref[...]
    dt = jnp.exp(ldt_ref[...])
    mag = jnp.exp(lr * dt)
    a_re = mag * jnp.cos(li * dt)
    a_im = mag * jnp.sin(li * dt)
    den = lr * lr + li * li
    f_re = ((a_re - 1.0) * lr + a_im * li) / den
    f_im = (a_im * lr - (a_re - 1.0) * li) / den
    br = br_ref[...]
    bi = bi_ref[...]
    are_ref[...] = a_re
    aim_ref[...] = a_im
    bbr_ref[...] = f_re * br - f_im * bi
    bbi_ref[...] = f_re * bi + f_im * br


def _s5_params(lw, n_batch):
    G, N, C = S5_GROUPS, S5_STATE, S5_GROUP
    rep = lambda a: jnp.repeat(a, C, axis=0)
    ldt = jnp.broadcast_to(lw["s5_log_dt"][:, None], (G, N))
    tr = lambda b: jnp.transpose(b, (0, 2, 1)).reshape(G * C, N)
    shp = jax.ShapeDtypeStruct((G * C, N), F32)
    a_re, a_im, bb_re, bb_im = pl.pallas_call(
        _s5_param_kernel, out_shape=[shp] * 4, name="s5_params",
    )(rep(lw["s5_lambda_re"]), rep(lw["s5_lambda_im"]), rep(ldt), tr(lw["s5_b_re"]), tr(lw["s5_b_im"]))
    eye = jnp.eye(G, dtype=F32)

    def blockdiag_in(bb):
        return (bb.reshape(G, C, 1, N) * eye[:, None, :, None]).reshape(G * C, G * N)

    def blockdiag_out(cc):
        return (jnp.transpose(cc, (0, 2, 1)).reshape(G, N, 1, C) * eye[:, None, :, None]).reshape(G * N, G * C)

    w_b = jnp.concatenate([blockdiag_in(bb_re), blockdiag_in(bb_im)], axis=1).astype(BF16)
    w_cre = blockdiag_out(lw["s5_c_re"]).astype(BF16)
    w_cim = blockdiag_out(lw["s5_c_im"]).astype(BF16)
    a_row = jnp.concatenate([a_re[::C].reshape(1, G * N), a_im[::C].reshape(1, G * N)], axis=1)
    a_rep = jnp.broadcast_to(a_row, (n_batch, 2 * G * N))
    return dict(w_b=w_b, w_cre=w_cre, w_cim=w_cim, a=a_rep, d=lw["s5_d"][None],
                w_glu=lw["s5_w_glu"].astype(BF16), b_glu=lw["s5_b_glu"][None])


def _s5_kernel(u_ref, a_ref, wb_ref, wcr_ref, wci_ref, d_ref, wg_ref, bg_ref, o_ref, st_ref, xs_ref, *, nb, ts):
    @pl.when(pl.program_id(0) == 0)
    def _():
        st_ref[...] = jnp.zeros_like(st_ref)

    N = S5_STATES
    u = u_ref[...]
    xs_ref[...] = _dot(u.astype(BF16), wb_ref[...])
    ar = a_ref[:, :N]
    ai = a_ref[:, N:]

    def step(t, carry):
        xr, xi = carry
        r = pl.multiple_of(t * nb, nb)
        br = xs_ref[pl.ds(r, nb), :N]
        bi = xs_ref[pl.ds(r, nb), N:]
        nr = ar * xr - ai * xi + br
        ni = ar * xi + ai * xr + bi
        xs_ref[pl.ds(r, nb), :N] = nr
        xs_ref[pl.ds(r, nb), N:] = ni
        return nr, ni

    xr, xi = lax.fori_loop(0, ts, step, (st_ref[:, :N], st_ref[:, N:]))
    st_ref[:, :N] = xr
    st_ref[:, N:] = xi

    y = _dot(xs_ref[:, :N].astype(BF16), wcr_ref[...]) - _dot(xs_ref[:, N:].astype(BF16), wci_ref[...])
    y = y + d_ref[...] * u
    z = jax.nn.gelu(y, approximate=True)
    gate = jax.nn.sigmoid(_dot(z.astype(BF16), wg_ref[...]) + bg_ref[...])
    o_ref[...] = (z * gate).astype(o_ref.dtype)


def _s5_call(u_tm, p, nb, ts):
    rows = u_tm.shape[0]
    blk = ts * nb
    args = [u_tm, p["a"], p["w_b"], p["w_cre"], p["w_cim"], p["d"], p["w_glu"], p["b_glu"]]
    kern = functools.partial(_s5_kernel, nb=nb, ts=ts)
    return pl.pallas_call(
        kern,
        grid=(rows // blk,),
        in_specs=[pl.BlockSpec((blk, S5_WIDTH), lambda i: (i, 0))] + [_full(a.shape) for a in args[1:]],
        out_specs=pl.BlockSpec((blk, S5_WIDTH), lambda i: (i, 0)),
        out_shape=jax.ShapeDtypeStruct((rows, S5_WIDTH), BF16),
        scratch_shapes=[pltpu.VMEM((nb, 2 * S5_STATES), F32), pltpu.VMEM((blk, 2 * S5_STATES), F32)],
        compiler_params=_cparams(("arbitrary",)),
        name="s5",
    )(*args)


def _merge_kernel(x_ref, ym_ref, yf_ref, ys_ref, ng_ref, wg_ref, wm_ref, wo_ref, wout_ref, o_ref):
    x = x_ref[0]
    h = _rms(x, ng_ref[...]).astype(BF16)
    merged = None
    for b, y_ref in enumerate((ym_ref, yf_ref, ys_ref)):
        g = _dot(h, wg_ref[:, b * MLA_WIDTH:(b + 1) * MLA_WIDTH])
        gated = (y_ref[0].astype(F32) * (g * jax.nn.sigmoid(g))).astype(BF16)
        o = _dot(gated, wo_ref[b * MLA_WIDTH:(b + 1) * MLA_WIDTH, :])
        m = _dot(h, wm_ref[:, b * D_MODEL:(b + 1) * D_MODEL])
        term = jax.nn.sigmoid(m) * o
        merged = term if merged is None else merged + term
    o_ref[0] = x + _dot(merged.astype(BF16), wout_ref[...])


def _merge_call(x, y_mla, y_fox, y_s5, lw, tm):
    B, S, _ = x.shape
    w_in = lw["w_in"]
    wg = w_in[:, O_GATE:O_GATE + BRANCH_WIDTH].astype(BF16)
    wm = w_in[:, O_MERGE:O_END].astype(BF16)
    wo = lw["w_branch_out"].astype(BF16)
    wout = lw["w_out"].astype(BF16)
    tok = lambda width: pl.BlockSpec((1, tm, width), lambda b, s: (b, s, 0))
    args = [x, y_mla, y_fox, y_s5, lw["norm_g"][None], wg, wm, wo, wout]
    return pl.pallas_call(
        _merge_kernel,
        grid=(B, S // tm),
        in_specs=[tok(D_MODEL), tok(MLA_WIDTH), tok(FOX_WIDTH), tok(S5_WIDTH)] + [_full(a.shape) for a in args[4:]],
        out_specs=tok(D_MODEL),
        out_shape=jax.ShapeDtypeStruct(x.shape, x.dtype),
        compiler_params=_cparams(("parallel", "parallel")),
        name="merge",
    )(*args)


_LAYER_KEYS = ("norm_g", "w_in", "mla_q_a_norm", "mla_w_q_up", "mla_kv_a_norm", "mla_w_kv_up", "mla_q_norm",
               "mla_k_norm", "fox_b_f", "fox_q_norm", "fox_k_norm", "s5_lambda_re", "s5_lambda_im", "s5_log_dt",
               "s5_b_re", "s5_b_im", "s5_c_re", "s5_c_im", "s5_d", "s5_w_glu", "s5_b_glu", "w_branch_out", "w_out")


def _forward(x, positions, params):
    B, S, _ = x.shape
    tm = min(S, 512)
    tq = min(S, 512)
    ts = min(S, 64)
    depth = params["w_in"].shape[0]
    ca, sb = _rope_tables(positions)
    consts = _prep_consts(tm)
    mask_mla, mask_fox = _head_masks()
    h = x
    for l in range(depth):
        lw = {k: params[k][l] for k in _LAYER_KEYS}
        qm, km, vm, qf, kf, vf, u = _prep_call(h, ca, sb, _prep_weights(lw), consts, tm)
        y_mla = _attn_call(qm, km, vm, mask_mla, True, tq, "attn_mla")
        y_fox = _attn_call(qf, kf, vf, mask_fox, False, tq, "attn_fox")
        u_tm = jnp.transpose(u, (1, 0, 2)).reshape(S * B, S5_WIDTH)
        z_tm = _s5_call(u_tm, _s5_params(lw, B), B, ts)
        y_s5 = jnp.transpose(z_tm.reshape(S, B, S5_WIDTH), (1, 0, 2))
        h = _merge_call(h, y_mla, y_fox, y_s5, lw, tm)
    return h


def kernel(x, positions, norm_g, w_in, mla_q_a_norm, mla_w_q_up, mla_kv_a_norm, mla_w_kv_up, mla_q_norm, mla_k_norm, fox_b_f, fox_q_norm, fox_k_norm, s5_lambda_re, s5_lambda_im, s5_log_dt, s5_b_re, s5_b_im, s5_c_re, s5_c_im, s5_d, s5_w_glu, s5_b_glu, w_branch_out, w_out):
    params = dict(zip(_LAYER_KEYS, (norm_g, w_in, mla_q_a_norm, mla_w_q_up, mla_kv_a_norm, mla_w_kv_up, mla_q_norm,
                                    mla_k_norm, fox_b_f, fox_q_norm, fox_k_norm, s5_lambda_re, s5_lambda_im,
                                    s5_log_dt, s5_b_re, s5_b_im, s5_c_re, s5_c_im, s5_d, s5_w_glu, s5_b_glu,
                                    w_branch_out, w_out)))
    return _forward(x, positions, params)
```

```python
import functools
import math

import numpy as np
import jax
import jax.numpy as jnp
from jax import lax
from jax.experimental import pallas as pl
from jax.experimental.pallas import tpu as pltpu

F32 = jnp.float32
BF16 = jnp.bfloat16

D_MODEL = 1024
CHUNK = 64
EPS = 1e-6

MLA_HEADS = 8
MLA_NOPE = 64
MLA_ROPE = 32
MLA_V = 64
MLA_Q_RANK = 256
MLA_KV_RANK = 128
MLA_WIDTH = MLA_HEADS * MLA_V
MLA_QK_DIM = MLA_NOPE + MLA_ROPE
ROPE_THETA = 10000.0

FOX_HEADS = 8
FOX_HEAD_DIM = 64
FOX_WIDTH = FOX_HEADS * FOX_HEAD_DIM

S5_WIDTH = 512
S5_GROUP = 16
S5_GROUPS = S5_WIDTH // S5_GROUP
S5_STATE = 64
S5_STATES = S5_GROUPS * S5_STATE

BRANCH_WIDTH = MLA_WIDTH + FOX_WIDTH + S5_WIDTH

LANE = 128
HEAD_PAD = 128
PAIR = 2 * HEAD_PAD
N_PAIRS = 4
N_SPLIT = 3
LOG2E = math.log2(math.e)

_OFF = np.cumsum([0, MLA_Q_RANK, MLA_KV_RANK, MLA_ROPE, FOX_WIDTH, FOX_WIDTH, FOX_WIDTH, FOX_HEADS,
                  S5_WIDTH, MLA_WIDTH, FOX_WIDTH, S5_WIDTH, D_MODEL, D_MODEL, D_MODEL]).tolist()
(O_CQ, O_CKV, O_KPE, O_FQ, O_FK, O_FV, O_FF, O_S5U, O_GATE, _o1, _o2, O_MERGE, _o3, _o4, O_END) = _OFF

C_CQ = 0
C_CKV = C_CQ + MLA_Q_RANK
C_KPA = C_CKV + MLA_KV_RANK
C_KPB = C_KPA + LANE
C_FQ = C_KPB + LANE
C_FK = C_FQ + FOX_WIDTH
C_FV = C_FK + FOX_WIDTH
C_FF = C_FV + FOX_WIDTH
C_S5 = C_FF + LANE
C_END = C_S5 + S5_WIDTH

VMEM_LIMIT = 56 * 1024 * 1024


def _cparams(sem):
    return pltpu.CompilerParams(dimension_semantics=sem, vmem_limit_bytes=VMEM_LIMIT)


def _full(shape):
    n = len(shape)
    return pl.BlockSpec(shape, lambda *_: (0,) * n)


def _rms(x, g):
    return x * lax.rsqrt(jnp.mean(x * x, axis=-1, keepdims=True) + EPS) * g


def _dot(a, b):
    return jnp.dot(a, b, preferred_element_type=F32)


def _store_v_with_ones(v_ref, v):
    lane = lax.broadcasted_iota(jnp.int32, (v.shape[0], LANE), 1)
    for p in range(N_PAIRS):
        vp = v[:, p * LANE:(p + 1) * LANE]
        v_ref[0, :, p * PAIR:p * PAIR + LANE] = jnp.where(lane < LANE // 2, vp, 1.0).astype(v_ref.dtype)
        v_ref[0, :, p * PAIR + LANE:(p + 1) * PAIR] = jnp.where(lane < LANE // 2, 1.0, vp).astype(v_ref.dtype)


def _rope_kernel(pos_ref, inv_ref, sign_ref, ca_ref, sb_ref):
    ang = pos_ref[0].astype(F32) * inv_ref[...]
    ca_ref[0] = jnp.cos(ang)
    sb_ref[0] = jnp.sin(ang) * sign_ref[...]


def _rope_tables(positions):
    B, S = positions.shape
    ts = min(S, 512)
    inv = 1.0 / (ROPE_THETA ** (jnp.arange(0, MLA_ROPE, 2, dtype=F32) / MLA_ROPE))
    half = MLA_ROPE // 2
    inv_row = jnp.zeros((1, LANE), F32).at[0, MLA_NOPE:MLA_NOPE + half].set(inv)
    inv_row = inv_row.at[0, MLA_NOPE + half:MLA_NOPE + MLA_ROPE].set(inv)
    sign = np.zeros((1, LANE), np.float32)
    sign[0, MLA_NOPE:MLA_NOPE + half] = -1.0
    sign[0, MLA_NOPE + half:MLA_NOPE + MLA_ROPE] = 1.0
    blk = pl.BlockSpec((1, ts, LANE), lambda b, s: (b, s, 0))
    return pl.pallas_call(
        _rope_kernel,
        grid=(B, S // ts),
        in_specs=[pl.BlockSpec((1, ts, 1), lambda b, s: (b, s, 0)), _full((1, LANE)), _full((1, LANE))],
        out_specs=[blk, blk],
        out_shape=[jax.ShapeDtypeStruct((B, S, LANE), F32)] * 2,
        compiler_params=_cparams(("parallel", "parallel")),
        name="rope_tables",
    )(positions.reshape(B, S, 1), inv_row, jnp.asarray(sign))


def _split3(x):
    hi = x.astype(BF16)
    r1 = x - hi.astype(F32)
    mid = r1.astype(BF16)
    return hi, mid, (r1 - mid.astype(F32)).astype(BF16)


def _fox_qk_norm(x, bd, gain):
    sq = (x * x).astype(BF16)
    ss = jnp.concatenate([_dot(sq[:, c:c + PAIR], bd) for c in range(0, FOX_WIDTH, PAIR)], axis=1)
    return (x * lax.rsqrt(ss * (1.0 / FOX_HEAD_DIM) + EPS) * gain).astype(BF16)


def _prep_kernel(x_ref, ca_ref, sb_ref, ng_ref, wcat_ref, qan_ref, wqa_ref, wqb_ref, kvn_ref, wk_ref, wv_ref,
                 gqm_ref, gkm_ref, bf_ref, gqf_ref, gkf_ref, bd_ref, tri_ref,
                 eq_ref, ek_ref, oq_ref, ok_ref,
                 qm_ref, km_ref, vm_ref, qf_ref, kf_ref, vf_ref, u_ref, carry_ref):
    @pl.when(pl.program_id(1) == 0)
    def _():
        carry_ref[...] = jnp.zeros_like(carry_ref)

    x = x_ref[0]
    h = _rms(x, ng_ref[...]).astype(BF16)
    proj = _dot(h, wcat_ref[...])
    ca = ca_ref[0]
    sb = sb_ref[0]

    cqn = _rms(proj[:, C_CQ:C_CQ + MLA_Q_RANK], qan_ref[...]).astype(BF16)
    qa = _dot(cqn, wqa_ref[...])
    qb = _dot(cqn, wqb_ref[...])
    ckvn = _rms(proj[:, C_CKV:C_CKV + MLA_KV_RANK], kvn_ref[...]).astype(BF16)
    kc = _dot(ckvn, wk_ref[...])
    _store_v_with_ones(vm_ref, _dot(ckvn, wv_ref[...]))
    kpe = proj[:, C_KPA:C_KPA + LANE] * ca + proj[:, C_KPB:C_KPB + LANE] * sb
    inv_d = 1.0 / MLA_QK_DIM
    for hd in range(MLA_HEADS):
        sl = slice(hd * HEAD_PAD, (hd + 1) * HEAD_PAD)
        qh = qa[:, sl] * ca + qb[:, sl] * sb
        ss = jnp.sum(qh * qh, axis=-1, keepdims=True)
        qm_ref[0, :, sl] = (qh * lax.rsqrt(ss * inv_d + EPS) * gqm_ref[...]).astype(BF16)
        kh = kc[:, sl] + kpe
        ss = jnp.sum(kh * kh, axis=-1, keepdims=True)
        km_ref[0, :, sl] = (kh * lax.rsqrt(ss * inv_d + EPS) * gkm_ref[...]).astype(BF16)

    fqn = _fox_qk_norm(proj[:, C_FQ:C_FQ + FOX_WIDTH], bd_ref[...], gqf_ref[...])
    fkn = _fox_qk_norm(proj[:, C_FK:C_FK + FOX_WIDTH], bd_ref[...], gkf_ref[...])
    _store_v_with_ones(vf_ref, proj[:, C_FV:C_FV + FOX_WIDTH])

    z = proj[:, C_FF:C_FF + LANE] + bf_ref[...]
    log_f = jnp.minimum(z, 0.0) - jnp.log1p(jnp.exp(-jnp.abs(z)))
    sums = _dot(tri_ref[...], jnp.concatenate(_split3(log_f), axis=1))
    cum = sums[:, :LANE] + sums[:, LANE:2 * LANE] + sums[:, 2 * LANE:] + carry_ref[...]
    carry_ref[...] = cum[cum.shape[0] - 1:, :]
    hi, mid, lo = _split3(cum * LOG2E)
    lane = lax.broadcasted_iota(jnp.int32, hi.shape, 1)
    pieces = jnp.where(lane < FOX_HEADS, hi, jnp.where(lane < 2 * FOX_HEADS, mid, lo))
    auxq = (_dot(pieces, eq_ref[...]) + oq_ref[...]).astype(BF16)
    auxk = (_dot(pieces, ek_ref[...]) + ok_ref[...]).astype(BF16)
    for p in range(N_PAIRS):
        src = slice(p * LANE, (p + 1) * LANE)
        qf_ref[0, :, p * PAIR:p * PAIR + LANE] = fqn[:, src]
        qf_ref[0, :, p * PAIR + LANE:(p + 1) * PAIR] = auxq[:, src]
        kf_ref[0, :, p * PAIR:p * PAIR + LANE] = fkn[:, src]
        kf_ref[0, :, p * PAIR + LANE:(p + 1) * PAIR] = auxk[:, src]

    u_ref[0] = proj[:, C_S5:C_S5 + S5_WIDTH]


def _prep_weights(lw):
    w_in = lw["w_in"]
    zpad = jnp.zeros((D_MODEL, MLA_NOPE), F32)
    kpe = w_in[:, O_KPE:O_KPE + MLA_ROPE]
    half = MLA_ROPE // 2
    kpe_sw = jnp.concatenate([kpe[:, half:], kpe[:, :half]], axis=1)
    tail = jnp.zeros((D_MODEL, LANE - MLA_NOPE - MLA_ROPE), F32)
    ff = jnp.concatenate([w_in[:, O_FF:O_FF + FOX_HEADS]] * N_SPLIT
                         + [jnp.zeros((D_MODEL, LANE - N_SPLIT * FOX_HEADS), F32)], axis=1)
    wcat = jnp.concatenate([
        w_in[:, O_CQ:O_CQ + MLA_Q_RANK], w_in[:, O_CKV:O_CKV + MLA_KV_RANK],
        zpad, kpe, tail, zpad, kpe_sw, tail,
        w_in[:, O_FQ:O_FQ + 3 * FOX_WIDTH], ff, w_in[:, O_S5U:O_S5U + S5_WIDTH]], axis=1).astype(BF16)

    wq = lw["mla_w_q_up"].reshape(MLA_Q_RANK, MLA_HEADS, MLA_QK_DIM)
    nope, pe = wq[:, :, :MLA_NOPE], wq[:, :, MLA_NOPE:]
    pe_sw = jnp.concatenate([pe[:, :, half:], pe[:, :, :half]], axis=2)
    z32 = jnp.zeros((MLA_Q_RANK, MLA_HEADS, HEAD_PAD - MLA_QK_DIM), F32)
    wqa = jnp.concatenate([nope, pe, z32], axis=2).reshape(MLA_Q_RANK, MLA_HEADS * HEAD_PAD).astype(BF16)
    wqb = jnp.concatenate([jnp.zeros_like(nope), pe_sw, z32], axis=2).reshape(MLA_Q_RANK, -1).astype(BF16)

    wkv = lw["mla_w_kv_up"].reshape(MLA_KV_RANK, MLA_HEADS, MLA_NOPE + MLA_V)
    wk = jnp.concatenate([wkv[:, :, :MLA_NOPE], jnp.zeros((MLA_KV_RANK, MLA_HEADS, HEAD_PAD - MLA_NOPE), F32)],
                         axis=2).reshape(MLA_KV_RANK, -1).astype(BF16)
    wv = wkv[:, :, MLA_NOPE:].reshape(MLA_KV_RANK, MLA_WIDTH).astype(BF16)

    pad = jnp.zeros((HEAD_PAD - MLA_QK_DIM,), F32)
    gqm = (jnp.concatenate([lw["mla_q_norm"], pad]) * (LOG2E / math.sqrt(MLA_QK_DIM)))[None]
    gkm = jnp.concatenate([lw["mla_k_norm"], pad])[None]
    bf = jnp.concatenate([lw["fox_b_f"]] * N_SPLIT + [jnp.zeros((LANE - N_SPLIT * FOX_HEADS,), F32)])[None]
    gqf = (jnp.tile(lw["fox_q_norm"], FOX_HEADS) * (LOG2E / math.sqrt(FOX_HEAD_DIM)))[None]
    gkf = jnp.tile(lw["fox_k_norm"], FOX_HEADS)[None]
    return dict(ng=lw["norm_g"][None], wcat=wcat, qan=lw["mla_q_a_norm"][None], wqa=wqa, wqb=wqb,
                kvn=lw["mla_kv_a_norm"][None], wk=wk, wv=wv, gqm=gqm, gkm=gkm, bf=bf, gqf=gqf, gkf=gkf)


def _prep_consts(tm):
    bd = np.kron(np.eye(PAIR // FOX_HEAD_DIM, dtype=np.float32), np.ones((FOX_HEAD_DIM, FOX_HEAD_DIM), np.float32))
    tri = np.tril(np.ones((tm, tm), np.float32))
    eq = np.zeros((LANE, N_PAIRS * LANE), np.float32)
    ek = np.zeros_like(eq)
    oq = np.zeros((1, N_PAIRS * LANE), np.float32)
    ok = np.zeros_like(oq)
    for hd in range(FOX_HEADS):
        p, a = divmod(hd, 2)
        base = p * LANE + 2 * N_SPLIT * a
        for i in range(N_SPLIT):
            ek[i * FOX_HEADS + hd, base + i] = -1.0
            oq[0, base + i] = 1.0
            eq[i * FOX_HEADS + hd, base + N_SPLIT + i] = 1.0
            ok[0, base + N_SPLIT + i] = 1.0
    return dict(bd=jnp.asarray(bd, BF16), tri=jnp.asarray(tri, BF16),
                eq=jnp.asarray(eq, BF16), ek=jnp.asarray(ek, BF16), oq=jnp.asarray(oq), ok=jnp.asarray(ok))


def _prep_call(x, ca, sb, w, c, tm):
    B, S, _ = x.shape
    tok = lambda width: pl.BlockSpec((1, tm, width), lambda b, s: (b, s, 0))
    args = [x, ca, sb, w["ng"], w["wcat"], w["qan"], w["wqa"], w["wqb"], w["kvn"], w["wk"], w["wv"],
            w["gqm"], w["gkm"], w["bf"], w["gqf"], w["gkf"], c["bd"], c["tri"],
            c["eq"], c["ek"], c["oq"], c["ok"]]
    in_specs = [tok(D_MODEL), tok(LANE), tok(LANE)] + [_full(a.shape) for a in args[3:]]
    widths = [MLA_HEADS * HEAD_PAD, MLA_HEADS * HEAD_PAD, N_PAIRS * PAIR, N_PAIRS * PAIR, N_PAIRS * PAIR, N_PAIRS * PAIR]
    out_shape = [jax.ShapeDtypeStruct((B, S, wd), BF16) for wd in widths]
    out_shape.append(jax.ShapeDtypeStruct((B, S, S5_WIDTH), F32))
    return pl.pallas_call(
        _prep_kernel,
        grid=(B, S // tm),
        in_specs=in_specs,
        out_specs=[tok(wd) for wd in widths] + [tok(S5_WIDTH)],
        out_shape=out_shape,
        scratch_shapes=[pltpu.VMEM((1, LANE), F32)],
        compiler_params=_cparams(("parallel", "arbitrary")),
        name="prep",
    )(*args)


def _attn_kernel(q_ref, k_ref, v_ref, hm_ref, o_ref, qm_ref, sa_ref, sb_ref, bias_ref, m_ref, acc_ref, *,
                 tq, chunk_causal):
    S = q_ref.shape[1]
    nq = S // tq
    n_steps = nq * (nq + 1) // 2
    assert n_steps % 2 == 0 or n_steps == 1
    row = lax.broadcasted_iota(jnp.int32, (tq, tq), 0)
    col = lax.broadcasted_iota(jnp.int32, (tq, tq), 1)
    shift = int(math.log2(CHUNK)) if chunk_causal else 0
    bias_ref[0] = jnp.zeros((tq, tq), F32)
    bias_ref[1] = jnp.where((col >> shift) <= (row >> shift), 0.0, -jnp.inf)
    for a in range(2):
        qm_ref[a] = q_ref[0] * hm_ref[a:a + 1, :]
    acc_ref[...] = jnp.zeros(acc_ref.shape, F32)
    lane = lax.broadcasted_iota(jnp.int32, (tq, LANE), 1)

    def qk(qi, j, s_ref):
        k = k_ref[0, pl.ds(pl.multiple_of(j * tq, tq), tq), :]
        for a in range(2):
            q = qm_ref[a, pl.ds(pl.multiple_of(qi * tq, tq), tq), :]
            s_ref[a] = lax.dot_general(q, k, (((1,), (1,)), ((), ())), preferred_element_type=F32)

    def step(carry, s_cur, s_next):
        qi, j = carry
        last = j == qi
        qi_next = jnp.minimum(qi + last.astype(jnp.int32), nq - 1)
        j_next = jnp.where(last, 0, j + 1)
        qk(qi_next, j_next, s_next)

        bias = bias_ref[last.astype(jnp.int32)]
        k0 = pl.multiple_of(j * tq, tq)
        outs = []
        for a in range(2):
            s = s_cur[a] + bias
            m_prev = jnp.where(j == 0, -jnp.inf, m_ref[a])
            m_new = jnp.maximum(m_prev, jnp.max(s, axis=-1, keepdims=True))
            alpha = jnp.exp2(m_prev - m_new)
            p = jnp.exp2(s - jnp.concatenate([m_new] * (tq // LANE), axis=1))
            v = v_ref[0, pl.ds(k0, tq), a * LANE:(a + 1) * LANE]
            acc = alpha * acc_ref[a] + _dot(p.astype(BF16), v)
            acc_ref[a] = acc
            m_ref[a] = m_new
            outs.append(acc / pltpu.roll(acc, LANE // 2, axis=1))
        q0 = pl.multiple_of(qi * tq, tq)
        o_ref[0, pl.ds(q0, tq), :] = jnp.where(lane < LANE // 2, outs[0], outs[1]).astype(o_ref.dtype)
        return qi_next, j_next

    qk(0, 0, sa_ref)
    start = (jnp.int32(0), jnp.int32(0))
    if n_steps == 1:
        step(start, sa_ref, sb_ref)
    else:
        lax.fori_loop(0, n_steps // 2, lambda _, c: step(step(c, sa_ref, sb_ref), sb_ref, sa_ref), start)


def _attn_call(q, k, v, head_mask, chunk_causal, tq, name):
    B, S, _ = q.shape
    kern = functools.partial(_attn_kernel, tq=tq, chunk_causal=chunk_causal)
    return pl.pallas_call(
        kern,
        grid=(B, N_PAIRS),
        in_specs=[pl.BlockSpec((1, S, PAIR), lambda b, p: (b, 0, p)),
                  pl.BlockSpec((1, S, PAIR), lambda b, p: (b, 0, p)),
                  pl.BlockSpec((1, S, PAIR), lambda b, p: (b, 0, p)),
                  _full((2, PAIR))],
        out_specs=pl.BlockSpec((1, S, LANE), lambda b, p: (b, 0, p)),
        out_shape=jax.ShapeDtypeStruct((B, S, N_PAIRS * LANE), BF16),
        scratch_shapes=[pltpu.VMEM((2, S, PAIR), BF16), pltpu.VMEM((2, tq, tq), F32), pltpu.VMEM((2, tq, tq), F32),
                        pltpu.VMEM((2, tq, tq), F32), pltpu.VMEM((2, tq, LANE), F32), pltpu.VMEM((2, tq, LANE), F32)],
        compiler_params=_cparams(("parallel", "parallel")),
        name=name,
    )(q, k, v, head_mask)


def _head_masks():
    mla = np.zeros((2, PAIR), np.float32)
    fox = np.zeros((2, PAIR), np.float32)
    for a in range(2):
        mla[a, a * HEAD_PAD:(a + 1) * HEAD_PAD] = 1.0
        fox[a, a * FOX_HEAD_DIM:(a + 1) * FOX_HEAD_DIM] = 1.0
        fox[a, LANE + 2 * N_SPLIT * a:LANE + 2 * N_SPLIT * (a + 1)] = 1.0
    return jnp.asarray(mla, BF16), jnp.asarray(fox, BF16)


def _s5_param_kernel(lr_ref, li_ref, ldt_ref, br_ref, bi_ref, are_ref, aim_ref, bbr_ref, bbi_ref):
    lr = lr_ref[...]
    li = li_ref[...]
    dt = jnp.exp(ldt_ref[...])
    mag = jnp.exp(lr * dt)
    a_re = mag * jnp.cos(li * dt)
    a_im = mag * jnp.sin(li * dt)
    den = lr * lr + li * li
    f_re = ((a_re - 1.0) * lr + a_im * li) / den
    f_im = (a_im * lr - (a_re - 1.0) * li) / den
    br = br_ref[...]
    bi = bi_ref[...]
    are_ref[...] = a_re
    aim_ref[...] = a_im
    bbr_ref[...] = f_re * br - f_im * bi
    bbi_ref[...] = f_re * bi + f_im * br


def _s5_params(lw, n_batch):
    G, N, C = S5_GROUPS, S5_STATE, S5_GROUP
    rep = lambda a: jnp.repeat(a, C, axis=0)
    ldt = jnp.broadcast_to(lw["s5_log_dt"][:, None], (G, N))
    tr = lambda b: jnp.transpose(b, (0, 2, 1)).reshape(G * C, N)
    shp = jax.ShapeDtypeStruct((G * C, N), F32)
    a_re, a_im, bb_re, bb_im = pl.pallas_call(
        _s5_param_kernel, out_shape=[shp] * 4, name="s5_params",
    )(rep(lw["s5_lambda_re"]), rep(lw["s5_lambda_im"]), rep(ldt), tr(lw["s5_b_re"]), tr(lw["s5_b_im"]))
    eye = jnp.eye(G, dtype=F32)

    def blockdiag_in(bb):
        return (bb.reshape(G, C, 1, N) * eye[:, None, :, None]).reshape(G * C, G * N)

    def blockdiag_out(cc):
        return (jnp.transpose(cc, (0, 2, 1)).reshape(G, N, 1, C) * eye[:, None, :, None]).reshape(G * N, G * C)

    w_b = jnp.concatenate([blockdiag_in(bb_re), blockdiag_in(bb_im)], axis=1).astype(BF16)
    w_cre = blockdiag_out(lw["s5_c_re"]).astype(BF16)
    w_cim = blockdiag_out(lw["s5_c_im"]).astype(BF16)
    a_row = jnp.concatenate([a_re[::C].reshape(1, G * N), a_im[::C].reshape(1, G * N)], axis=1)
    a_rep = jnp.broadcast_to(a_row, (n_batch, 2 * G * N))
    return dict(w_b=w_b, w_cre=w_cre, w_cim=w_cim, a=a_rep, d=lw["s5_d"][None],
                w_glu=lw["s5_w_glu"].astype(BF16), b_glu=lw["s5_b_glu"][None])


def _s5_kernel(u_ref, a_ref, wb_ref, wcr_ref, wci_ref, d_ref, wg_ref, bg_ref, o_ref, st_ref, xs_ref, *, nb, ts):
    @pl.when(pl.program_id(0) == 0)
    def _():
        st_ref[...] = jnp.zeros_like(st_ref)

    N = S5_STATES
    u = u_ref[...]
    ub = u.astype(BF16)
    for n in range(2 * N // PAIR):
        c0 = (n % (N // PAIR)) // 2 * LANE
        xs_ref[:, n * PAIR:(n + 1) * PAIR] = _dot(ub[:, c0:c0 + LANE], wb_ref[c0:c0 + LANE, n * PAIR:(n + 1) * PAIR])
    ar = a_ref[:, :N]
    ai = a_ref[:, N:]

    def step(t, carry):
        xr, xi = carry
        r = pl.multiple_of(t * nb, nb)
        br = xs_ref[pl.ds(r, nb), :N]
        bi = xs_ref[pl.ds(r, nb), N:]
        nr = ar * xr - ai * xi + br
        ni = ar * xi + ai * xr + bi
        xs_ref[pl.ds(r, nb), :N] = nr
        xs_ref[pl.ds(r, nb), N:] = ni
        return nr, ni

    xr, xi = lax.fori_loop(0, ts, step, (st_ref[:, :N], st_ref[:, N:]))
    st_ref[:, :N] = xr
    st_ref[:, N:] = xi

    ys = []
    for t in range(S5_WIDTH // PAIR):
        r0 = t * (N // 2)
        xr = xs_ref[:, r0:r0 + N // 2].astype(BF16)
        xi = xs_ref[:, N + r0:N + r0 + N // 2].astype(BF16)
        cols = slice(t * PAIR, (t + 1) * PAIR)
        ys.append(_dot(xr, wcr_ref[r0:r0 + N // 2, cols]) - _dot(xi, wci_ref[r0:r0 + N // 2, cols]))
    y = jnp.concatenate(ys, axis=1) + d_ref[...] * u
    z = jax.nn.gelu(y, approximate=True)
    gate = jax.nn.sigmoid(_dot(z.astype(BF16), wg_ref[...]) + bg_ref[...])
    o_ref[...] = (z * gate).astype(o_ref.dtype)


def _s5_call(u_tm, p, nb, ts):
    rows = u_tm.shape[0]
    blk = ts * nb
    args = [u_tm, p["a"], p["w_b"], p["w_cre"], p["w_cim"], p["d"], p["w_glu"], p["b_glu"]]
    kern = functools.partial(_s5_kernel, nb=nb, ts=ts)
    return pl.pallas_call(
        kern,
        grid=(rows // blk,),
        in_specs=[pl.BlockSpec((blk, S5_WIDTH), lambda i: (i, 0))] + [_full(a.shape) for a in args[1:]],
        out_specs=pl.BlockSpec((blk, S5_WIDTH), lambda i: (i, 0)),
        out_shape=jax.ShapeDtypeStruct((rows, S5_WIDTH), BF16),
        scratch_shapes=[pltpu.VMEM((nb, 2 * S5_STATES), F32), pltpu.VMEM((blk, 2 * S5_STATES), F32)],
        compiler_params=_cparams(("arbitrary",)),
        name="s5",
    )(*args)


def _merge_kernel(x_ref, ym_ref, yf_ref, ys_ref, ng_ref, wg_ref, wm_ref, wo_ref, wout_ref, o_ref):
    x = x_ref[0]
    h = _rms(x, ng_ref[...]).astype(BF16)
    merged = None
    for b, y_ref in enumerate((ym_ref, yf_ref, ys_ref)):
        g = _dot(h, wg_ref[:, b * MLA_WIDTH:(b + 1) * MLA_WIDTH])
        gated = (y_ref[0].astype(F32) * (g * jax.nn.sigmoid(g))).astype(BF16)
        o = _dot(gated, wo_ref[b * MLA_WIDTH:(b + 1) * MLA_WIDTH, :])
        m = _dot(h, wm_ref[:, b * D_MODEL:(b + 1) * D_MODEL])
        term = jax.nn.sigmoid(m) * o
        merged = term if merged is None else merged + term
    o_ref[0] = x + _dot(merged.astype(BF16), wout_ref[...])


def _merge_call(x, y_mla, y_fox, y_s5, lw, tm):
    B, S, _ = x.shape
    w_in = lw["w_in"]
    wg = w_in[:, O_GATE:O_GATE + BRANCH_WIDTH].astype(BF16)
    wm = w_in[:, O_MERGE:O_END].astype(BF16)
    wo = lw["w_branch_out"].astype(BF16)
    wout = lw["w_out"].astype(BF16)
    tok = lambda width: pl.BlockSpec((1, tm, width), lambda b, s: (b, s, 0))
    args = [x, y_mla, y_fox, y_s5, lw["norm_g"][None], wg, wm, wo, wout]
    return pl.pallas_call(
        _merge_kernel,
        grid=(B, S // tm),
        in_specs=[tok(D_MODEL), tok(MLA_WIDTH), tok(FOX_WIDTH), tok(S5_WIDTH)] + [_full(a.shape) for a in args[4:]],
        out_specs=tok(D_MODEL),
        out_shape=jax.ShapeDtypeStruct(x.shape, x.dtype),
        compiler_params=_cparams(("parallel", "parallel")),
        name="merge",
    )(*args)


_LAYER_KEYS = ("norm_g", "w_in", "mla_q_a_norm", "mla_w_q_up", "mla_kv_a_norm", "mla_w_kv_up", "mla_q_norm",
               "mla_k_norm", "fox_b_f", "fox_q_norm", "fox_k_norm", "s5_lambda_re", "s5_lambda_im", "s5_log_dt",
               "s5_b_re", "s5_b_im", "s5_c_re", "s5_c_im", "s5_d", "s5_w_glu", "s5_b_glu", "w_branch_out", "w_out")


def _forward(x, positions, params):
    B, S, _ = x.shape
    tm = min(S, 512)
    tq = min(S, 512)
    ts = min(S, 64)
    depth = params["w_in"].shape[0]
    ca, sb = _rope_tables(positions)
    consts = _prep_consts(tm)
    mask_mla, mask_fox = _head_masks()
    h = x
    for l in range(depth):
        lw = {k: params[k][l] for k in _LAYER_KEYS}
        qm, km, vm, qf, kf, vf, u = _prep_call(h, ca, sb, _prep_weights(lw), consts, tm)
        y_mla = _attn_call(qm, km, vm, mask_mla, True, tq, "attn_mla")
        y_fox = _attn_call(qf, kf, vf, mask_fox, False, tq, "attn_fox")
        u_tm = jnp.transpose(u, (1, 0, 2)).reshape(S * B, S5_WIDTH)
        z_tm = _s5_call(u_tm, _s5_params(lw, B), B, ts)
        y_s5 = jnp.transpose(z_tm.reshape(S, B, S5_WIDTH), (1, 0, 2))
        h = _merge_call(h, y_mla, y_fox, y_s5, lw, tm)
    return h


def kernel(x, positions, norm_g, w_in, mla_q_a_norm, mla_w_q_up, mla_kv_a_norm, mla_w_kv_up, mla_q_norm, mla_k_norm, fox_b_f, fox_q_norm, fox_k_norm, s5_lambda_re, s5_lambda_im, s5_log_dt, s5_b_re, s5_b_im, s5_c_re, s5_c_im, s5_d, s5_w_glu, s5_b_glu, w_branch_out, w_out):
    params = dict(zip(_LAYER_KEYS, (norm_g, w_in, mla_q_a_norm, mla_w_q_up, mla_kv_a_norm, mla_w_kv_up, mla_q_norm,
                                    mla_k_norm, fox_b_f, fox_q_norm, fox_k_norm, s5_lambda_re, s5_lambda_im,
                                    s5_log_dt, s5_b_re, s5_b_im, s5_c_re, s5_c_im, s5_d, s5_w_glu, s5_b_glu,
                                    w_branch_out, w_out)))
    return _forward(x, positions, params)
```

```python
import functools
import math

import numpy as np
import jax
import jax.numpy as jnp
from jax import lax
from jax.experimental import pallas as pl
from jax.experimental.pallas import tpu as pltpu

F32 = jnp.float32
BF16 = jnp.bfloat16

D_MODEL = 1024
CHUNK = 64
EPS = 1e-6

MLA_HEADS = 8
MLA_NOPE = 64
MLA_ROPE = 32
MLA_V = 64
MLA_Q_RANK = 256
MLA_KV_RANK = 128
MLA_WIDTH = MLA_HEADS * MLA_V
MLA_QK_DIM = MLA_NOPE + MLA_ROPE
ROPE_THETA = 10000.0

FOX_HEADS = 8
FOX_HEAD_DIM = 64
FOX_WIDTH = FOX_HEADS * FOX_HEAD_DIM

S5_WIDTH = 512
S5_GROUP = 16
S5_GROUPS = S5_WIDTH // S5_GROUP
S5_STATE = 64
S5_STATES = S5_GROUPS * S5_STATE

BRANCH_WIDTH = MLA_WIDTH + FOX_WIDTH + S5_WIDTH

LANE = 128
HEAD_PAD = 128
PAIR = 2 * HEAD_PAD
N_PAIRS = 4
N_SPLIT = 3
LOG2E = math.log2(math.e)

_OFF = np.cumsum([0, MLA_Q_RANK, MLA_KV_RANK, MLA_ROPE, FOX_WIDTH, FOX_WIDTH, FOX_WIDTH, FOX_HEADS,
                  S5_WIDTH, MLA_WIDTH, FOX_WIDTH, S5_WIDTH, D_MODEL, D_MODEL, D_MODEL]).tolist()
(O_CQ, O_CKV, O_KPE, O_FQ, O_FK, O_FV, O_FF, O_S5U, O_GATE, _o1, _o2, O_MERGE, _o3, _o4, O_END) = _OFF

C_CQ = 0
C_CKV = C_CQ + MLA_Q_RANK
C_KPA = C_CKV + MLA_KV_RANK
C_KPB = C_KPA + LANE
C_FQ = C_KPB + LANE
C_FK = C_FQ + FOX_WIDTH
C_FV = C_FK + FOX_WIDTH
C_FF = C_FV + FOX_WIDTH
C_S5 = C_FF + LANE
C_END = C_S5 + S5_WIDTH

VMEM_LIMIT = 56 * 1024 * 1024


def _cparams(sem):
    return pltpu.CompilerParams(dimension_semantics=sem, vmem_limit_bytes=VMEM_LIMIT)


def _full(shape):
    n = len(shape)
    return pl.BlockSpec(shape, lambda *_: (0,) * n)


def _rms(x, g):
    return x * lax.rsqrt(jnp.mean(x * x, axis=-1, keepdims=True) + EPS) * g


def _dot(a, b):
    return jnp.dot(a, b, preferred_element_type=F32)


def _store_v_with_ones(v_ref, v):
    lane = lax.broadcasted_iota(jnp.int32, (v.shape[0], LANE), 1)
    for p in range(N_PAIRS):
        vp = v[:, p * LANE:(p + 1) * LANE]
        v_ref[0, :, p * PAIR:p * PAIR + LANE] = jnp.where(lane < LANE // 2, vp, 1.0).astype(v_ref.dtype)
        v_ref[0, :, p * PAIR + LANE:(p + 1) * PAIR] = jnp.where(lane < LANE // 2, 1.0, vp).astype(v_ref.dtype)


def _rope_kernel(pos_ref, inv_ref, sign_ref, ca_ref, sb_ref):
    ang = pos_ref[0].astype(F32) * inv_ref[...]
    ca_ref[0] = jnp.cos(ang)
    sb_ref[0] = jnp.sin(ang) * sign_ref[...]


def _rope_tables(positions):
    B, S = positions.shape
    ts = min(S, 512)
    inv = 1.0 / (ROPE_THETA ** (jnp.arange(0, MLA_ROPE, 2, dtype=F32) / MLA_ROPE))
    half = MLA_ROPE // 2
    inv_row = jnp.zeros((1, LANE), F32).at[0, MLA_NOPE:MLA_NOPE + half].set(inv)
    inv_row = inv_row.at[0, MLA_NOPE + half:MLA_NOPE + MLA_ROPE].set(inv)
    sign = np.zeros((1, LANE), np.float32)
    sign[0, MLA_NOPE:MLA_NOPE + half] = -1.0
    sign[0, MLA_NOPE + half:MLA_NOPE + MLA_ROPE] = 1.0
    blk = pl.BlockSpec((1, ts, LANE), lambda b, s: (b, s, 0))
    return pl.pallas_call(
        _rope_kernel,
        grid=(B, S // ts),
        in_specs=[pl.BlockSpec((1, ts, 1), lambda b, s: (b, s, 0)), _full((1, LANE)), _full((1, LANE))],
        out_specs=[blk, blk],
        out_shape=[jax.ShapeDtypeStruct((B, S, LANE), F32)] * 2,
        compiler_params=_cparams(("parallel", "parallel")),
        name="rope_tables",
    )(positions.reshape(B, S, 1), inv_row, jnp.asarray(sign))


def _split3(x):
    hi = x.astype(BF16)
    r1 = x - hi.astype(F32)
    mid = r1.astype(BF16)
    return hi, mid, (r1 - mid.astype(F32)).astype(BF16)


def _fox_qk_norm(x, bd, gain):
    sq = (x * x).astype(BF16)
    ss = jnp.concatenate([_dot(sq[:, c:c + PAIR], bd) for c in range(0, FOX_WIDTH, PAIR)], axis=1)
    return (x * lax.rsqrt(ss * (1.0 / FOX_HEAD_DIM) + EPS) * gain).astype(BF16)


def _prep_kernel(x_ref, ca_ref, sb_ref, ng_ref, wcat_ref, qan_ref, wqa_ref, wqb_ref, kvn_ref, wk_ref, wv_ref,
                 gqm_ref, gkm_ref, bf_ref, gqf_ref, gkf_ref, bd_ref, tri_ref,
                 eq_ref, ek_ref, oq_ref, ok_ref,
                 qm_ref, km_ref, vm_ref, qf_ref, kf_ref, vf_ref, u_ref, carry_ref):
    @pl.when(pl.program_id(1) == 0)
    def _():
        carry_ref[...] = jnp.zeros_like(carry_ref)

    x = x_ref[0]
    h = _rms(x, ng_ref[...]).astype(BF16)
    proj = _dot(h, wcat_ref[...])
    ca = ca_ref[0]
    sb = sb_ref[0]

    cqn = _rms(proj[:, C_CQ:C_CQ + MLA_Q_RANK], qan_ref[...]).astype(BF16)
    qa = _dot(cqn, wqa_ref[...])
    qb = _dot(cqn, wqb_ref[...])
    ckvn = _rms(proj[:, C_CKV:C_CKV + MLA_KV_RANK], kvn_ref[...]).astype(BF16)
    kc = _dot(ckvn, wk_ref[...])
    _store_v_with_ones(vm_ref, _dot(ckvn, wv_ref[...]))
    kpe = proj[:, C_KPA:C_KPA + LANE] * ca + proj[:, C_KPB:C_KPB + LANE] * sb
    inv_d = 1.0 / MLA_QK_DIM
    for hd in range(MLA_HEADS):
        sl = slice(hd * HEAD_PAD, (hd + 1) * HEAD_PAD)
        qh = qa[:, sl] * ca + qb[:, sl] * sb
        ss = jnp.sum(qh * qh, axis=-1, keepdims=True)
        qm_ref[0, :, sl] = (qh * lax.rsqrt(ss * inv_d + EPS) * gqm_ref[...]).astype(BF16)
        kh = kc[:, sl] + kpe
        ss = jnp.sum(kh * kh, axis=-1, keepdims=True)
        km_ref[0, :, sl] = (kh * lax.rsqrt(ss * inv_d + EPS) * gkm_ref[...]).astype(BF16)

    fqn = _fox_qk_norm(proj[:, C_FQ:C_FQ + FOX_WIDTH], bd_ref[...], gqf_ref[...])
    fkn = _fox_qk_norm(proj[:, C_FK:C_FK + FOX_WIDTH], bd_ref[...], gkf_ref[...])
    _store_v_with_ones(vf_ref, proj[:, C_FV:C_FV + FOX_WIDTH])

    z = proj[:, C_FF:C_FF + LANE] + bf_ref[...]
    log_f = jnp.minimum(z, 0.0) - jnp.log1p(jnp.exp(-jnp.abs(z)))
    sums = _dot(tri_ref[...], jnp.concatenate(_split3(log_f), axis=1))
    cum = sums[:, :LANE] + sums[:, LANE:2 * LANE] + sums[:, 2 * LANE:] + carry_ref[...]
    carry_ref[...] = cum[cum.shape[0] - 1:, :]
    hi, mid, lo = _split3(cum * LOG2E)
    lane = lax.broadcasted_iota(jnp.int32, hi.shape, 1)
    pieces = jnp.where(lane < FOX_HEADS, hi, jnp.where(lane < 2 * FOX_HEADS, mid, lo))
    auxq = (_dot(pieces, eq_ref[...]) + oq_ref[...]).astype(BF16)
    auxk = (_dot(pieces, ek_ref[...]) + ok_ref[...]).astype(BF16)
    for p in range(N_PAIRS):
        src = slice(p * LANE, (p + 1) * LANE)
        qf_ref[0, :, p * PAIR:p * PAIR + LANE] = fqn[:, src]
        qf_ref[0, :, p * PAIR + LANE:(p + 1) * PAIR] = auxq[:, src]
        kf_ref[0, :, p * PAIR:p * PAIR + LANE] = fkn[:, src]
        kf_ref[0, :, p * PAIR + LANE:(p + 1) * PAIR] = auxk[:, src]

    u_ref[0] = proj[:, C_S5:C_S5 + S5_WIDTH]


def _prep_weights(lw):
    w_in = lw["w_in"]
    zpad = jnp.zeros((D_MODEL, MLA_NOPE), F32)
    kpe = w_in[:, O_KPE:O_KPE + MLA_ROPE]
    half = MLA_ROPE // 2
    kpe_sw = jnp.concatenate([kpe[:, half:], kpe[:, :half]], axis=1)
    tail = jnp.zeros((D_MODEL, LANE - MLA_NOPE - MLA_ROPE), F32)
    ff = jnp.concatenate([w_in[:, O_FF:O_FF + FOX_HEADS]] * N_SPLIT
                         + [jnp.zeros((D_MODEL, LANE - N_SPLIT * FOX_HEADS), F32)], axis=1)
    wcat = jnp.concatenate([
        w_in[:, O_CQ:O_CQ + MLA_Q_RANK], w_in[:, O_CKV:O_CKV + MLA_KV_RANK],
        zpad, kpe, tail, zpad, kpe_sw, tail,
        w_in[:, O_FQ:O_FQ + 3 * FOX_WIDTH], ff, w_in[:, O_S5U:O_S5U + S5_WIDTH]], axis=1).astype(BF16)

    wq = lw["mla_w_q_up"].reshape(MLA_Q_RANK, MLA_HEADS, MLA_QK_DIM)
    nope, pe = wq[:, :, :MLA_NOPE], wq[:, :, MLA_NOPE:]
    pe_sw = jnp.concatenate([pe[:, :, half:], pe[:, :, :half]], axis=2)
    z32 = jnp.zeros((MLA_Q_RANK, MLA_HEADS, HEAD_PAD - MLA_QK_DIM), F32)
    wqa = jnp.concatenate([nope, pe, z32], axis=2).reshape(MLA_Q_RANK, MLA_HEADS * HEAD_PAD).astype(BF16)
    wqb = jnp.concatenate([jnp.zeros_like(nope), pe_sw, z32], axis=2).reshape(MLA_Q_RANK, -1).astype(BF16)

    wkv = lw["mla_w_kv_up"].reshape(MLA_KV_RANK, MLA_HEADS, MLA_NOPE + MLA_V)
    wk = jnp.concatenate([wkv[:, :, :MLA_NOPE], jnp.zeros((MLA_KV_RANK, MLA_HEADS, HEAD_PAD - MLA_NOPE), F32)],
                         axis=2).reshape(MLA_KV_RANK, -1).astype(BF16)
    wv = wkv[:, :, MLA_NOPE:].reshape(MLA_KV_RANK, MLA_WIDTH).astype(BF16)

    pad = jnp.zeros((HEAD_PAD - MLA_QK_DIM,), F32)
    gqm = (jnp.concatenate([lw["mla_q_norm"], pad]) * (LOG2E / math.sqrt(MLA_QK_DIM)))[None]
    gkm = jnp.concatenate([lw["mla_k_norm"], pad])[None]
    bf = jnp.concatenate([lw["fox_b_f"]] * N_SPLIT + [jnp.zeros((LANE - N_SPLIT * FOX_HEADS,), F32)])[None]
    gqf = (jnp.tile(lw["fox_q_norm"], FOX_HEADS) * (LOG2E / math.sqrt(FOX_HEAD_DIM)))[None]
    gkf = jnp.tile(lw["fox_k_norm"], FOX_HEADS)[None]
    return dict(ng=lw["norm_g"][None], wcat=wcat, qan=lw["mla_q_a_norm"][None], wqa=wqa, wqb=wqb,
                kvn=lw["mla_kv_a_norm"][None], wk=wk, wv=wv, gqm=gqm, gkm=gkm, bf=bf, gqf=gqf, gkf=gkf)


def _prep_consts(tm):
    bd = np.kron(np.eye(PAIR // FOX_HEAD_DIM, dtype=np.float32), np.ones((FOX_HEAD_DIM, FOX_HEAD_DIM), np.float32))
    tri = np.tril(np.ones((tm, tm), np.float32))
    eq = np.zeros((LANE, N_PAIRS * LANE), np.float32)
    ek = np.zeros_like(eq)
    oq = np.zeros((1, N_PAIRS * LANE), np.float32)
    ok = np.zeros_like(oq)
    for hd in range(FOX_HEADS):
        p, a = divmod(hd, 2)
        base = p * LANE + 2 * N_SPLIT * a
        for i in range(N_SPLIT):
            ek[i * FOX_HEADS + hd, base + i] = -1.0
            oq[0, base + i] = 1.0
            eq[i * FOX_HEADS + hd, base + N_SPLIT + i] = 1.0
            ok[0, base + N_SPLIT + i] = 1.0
    return dict(bd=jnp.asarray(bd, BF16), tri=jnp.asarray(tri, BF16),
                eq=jnp.asarray(eq, BF16), ek=jnp.asarray(ek, BF16), oq=jnp.asarray(oq), ok=jnp.asarray(ok))


def _prep_call(x, ca, sb, w, c, tm):
    B, S, _ = x.shape
    tok = lambda width: pl.BlockSpec((1, tm, width), lambda b, s: (b, s, 0))
    args = [x, ca, sb, w["ng"], w["wcat"], w["qan"], w["wqa"], w["wqb"], w["kvn"], w["wk"], w["wv"],
            w["gqm"], w["gkm"], w["bf"], w["gqf"], w["gkf"], c["bd"], c["tri"],
            c["eq"], c["ek"], c["oq"], c["ok"]]
    in_specs = [tok(D_MODEL), tok(LANE), tok(LANE)] + [_full(a.shape) for a in args[3:]]
    widths = [MLA_HEADS * HEAD_PAD, MLA_HEADS * HEAD_PAD, N_PAIRS * PAIR, N_PAIRS * PAIR, N_PAIRS * PAIR, N_PAIRS * PAIR]
    out_shape = [jax.ShapeDtypeStruct((B, S, wd), BF16) for wd in widths]
    out_shape.append(jax.ShapeDtypeStruct((B, S, S5_WIDTH), F32))
    return pl.pallas_call(
        _prep_kernel,
        grid=(B, S // tm),
        in_specs=in_specs,
        out_specs=[tok(wd) for wd in widths] + [tok(S5_WIDTH)],
        out_shape=out_shape,
        scratch_shapes=[pltpu.VMEM((1, LANE), F32)],
        compiler_params=_cparams(("parallel", "arbitrary")),
        name="prep",
    )(*args)


def _attn_kernel(q_ref, k_ref, v_ref, hm_ref, o_ref, qm_ref, sa_ref, sb_ref, bias_ref, m_ref, acc_ref, *,
                 tq, chunk_causal):
    S = q_ref.shape[1]
    nq = S // tq
    steps = [(qi, j) for qi in range(nq) for j in range(qi + 1)]
    row = lax.broadcasted_iota(jnp.int32, (tq, tq), 0)
    col = lax.broadcasted_iota(jnp.int32, (tq, tq), 1)
    shift = int(math.log2(CHUNK)) if chunk_causal else 0
    bias_ref[...] = jnp.where((col >> shift) <= (row >> shift), 0.0, -jnp.inf)
    for a in range(2):
        qm_ref[a] = q_ref[0] * hm_ref[a:a + 1, :]
    lane = lax.broadcasted_iota(jnp.int32, (tq, LANE), 1)

    def qk(qi, j, s_ref):
        k = k_ref[0, j * tq:(j + 1) * tq, :]
        for a in range(2):
            q = qm_ref[a, qi * tq:(qi + 1) * tq, :]
            s_ref[a] = lax.dot_general(q, k, (((1,), (1,)), ((), ())), preferred_element_type=F32)

    qk(0, 0, sa_ref)
    bufs = (sa_ref, sb_ref)
    for t, (qi, j) in enumerate(steps):
        s_cur, s_next = bufs[t % 2], bufs[(t + 1) % 2]
        if t + 1 < len(steps):
            qk(*steps[t + 1], s_next)
        outs = []
        for a in range(2):
            s = s_cur[a]
            if j == qi:
                s = s + bias_ref[...]
            v = v_ref[0, j * tq:(j + 1) * tq, a * LANE:(a + 1) * LANE]
            m_cur = jnp.max(s, axis=-1, keepdims=True)
            if j == 0:
                m_new = jnp.broadcast_to(m_cur, (tq, LANE))
                p = jnp.exp2(s - m_cur)
                acc = _dot(p.astype(BF16), v)
            else:
                m_prev = m_ref[a]
                m_new = jnp.maximum(m_prev, m_cur)
                alpha = jnp.exp2(m_prev - m_new)
                p = jnp.exp2(s - jnp.concatenate([m_new] * (tq // LANE), axis=1))
                acc = alpha * acc_ref[a] + _dot(p.astype(BF16), v)
            if j == qi:
                outs.append(acc / pltpu.roll(acc, LANE // 2, axis=1))
            else:
                acc_ref[a] = acc
                m_ref[a] = m_new
        if j == qi:
            o_ref[0, qi * tq:(qi + 1) * tq, :] = jnp.where(lane < LANE // 2, outs[0], outs[1]).astype(o_ref.dtype)


def _attn_call(q, k, v, head_mask, chunk_causal, tq, name):
    B, S, _ = q.shape
    kern = functools.partial(_attn_kernel, tq=tq, chunk_causal=chunk_causal)
    return pl.pallas_call(
        kern,
        grid=(B, N_PAIRS),
        in_specs=[pl.BlockSpec((1, S, PAIR), lambda b, p: (b, 0, p)),
                  pl.BlockSpec((1, S, PAIR), lambda b, p: (b, 0, p)),
                  pl.BlockSpec((1, S, PAIR), lambda b, p: (b, 0, p)),
                  _full((2, PAIR))],
        out_specs=pl.BlockSpec((1, S, LANE), lambda b, p: (b, 0, p)),
        out_shape=jax.ShapeDtypeStruct((B, S, N_PAIRS * LANE), BF16),
        scratch_shapes=[pltpu.VMEM((2, S, PAIR), BF16), pltpu.VMEM((2, tq, tq), F32), pltpu.VMEM((2, tq, tq), F32),
                        pltpu.VMEM((tq, tq), F32), pltpu.VMEM((2, tq, LANE), F32), pltpu.VMEM((2, tq, LANE), F32)],
        compiler_params=_cparams(("parallel", "parallel")),
        name=name,
    )(q, k, v, head_mask)


def _head_masks():
    mla = np.zeros((2, PAIR), np.float32)
    fox = np.zeros((2, PAIR), np.float32)
    for a in range(2):
        mla[a, a * HEAD_PAD:(a + 1) * HEAD_PAD] = 1.0
        fox[a, a * FOX_HEAD_DIM:(a + 1) * FOX_HEAD_DIM] = 1.0
        fox[a, LANE + 2 * N_SPLIT * a:LANE + 2 * N_SPLIT * (a + 1)] = 1.0
    return jnp.asarray(mla, BF16), jnp.asarray(fox, BF16)


def _s5_param_kernel(lr_ref, li_ref, ldt_ref, br_ref, bi_ref, are_ref, aim_ref, bbr_ref, bbi_ref):
    lr = lr_ref[...]
    li = li_ref[...]
    dt = jnp.exp(ldt_ref[...])
    mag = jnp.exp(lr * dt)
    a_re = mag * jnp.cos(li * dt)
    a_im = mag * jnp.sin(li * dt)
    den = lr * lr + li * li
    f_re = ((a_re - 1.0) * lr + a_im * li) / den
    f_im = (a_im * lr - (a_re - 1.0) * li) / den
    br = br_ref[...]
    bi = bi_ref[...]
    are_ref[...] = a_re
    aim_ref[...] = a_im
    bbr_ref[...] = f_re * br - f_im * bi
    bbi_ref[...] = f_re * bi + f_im * br


def _s5_params(lw, n_batch):
    G, N, C = S5_GROUPS, S5_STATE, S5_GROUP
    rep = lambda a: jnp.repeat(a, C, axis=0)
    ldt = jnp.broadcast_to(lw["s5_log_dt"][:, None], (G, N))
    tr = lambda b: jnp.transpose(b, (0, 2, 1)).reshape(G * C, N)
    shp = jax.ShapeDtypeStruct((G * C, N), F32)
    a_re, a_im, bb_re, bb_im = pl.pallas_call(
        _s5_param_kernel, out_shape=[shp] * 4, name="s5_params",
    )(rep(lw["s5_lambda_re"]), rep(lw["s5_lambda_im"]), rep(ldt), tr(lw["s5_b_re"]), tr(lw["s5_b_im"]))
    eye = jnp.eye(G, dtype=F32)

    def blockdiag_in(bb):
        return (bb.reshape(G, C, 1, N) * eye[:, None, :, None]).reshape(G * C, G * N)

    def blockdiag_out(cc):
        return (jnp.transpose(cc, (0, 2, 1)).reshape(G, N, 1, C) * eye[:, None, :, None]).reshape(G * N, G * C)

    w_b = jnp.concatenate([blockdiag_in(bb_re), blockdiag_in(bb_im)], axis=1).astype(BF16)
    w_cre = blockdiag_out(lw["s5_c_re"]).astype(BF16)
    w_cim = blockdiag_out(lw["s5_c_im"]).astype(BF16)
    a_row = jnp.concatenate([a_re[::C].reshape(1, G * N), a_im[::C].reshape(1, G * N)], axis=1)
    a_rep = jnp.broadcast_to(a_row, (n_batch, 2 * G * N))
    return dict(w_b=w_b, w_cre=w_cre, w_cim=w_cim, a=a_rep, d=lw["s5_d"][None],
                w_glu=lw["s5_w_glu"].astype(BF16), b_glu=lw["s5_b_glu"][None])


def _s5_kernel(u_ref, a_ref, wb_ref, wcr_ref, wci_ref, d_ref, wg_ref, bg_ref, o_ref, st_ref, xs_ref, *, nb, ts):
    @pl.when(pl.program_id(0) == 0)
    def _():
        st_ref[...] = jnp.zeros_like(st_ref)

    N = S5_STATES
    u = u_ref[...]
    ub = u.astype(BF16)
    for n in range(2 * N // PAIR):
        c0 = (n % (N // PAIR)) // 2 * LANE
        xs_ref[:, n * PAIR:(n + 1) * PAIR] = _dot(ub[:, c0:c0 + LANE], wb_ref[c0:c0 + LANE, n * PAIR:(n + 1) * PAIR])
    ar = a_ref[:, :N]
    ai = a_ref[:, N:]

    def step(t, carry):
        xr, xi = carry
        r = pl.multiple_of(t * nb, nb)
        br = xs_ref[pl.ds(r, nb), :N]
        bi = xs_ref[pl.ds(r, nb), N:]
        nr = ar * xr - ai * xi + br
        ni = ar * xi + ai * xr + bi
        xs_ref[pl.ds(r, nb), :N] = nr
        xs_ref[pl.ds(r, nb), N:] = ni
        return nr, ni

    xr, xi = lax.fori_loop(0, ts, step, (st_ref[:, :N], st_ref[:, N:]))
    st_ref[:, :N] = xr
    st_ref[:, N:] = xi

    ys = []
    for t in range(S5_WIDTH // PAIR):
        r0 = t * (N // 2)
        xr = xs_ref[:, r0:r0 + N // 2].astype(BF16)
        xi = xs_ref[:, N + r0:N + r0 + N // 2].astype(BF16)
        cols = slice(t * PAIR, (t + 1) * PAIR)
        ys.append(_dot(xr, wcr_ref[r0:r0 + N // 2, cols]) - _dot(xi, wci_ref[r0:r0 + N // 2, cols]))
    y = jnp.concatenate(ys, axis=1) + d_ref[...] * u
    z = jax.nn.gelu(y, approximate=True)
    gate = jax.nn.sigmoid(_dot(z.astype(BF16), wg_ref[...]) + bg_ref[...])
    o_ref[...] = (z * gate).astype(o_ref.dtype)


def _s5_call(u_tm, p, nb, ts):
    rows = u_tm.shape[0]
    blk = ts * nb
    args = [u_tm, p["a"], p["w_b"], p["w_cre"], p["w_cim"], p["d"], p["w_glu"], p["b_glu"]]
    kern = functools.partial(_s5_kernel, nb=nb, ts=ts)
    return pl.pallas_call(
        kern,
        grid=(rows // blk,),
        in_specs=[pl.BlockSpec((blk, S5_WIDTH), lambda i: (i, 0))] + [_full(a.shape) for a in args[1:]],
        out_specs=pl.BlockSpec((blk, S5_WIDTH), lambda i: (i, 0)),
        out_shape=jax.ShapeDtypeStruct((rows, S5_WIDTH), BF16),
        scratch_shapes=[pltpu.VMEM((nb, 2 * S5_STATES), F32), pltpu.VMEM((blk, 2 * S5_STATES), F32)],
        compiler_params=_cparams(("arbitrary",)),
        name="s5",
    )(*args)


def _merge_kernel(x_ref, ym_ref, yf_ref, ys_ref, ng_ref, wg_ref, wm_ref, wo_ref, wout_ref, o_ref):
    x = x_ref[0]
    h = _rms(x, ng_ref[...]).astype(BF16)
    merged = None
    for b, y_ref in enumerate((ym_ref, yf_ref, ys_ref)):
        g = _dot(h, wg_ref[:, b * MLA_WIDTH:(b + 1) * MLA_WIDTH])
        gated = (y_ref[0].astype(F32) * (g * jax.nn.sigmoid(g))).astype(BF16)
        o = _dot(gated, wo_ref[b * MLA_WIDTH:(b + 1) * MLA_WIDTH, :])
        m = _dot(h, wm_ref[:, b * D_MODEL:(b + 1) * D_MODEL])
        term = jax.nn.sigmoid(m) * o
        merged = term if merged is None else merged + term
    o_ref[0] = x + _dot(merged.astype(BF16), wout_ref[...])


def _merge_call(x, y_mla, y_fox, y_s5, lw, tm):
    B, S, _ = x.shape
    w_in = lw["w_in"]
    wg = w_in[:, O_GATE:O_GATE + BRANCH_WIDTH].astype(BF16)
    wm = w_in[:, O_MERGE:O_END].astype(BF16)
    wo = lw["w_branch_out"].astype(BF16)
    wout = lw["w_out"].astype(BF16)
    tok = lambda width: pl.BlockSpec((1, tm, width), lambda b, s: (b, s, 0))
    args = [x, y_mla, y_fox, y_s5, lw["norm_g"][None], wg, wm, wo, wout]
    return pl.pallas_call(
        _merge_kernel,
        grid=(B, S // tm),
        in_specs=[tok(D_MODEL), tok(MLA_WIDTH), tok(FOX_WIDTH), tok(S5_WIDTH)] + [_full(a.shape) for a in args[4:]],
        out_specs=tok(D_MODEL),
        out_shape=jax.ShapeDtypeStruct(x.shape, x.dtype),
        compiler_params=_cparams(("parallel", "parallel")),
        name="merge",
    )(*args)


_LAYER_KEYS = ("norm_g", "w_in", "mla_q_a_norm", "mla_w_q_up", "mla_kv_a_norm", "mla_w_kv_up", "mla_q_norm",
               "mla_k_norm", "fox_b_f", "fox_q_norm", "fox_k_norm", "s5_lambda_re", "s5_lambda_im", "s5_log_dt",
               "s5_b_re", "s5_b_im", "s5_c_re", "s5_c_im", "s5_d", "s5_w_glu", "s5_b_glu", "w_branch_out", "w_out")


def _forward(x, positions, params):
    B, S, _ = x.shape
    tm = min(S, 512)
    tq = min(S, 512)
    ts = min(S, 64)
    depth = params["w_in"].shape[0]
    ca, sb = _rope_tables(positions)
    consts = _prep_consts(tm)
    mask_mla, mask_fox = _head_masks()
    h = x
    for l in range(depth):
        lw = {k: params[k][l] for k in _LAYER_KEYS}
        qm, km, vm, qf, kf, vf, u = _prep_call(h, ca, sb, _prep_weights(lw), consts, tm)
        y_mla = _attn_call(qm, km, vm, mask_mla, True, tq, "attn_mla")
        y_fox = _attn_call(qf, kf, vf, mask_fox, False, tq, "attn_fox")
        u_tm = jnp.transpose(u, (1, 0, 2)).reshape(S * B, S5_WIDTH)
        z_tm = _s5_call(u_tm, _s5_params(lw, B), B, ts)
        y_s5 = jnp.transpose(z_tm.reshape(S, B, S5_WIDTH), (1, 0, 2))
        h = _merge_call(h, y_mla, y_fox, y_s5, lw, tm)
    return h


def kernel(x, positions, norm_g, w_in, mla_q_a_norm, mla_w_q_up, mla_kv_a_norm, mla_w_kv_up, mla_q_norm, mla_k_norm, fox_b_f, fox_q_norm, fox_k_norm, s5_lambda_re, s5_lambda_im, s5_log_dt, s5_b_re, s5_b_im, s5_c_re, s5_c_im, s5_d, s5_w_glu, s5_b_glu, w_branch_out, w_out):
    params = dict(zip(_LAYER_KEYS, (norm_g, w_in, mla_q_a_norm, mla_w_q_up, mla_kv_a_norm, mla_w_kv_up, mla_q_norm,
                                    mla_k_norm, fox_b_f, fox_q_norm, fox_k_norm, s5_lambda_re, s5_lambda_im,
                                    s5_log_dt, s5_b_re, s5_b_im, s5_c_re, s5_c_im, s5_d, s5_w_glu, s5_b_glu,
                                    w_branch_out, w_out)))
    return _forward(x, positions, params)
```

```python
import functools
import math

import numpy as np
import jax
import jax.numpy as jnp
from jax import lax
from jax.experimental import pallas as pl
from jax.experimental.pallas import tpu as pltpu

F32 = jnp.float32
BF16 = jnp.bfloat16

D_MODEL = 1024
CHUNK = 64
EPS = 1e-6

MLA_HEADS = 8
MLA_NOPE = 64
MLA_ROPE = 32
MLA_V = 64
MLA_Q_RANK = 256
MLA_KV_RANK = 128
MLA_WIDTH = MLA_HEADS * MLA_V
MLA_QK_DIM = MLA_NOPE + MLA_ROPE
ROPE_THETA = 10000.0

FOX_HEADS = 8
FOX_HEAD_DIM = 64
FOX_WIDTH = FOX_HEADS * FOX_HEAD_DIM

S5_WIDTH = 512
S5_GROUP = 16
S5_GROUPS = S5_WIDTH // S5_GROUP
S5_STATE = 64
S5_STATES = S5_GROUPS * S5_STATE
S5_ROW_PAD = 8
S5_SCAN_LANES = 1024

BRANCH_WIDTH = MLA_WIDTH + FOX_WIDTH + S5_WIDTH

LANE = 128
HEAD_PAD = 128
PAIR = 2 * HEAD_PAD
N_PAIRS = 4
N_SPLIT = 3
LOG2E = math.log2(math.e)

_OFF = np.cumsum([0, MLA_Q_RANK, MLA_KV_RANK, MLA_ROPE, FOX_WIDTH, FOX_WIDTH, FOX_WIDTH, FOX_HEADS,
                  S5_WIDTH, MLA_WIDTH, FOX_WIDTH, S5_WIDTH, D_MODEL, D_MODEL, D_MODEL]).tolist()
(O_CQ, O_CKV, O_KPE, O_FQ, O_FK, O_FV, O_FF, O_S5U, O_GATE, _o1, _o2, O_MERGE, _o3, _o4, O_END) = _OFF

C_CQ = 0
C_CKV = C_CQ + MLA_Q_RANK
C_KPA = C_CKV + MLA_KV_RANK
C_KPB = C_KPA + LANE
C_FQ = C_KPB + LANE
C_FK = C_FQ + FOX_WIDTH
C_FV = C_FK + FOX_WIDTH
C_FF = C_FV + FOX_WIDTH
C_S5 = C_FF + LANE
C_END = C_S5 + S5_WIDTH

VMEM_LIMIT = 56 * 1024 * 1024


def _cparams(sem):
    return pltpu.CompilerParams(dimension_semantics=sem, vmem_limit_bytes=VMEM_LIMIT)


def _full(shape):
    n = len(shape)
    return pl.BlockSpec(shape, lambda *_: (0,) * n)


def _rms(x, g):
    return x * lax.rsqrt(jnp.mean(x * x, axis=-1, keepdims=True) + EPS) * g


def _dot(a, b):
    return jnp.dot(a, b, preferred_element_type=F32)


def _store_v_with_ones(v_ref, v):
    lane = lax.broadcasted_iota(jnp.int32, (v.shape[0], LANE), 1)
    for p in range(N_PAIRS):
        vp = v[:, p * LANE:(p + 1) * LANE]
        v_ref[0, :, p * PAIR:p * PAIR + LANE] = jnp.where(lane < LANE // 2, vp, 1.0).astype(v_ref.dtype)
        v_ref[0, :, p * PAIR + LANE:(p + 1) * PAIR] = jnp.where(lane < LANE // 2, 1.0, vp).astype(v_ref.dtype)


def _rope_kernel(pos_ref, inv_ref, sign_ref, ca_ref, sb_ref):
    ang = pos_ref[0].astype(F32) * inv_ref[...]
    ca_ref[0] = jnp.cos(ang)
    sb_ref[0] = jnp.sin(ang) * sign_ref[...]


def _rope_tables(positions):
    B, S = positions.shape
    ts = min(S, 512)
    inv = 1.0 / (ROPE_THETA ** (jnp.arange(0, MLA_ROPE, 2, dtype=F32) / MLA_ROPE))
    half = MLA_ROPE // 2
    inv_row = jnp.zeros((1, LANE), F32).at[0, MLA_NOPE:MLA_NOPE + half].set(inv)
    inv_row = inv_row.at[0, MLA_NOPE + half:MLA_NOPE + MLA_ROPE].set(inv)
    sign = np.zeros((1, LANE), np.float32)
    sign[0, MLA_NOPE:MLA_NOPE + half] = -1.0
    sign[0, MLA_NOPE + half:MLA_NOPE + MLA_ROPE] = 1.0
    blk = pl.BlockSpec((1, ts, LANE), lambda b, s: (b, s, 0))
    return pl.pallas_call(
        _rope_kernel,
        grid=(B, S // ts),
        in_specs=[pl.BlockSpec((1, ts, 1), lambda b, s: (b, s, 0)), _full((1, LANE)), _full((1, LANE))],
        out_specs=[blk, blk],
        out_shape=[jax.ShapeDtypeStruct((B, S, LANE), F32)] * 2,
        compiler_params=_cparams(("parallel", "parallel")),
        name="rope_tables",
    )(positions.reshape(B, S, 1), inv_row, jnp.asarray(sign))


def _split3(x):
    hi = x.astype(BF16)
    r1 = x - hi.astype(F32)
    mid = r1.astype(BF16)
    return hi, mid, (r1 - mid.astype(F32)).astype(BF16)


def _fox_qk_norm(x, bd, gain):
    sq = (x * x).astype(BF16)
    ss = jnp.concatenate([_dot(sq[:, c:c + PAIR], bd) for c in range(0, FOX_WIDTH, PAIR)], axis=1)
    return (x * lax.rsqrt(ss * (1.0 / FOX_HEAD_DIM) + EPS) * gain).astype(BF16)


def _prep_kernel(x_ref, ca_ref, sb_ref, ng_ref, wcat_ref, qan_ref, wqa_ref, wqb_ref, kvn_ref, wk_ref, wv_ref,
                 gqm_ref, gkm_ref, bf_ref, gqf_ref, gkf_ref, bd_ref, tri_ref,
                 eq_ref, ek_ref, oq_ref, ok_ref,
                 qm_ref, km_ref, vm_ref, qf_ref, kf_ref, vf_ref, u_ref, carry_ref):
    @pl.when(pl.program_id(1) == 0)
    def _():
        carry_ref[...] = jnp.zeros_like(carry_ref)

    x = x_ref[0]
    h = _rms(x, ng_ref[...]).astype(BF16)
    proj = _dot(h, wcat_ref[...])
    ca = ca_ref[0]
    sb = sb_ref[0]

    cqn = _rms(proj[:, C_CQ:C_CQ + MLA_Q_RANK], qan_ref[...]).astype(BF16)
    qa = _dot(cqn, wqa_ref[...])
    qb = _dot(cqn, wqb_ref[...])
    ckvn = _rms(proj[:, C_CKV:C_CKV + MLA_KV_RANK], kvn_ref[...]).astype(BF16)
    kc = _dot(ckvn, wk_ref[...])
    _store_v_with_ones(vm_ref, _dot(ckvn, wv_ref[...]))
    kpe = proj[:, C_KPA:C_KPA + LANE] * ca + proj[:, C_KPB:C_KPB + LANE] * sb
    inv_d = 1.0 / MLA_QK_DIM
    for hd in range(MLA_HEADS):
        sl = slice(hd * HEAD_PAD, (hd + 1) * HEAD_PAD)
        qh = qa[:, sl] * ca + qb[:, sl] * sb
        ss = jnp.sum(qh * qh, axis=-1, keepdims=True)
        qm_ref[0, :, sl] = (qh * lax.rsqrt(ss * inv_d + EPS) * gqm_ref[...]).astype(BF16)
        kh = kc[:, sl] + kpe
        ss = jnp.sum(kh * kh, axis=-1, keepdims=True)
        km_ref[0, :, sl] = (kh * lax.rsqrt(ss * inv_d + EPS) * gkm_ref[...]).astype(BF16)

    fqn = _fox_qk_norm(proj[:, C_FQ:C_FQ + FOX_WIDTH], bd_ref[...], gqf_ref[...])
    fkn = _fox_qk_norm(proj[:, C_FK:C_FK + FOX_WIDTH], bd_ref[...], gkf_ref[...])
    _store_v_with_ones(vf_ref, proj[:, C_FV:C_FV + FOX_WIDTH])

    z = proj[:, C_FF:C_FF + LANE] + bf_ref[...]
    log_f = jnp.minimum(z, 0.0) - jnp.log1p(jnp.exp(-jnp.abs(z)))
    sums = _dot(tri_ref[...], jnp.concatenate(_split3(log_f), axis=1))
    cum = sums[:, :LANE] + sums[:, LANE:2 * LANE] + sums[:, 2 * LANE:] + carry_ref[...]
    carry_ref[...] = cum[cum.shape[0] - 1:, :]
    hi, mid, lo = _split3(cum * LOG2E)
    lane = lax.broadcasted_iota(jnp.int32, hi.shape, 1)
    pieces = jnp.where(lane < FOX_HEADS, hi, jnp.where(lane < 2 * FOX_HEADS, mid, lo))
    auxq = (_dot(pieces, eq_ref[...]) + oq_ref[...]).astype(BF16)
    auxk = (_dot(pieces, ek_ref[...]) + ok_ref[...]).astype(BF16)
    for p in range(N_PAIRS):
        src = slice(p * LANE, (p + 1) * LANE)
        qf_ref[0, :, p * PAIR:p * PAIR + LANE] = fqn[:, src]
        qf_ref[0, :, p * PAIR + LANE:(p + 1) * PAIR] = auxq[:, src]
        kf_ref[0, :, p * PAIR:p * PAIR + LANE] = fkn[:, src]
        kf_ref[0, :, p * PAIR + LANE:(p + 1) * PAIR] = auxk[:, src]

    u_ref[0] = proj[:, C_S5:C_S5 + S5_WIDTH]


def _prep_weights(lw):
    w_in = lw["w_in"]
    zpad = jnp.zeros((D_MODEL, MLA_NOPE), F32)
    kpe = w_in[:, O_KPE:O_KPE + MLA_ROPE]
    half = MLA_ROPE // 2
    kpe_sw = jnp.concatenate([kpe[:, half:], kpe[:, :half]], axis=1)
    tail = jnp.zeros((D_MODEL, LANE - MLA_NOPE - MLA_ROPE), F32)
    ff = jnp.concatenate([w_in[:, O_FF:O_FF + FOX_HEADS]] * N_SPLIT
                         + [jnp.zeros((D_MODEL, LANE - N_SPLIT * FOX_HEADS), F32)], axis=1)
    wcat = jnp.concatenate([
        w_in[:, O_CQ:O_CQ + MLA_Q_RANK], w_in[:, O_CKV:O_CKV + MLA_KV_RANK],
        zpad, kpe, tail, zpad, kpe_sw, tail,
        w_in[:, O_FQ:O_FQ + 3 * FOX_WIDTH], ff, w_in[:, O_S5U:O_S5U + S5_WIDTH]], axis=1).astype(BF16)

    wq = lw["mla_w_q_up"].reshape(MLA_Q_RANK, MLA_HEADS, MLA_QK_DIM)
    nope, pe = wq[:, :, :MLA_NOPE], wq[:, :, MLA_NOPE:]
    pe_sw = jnp.concatenate([pe[:, :, half:], pe[:, :, :half]], axis=2)
    z32 = jnp.zeros((MLA_Q_RANK, MLA_HEADS, HEAD_PAD - MLA_QK_DIM), F32)
    wqa = jnp.concatenate([nope, pe, z32], axis=2).reshape(MLA_Q_RANK, MLA_HEADS * HEAD_PAD).astype(BF16)
    wqb = jnp.concatenate([jnp.zeros_like(nope), pe_sw, z32], axis=2).reshape(MLA_Q_RANK, -1).astype(BF16)

    wkv = lw["mla_w_kv_up"].reshape(MLA_KV_RANK, MLA_HEADS, MLA_NOPE + MLA_V)
    wk = jnp.concatenate([wkv[:, :, :MLA_NOPE], jnp.zeros((MLA_KV_RANK, MLA_HEADS, HEAD_PAD - MLA_NOPE), F32)],
                         axis=2).reshape(MLA_KV_RANK, -1).astype(BF16)
    wv = wkv[:, :, MLA_NOPE:].reshape(MLA_KV_RANK, MLA_WIDTH).astype(BF16)

    pad = jnp.zeros((HEAD_PAD - MLA_QK_DIM,), F32)
    gqm = (jnp.concatenate([lw["mla_q_norm"], pad]) * (LOG2E / math.sqrt(MLA_QK_DIM)))[None]
    gkm = jnp.concatenate([lw["mla_k_norm"], pad])[None]
    bf = jnp.concatenate([lw["fox_b_f"]] * N_SPLIT + [jnp.zeros((LANE - N_SPLIT * FOX_HEADS,), F32)])[None]
    gqf = (jnp.tile(lw["fox_q_norm"], FOX_HEADS) * (LOG2E / math.sqrt(FOX_HEAD_DIM)))[None]
    gkf = jnp.tile(lw["fox_k_norm"], FOX_HEADS)[None]
    return dict(ng=lw["norm_g"][None], wcat=wcat, qan=lw["mla_q_a_norm"][None], wqa=wqa, wqb=wqb,
                kvn=lw["mla_kv_a_norm"][None], wk=wk, wv=wv, gqm=gqm, gkm=gkm, bf=bf, gqf=gqf, gkf=gkf)


def _prep_consts(tm):
    bd = np.kron(np.eye(PAIR // FOX_HEAD_DIM, dtype=np.float32), np.ones((FOX_HEAD_DIM, FOX_HEAD_DIM), np.float32))
    tri = np.tril(np.ones((tm, tm), np.float32))
    eq = np.zeros((LANE, N_PAIRS * LANE), np.float32)
    ek = np.zeros_like(eq)
    oq = np.zeros((1, N_PAIRS * LANE), np.float32)
    ok = np.zeros_like(oq)
    for hd in range(FOX_HEADS):
        p, a = divmod(hd, 2)
        base = p * LANE + 2 * N_SPLIT * a
        for i in range(N_SPLIT):
            ek[i * FOX_HEADS + hd, base + i] = -1.0
            oq[0, base + i] = 1.0
            eq[i * FOX_HEADS + hd, base + N_SPLIT + i] = 1.0
            ok[0, base + N_SPLIT + i] = 1.0
    return dict(bd=jnp.asarray(bd, BF16), tri=jnp.asarray(tri, BF16),
                eq=jnp.asarray(eq, BF16), ek=jnp.asarray(ek, BF16), oq=jnp.asarray(oq), ok=jnp.asarray(ok))


def _prep_call(x, ca, sb, w, c, tm):
    B, S, _ = x.shape
    tok = lambda width: pl.BlockSpec((1, tm, width), lambda b, s: (b, s, 0))
    args = [x, ca, sb, w["ng"], w["wcat"], w["qan"], w["wqa"], w["wqb"], w["kvn"], w["wk"], w["wv"],
            w["gqm"], w["gkm"], w["bf"], w["gqf"], w["gkf"], c["bd"], c["tri"],
            c["eq"], c["ek"], c["oq"], c["ok"]]
    in_specs = [tok(D_MODEL), tok(LANE), tok(LANE)] + [_full(a.shape) for a in args[3:]]
    widths = [MLA_HEADS * HEAD_PAD, MLA_HEADS * HEAD_PAD, N_PAIRS * PAIR, N_PAIRS * PAIR, N_PAIRS * PAIR, N_PAIRS * PAIR]
    out_shape = [jax.ShapeDtypeStruct((B, S, wd), BF16) for wd in widths]
    out_shape.append(jax.ShapeDtypeStruct((B, S, S5_WIDTH), F32))
    return pl.pallas_call(
        _prep_kernel,
        grid=(B, S // tm),
        in_specs=in_specs,
        out_specs=[tok(wd) for wd in widths] + [tok(S5_WIDTH)],
        out_shape=out_shape,
        scratch_shapes=[pltpu.VMEM((1, LANE), F32)],
        compiler_params=_cparams(("parallel", "arbitrary")),
        name="prep",
    )(*args)


def _attn_kernel(q_ref, k_ref, v_ref, hm_ref, o_ref, qm_ref, sa_ref, sb_ref, bias_ref, m_ref, acc_ref, *,
                 tq, chunk_causal):
    S = q_ref.shape[1]
    nq = S // tq
    steps = [(qi, j) for qi in range(nq) for j in range(qi + 1)]
    row = lax.broadcasted_iota(jnp.int32, (tq, tq), 0)
    col = lax.broadcasted_iota(jnp.int32, (tq, tq), 1)
    shift = int(math.log2(CHUNK)) if chunk_causal else 0
    bias_ref[...] = jnp.where((col >> shift) <= (row >> shift), 0.0, -jnp.inf)
    for a in range(2):
        qm_ref[a] = q_ref[0] * hm_ref[a:a + 1, :]
    lane = lax.broadcasted_iota(jnp.int32, (tq, LANE), 1)

    def qk(qi, j, s_ref):
        k = k_ref[0, j * tq:(j + 1) * tq, :]
        for a in range(2):
            q = qm_ref[a, qi * tq:(qi + 1) * tq, :]
            s_ref[a] = lax.dot_general(q, k, (((1,), (1,)), ((), ())), preferred_element_type=F32)

    qk(0, 0, sa_ref)
    bufs = (sa_ref, sb_ref)
    for t, (qi, j) in enumerate(steps):
        s_cur, s_next = bufs[t % 2], bufs[(t + 1) % 2]
        if t + 1 < len(steps):
            qk(*steps[t + 1], s_next)
        outs = []
        for a in range(2):
            s = s_cur[a]
            if j == qi:
                s = s + bias_ref[...]
            v = v_ref[0, j * tq:(j + 1) * tq, a * LANE:(a + 1) * LANE]
            m_cur = jnp.max(s, axis=-1, keepdims=True)
            if j == 0:
                m_new = jnp.broadcast_to(m_cur, (tq, LANE))
                p = jnp.exp2(s - m_cur)
                acc = _dot(p.astype(BF16), v)
            else:
                m_prev = m_ref[a]
                m_new = jnp.maximum(m_prev, m_cur)
                alpha = jnp.exp2(m_prev - m_new)
                p = jnp.exp2(s - jnp.concatenate([m_new] * (tq // LANE), axis=1))
                acc = alpha * acc_ref[a] + _dot(p.astype(BF16), v)
            if j == qi:
                outs.append(acc / pltpu.roll(acc, LANE // 2, axis=1))
            else:
                acc_ref[a] = acc
                m_ref[a] = m_new
        if j == qi:
            o_ref[0, qi * tq:(qi + 1) * tq, :] = jnp.where(lane < LANE // 2, outs[0], outs[1]).astype(o_ref.dtype)


def _attn_call(q, k, v, head_mask, chunk_causal, tq, name):
    B, S, _ = q.shape
    kern = functools.partial(_attn_kernel, tq=tq, chunk_causal=chunk_causal)
    return pl.pallas_call(
        kern,
        grid=(B, N_PAIRS),
        in_specs=[pl.BlockSpec((1, S, PAIR), lambda b, p: (b, 0, p)),
                  pl.BlockSpec((1, S, PAIR), lambda b, p: (b, 0, p)),
                  pl.BlockSpec((1, S, PAIR), lambda b, p: (b, 0, p)),
                  _full((2, PAIR))],
        out_specs=pl.BlockSpec((1, S, LANE), lambda b, p: (b, 0, p)),
        out_shape=jax.ShapeDtypeStruct((B, S, N_PAIRS * LANE), BF16),
        scratch_shapes=[pltpu.VMEM((2, S, PAIR), BF16), pltpu.VMEM((2, tq, tq), F32), pltpu.VMEM((2, tq, tq), F32),
                        pltpu.VMEM((tq, tq), F32), pltpu.VMEM((2, tq, LANE), F32), pltpu.VMEM((2, tq, LANE), F32)],
        compiler_params=_cparams(("parallel", "parallel")),
        name=name,
    )(q, k, v, head_mask)


def _head_masks():
    mla = np.zeros((2, PAIR), np.float32)
    fox = np.zeros((2, PAIR), np.float32)
    for a in range(2):
        mla[a, a * HEAD_PAD:(a + 1) * HEAD_PAD] = 1.0
        fox[a, a * FOX_HEAD_DIM:(a + 1) * FOX_HEAD_DIM] = 1.0
        fox[a, LANE + 2 * N_SPLIT * a:LANE + 2 * N_SPLIT * (a + 1)] = 1.0
    return jnp.asarray(mla, BF16), jnp.asarray(fox, BF16)


def _s5_param_kernel(lr_ref, li_ref, ldt_ref, br_ref, bi_ref, are_ref, aim_ref, bbr_ref, bbi_ref):
    lr = lr_ref[...]
    li = li_ref[...]
    dt = jnp.exp(ldt_ref[...])
    mag = jnp.exp(lr * dt)
    a_re = mag * jnp.cos(li * dt)
    a_im = mag * jnp.sin(li * dt)
    den = lr * lr + li * li
    f_re = ((a_re - 1.0) * lr + a_im * li) / den
    f_im = (a_im * lr - (a_re - 1.0) * li) / den
    br = br_ref[...]
    bi = bi_ref[...]
    are_ref[...] = a_re
    aim_ref[...] = a_im
    bbr_ref[...] = f_re * br - f_im * bi
    bbi_ref[...] = f_re * bi + f_im * br


def _s5_params(lw, n_batch):
    G, N, C = S5_GROUPS, S5_STATE, S5_GROUP
    rep = lambda a: jnp.repeat(a, C, axis=0)
    ldt = jnp.broadcast_to(lw["s5_log_dt"][:, None], (G, N))
    tr = lambda b: jnp.transpose(b, (0, 2, 1)).reshape(G * C, N)
    shp = jax.ShapeDtypeStruct((G * C, N), F32)
    a_re, a_im, bb_re, bb_im = pl.pallas_call(
        _s5_param_kernel, out_shape=[shp] * 4, name="s5_params",
    )(rep(lw["s5_lambda_re"]), rep(lw["s5_lambda_im"]), rep(ldt), tr(lw["s5_b_re"]), tr(lw["s5_b_im"]))
    eye = jnp.eye(G, dtype=F32)

    def blockdiag_in(bb):
        return (bb.reshape(G, C, 1, N) * eye[:, None, :, None]).reshape(G * C, G * N)

    def blockdiag_out(cc):
        return (jnp.transpose(cc, (0, 2, 1)).reshape(G, N, 1, C) * eye[:, None, :, None]).reshape(G * N, G * C)

    w_b = jnp.concatenate([blockdiag_in(bb_re), blockdiag_in(bb_im)], axis=1).astype(BF16)
    w_cre = blockdiag_out(lw["s5_c_re"]).astype(BF16)
    w_cim = blockdiag_out(lw["s5_c_im"]).astype(BF16)
    a_row = jnp.concatenate([a_re[::C].reshape(1, G * N), a_im[::C].reshape(1, G * N)], axis=1)
    a_rep = jnp.broadcast_to(a_row, (n_batch, 2 * G * N))
    return dict(w_b=w_b, w_cre=w_cre, w_cim=w_cim, a=a_rep, d=lw["s5_d"][None],
                w_glu=lw["s5_w_glu"].astype(BF16), b_glu=lw["s5_b_glu"][None])


def _s5_kernel(u_ref, a_ref, wb_ref, wcr_ref, wci_ref, d_ref, wg_ref, bg_ref, o_ref, st_ref, us_ref, zs_ref, bu_ref,
               xs_ref, *, nb, ts, n_sub):
    @pl.when(pl.program_id(0) == 0)
    def _():
        st_ref[...] = jnp.zeros_like(st_ref)

    N = S5_STATES
    pitch = ts + S5_ROW_PAD
    nk = S5_WIDTH // LANE

    for c in range(n_sub):
        for b in range(nb):
            for k in range(nk):
                us_ref[k, b * pitch:b * pitch + ts, :] = u_ref[b, c * ts:(c + 1) * ts, k * LANE:(k + 1) * LANE]
        u = jnp.concatenate(
            [jnp.concatenate([us_ref[k, pl.ds(t, nb, stride=pitch), :] for k in range(nk)], axis=1) for t in range(ts)],
            axis=0)
        ub = u.astype(BF16)
        for n in range(2 * N // PAIR):
            c0 = (n % (N // PAIR)) // 2 * LANE
            bu_ref[:, n * PAIR:(n + 1) * PAIR] = _dot(ub[:, c0:c0 + LANE], wb_ref[c0:c0 + LANE, n * PAIR:(n + 1) * PAIR])

        for g in range(N // S5_SCAN_LANES):
            re = slice(g * S5_SCAN_LANES, (g + 1) * S5_SCAN_LANES)
            im = slice(N + g * S5_SCAN_LANES, N + (g + 1) * S5_SCAN_LANES)
            ar = a_ref[:, re]
            ai = a_ref[:, im]

            def step(t, carry):
                xr, xi = carry
                r = pl.multiple_of(t * nb, nb)
                nr = ar * xr - ai * xi + bu_ref[pl.ds(r, nb), re]
                ni = ar * xi + ai * xr + bu_ref[pl.ds(r, nb), im]
                xs_ref[pl.ds(r, nb), re] = nr
                xs_ref[pl.ds(r, nb), im] = ni
                return nr, ni

            xr, xi = lax.fori_loop(0, ts, step, (st_ref[:, re], st_ref[:, im]), unroll=4)
            st_ref[:, re] = xr
            st_ref[:, im] = xi

        ys = []
        for t in range(S5_WIDTH // PAIR):
            r0 = t * (N // 2)
            xr = xs_ref[:, r0:r0 + N // 2].astype(BF16)
            xi = xs_ref[:, N + r0:N + r0 + N // 2].astype(BF16)
            cols = slice(t * PAIR, (t + 1) * PAIR)
            ys.append(_dot(xr, wcr_ref[r0:r0 + N // 2, cols]) - _dot(xi, wci_ref[r0:r0 + N // 2, cols]))
        y = jnp.concatenate(ys, axis=1) + d_ref[...] * u
        z = jax.nn.gelu(y, approximate=True)
        z = z * jax.nn.sigmoid(_dot(z.astype(BF16), wg_ref[...]) + bg_ref[...])
        for t in range(ts):
            for k in range(nk):
                zs_ref[k, pl.ds(t, nb, stride=pitch), :] = z[t * nb:(t + 1) * nb, k * LANE:(k + 1) * LANE]
        for b in range(nb):
            rows = jnp.concatenate([zs_ref[k, b * pitch:b * pitch + ts, :] for k in range(nk)], axis=1)
            o_ref[b, c * ts:(c + 1) * ts, :] = rows.astype(o_ref.dtype)


def _s5_call(u, p, ts, n_sub):
    B, S, _ = u.shape
    blk = ts * n_sub
    args = [u, p["a"], p["w_b"], p["w_cre"], p["w_cim"], p["d"], p["w_glu"], p["b_glu"]]
    kern = functools.partial(_s5_kernel, nb=B, ts=ts, n_sub=n_sub)
    once = lambda a: pl.BlockSpec(a.shape, lambda i: (0,) * a.ndim, pipeline_mode=pl.Buffered(1))
    slab_buf = pltpu.VMEM((S5_WIDTH // LANE, B * (ts + S5_ROW_PAD), LANE), F32)
    state_buf = pltpu.VMEM((ts * B, 2 * S5_STATES), F32)
    return pl.pallas_call(
        kern,
        grid=(S // blk,),
        in_specs=[pl.BlockSpec((B, blk, S5_WIDTH), lambda i: (0, i, 0))] + [once(a) for a in args[1:]],
        out_specs=pl.BlockSpec((B, blk, S5_WIDTH), lambda i: (0, i, 0)),
        out_shape=jax.ShapeDtypeStruct((B, S, S5_WIDTH), BF16),
        scratch_shapes=[pltpu.VMEM((B, 2 * S5_STATES), F32), slab_buf, slab_buf, state_buf, state_buf],
        compiler_params=_cparams(("arbitrary",)),
        name="s5",
    )(*args)


def _merge_kernel(x_ref, ym_ref, yf_ref, ys_ref, ng_ref, wg_ref, wm_ref, wo_ref, wout_ref, o_ref):
    x = x_ref[0]
    h = _rms(x, ng_ref[...]).astype(BF16)
    merged = None
    for b, y_ref in enumerate((ym_ref, yf_ref, ys_ref)):
        g = _dot(h, wg_ref[:, b * MLA_WIDTH:(b + 1) * MLA_WIDTH])
        gated = (y_ref[0].astype(F32) * (g * jax.nn.sigmoid(g))).astype(BF16)
        o = _dot(gated, wo_ref[b * MLA_WIDTH:(b + 1) * MLA_WIDTH, :])
        m = _dot(h, wm_ref[:, b * D_MODEL:(b + 1) * D_MODEL])
        term = jax.nn.sigmoid(m) * o
        merged = term if merged is None else merged + term
    o_ref[0] = x + _dot(merged.astype(BF16), wout_ref[...])


def _merge_call(x, y_mla, y_fox, y_s5, lw, tm):
    B, S, _ = x.shape
    w_in = lw["w_in"]
    wg = w_in[:, O_GATE:O_GATE + BRANCH_WIDTH].astype(BF16)
    wm = w_in[:, O_MERGE:O_END].astype(BF16)
    wo = lw["w_branch_out"].astype(BF16)
    wout = lw["w_out"].astype(BF16)
    tok = lambda width: pl.BlockSpec((1, tm, width), lambda b, s: (b, s, 0))
    args = [x, y_mla, y_fox, y_s5, lw["norm_g"][None], wg, wm, wo, wout]
    return pl.pallas_call(
        _merge_kernel,
        grid=(B, S // tm),
        in_specs=[tok(D_MODEL), tok(MLA_WIDTH), tok(FOX_WIDTH), tok(S5_WIDTH)] + [_full(a.shape) for a in args[4:]],
        out_specs=tok(D_MODEL),
        out_shape=jax.ShapeDtypeStruct(x.shape, x.dtype),
        compiler_params=_cparams(("parallel", "parallel")),
        name="merge",
    )(*args)


_LAYER_KEYS = ("norm_g", "w_in", "mla_q_a_norm", "mla_w_q_up", "mla_kv_a_norm", "mla_w_kv_up", "mla_q_norm",
               "mla_k_norm", "fox_b_f", "fox_q_norm", "fox_k_norm", "s5_lambda_re", "s5_lambda_im", "s5_log_dt",
               "s5_b_re", "s5_b_im", "s5_c_re", "s5_c_im", "s5_d", "s5_w_glu", "s5_b_glu", "w_branch_out", "w_out")


def _forward(x, positions, params):
    B, S, _ = x.shape
    tm = min(S, 512)
    tq = min(S, 512)
    ts = min(S, 64)
    depth = params["w_in"].shape[0]
    ca, sb = _rope_tables(positions)
    consts = _prep_consts(tm)
    mask_mla, mask_fox = _head_masks()
    h = x
    for l in range(depth):
        lw = {k: params[k][l] for k in _LAYER_KEYS}
        qm, km, vm, qf, kf, vf, u = _prep_call(h, ca, sb, _prep_weights(lw), consts, tm)
        y_mla = _attn_call(qm, km, vm, mask_mla, True, tq, "attn_mla")
        y_fox = _attn_call(qf, kf, vf, mask_fox, False, tq, "attn_fox")
        y_s5 = _s5_call(u, _s5_params(lw, B), ts, min(S // ts, 4))
        h = _merge_call(h, y_mla, y_fox, y_s5, lw, tm)
    return h


def kernel(x, positions, norm_g, w_in, mla_q_a_norm, mla_w_q_up, mla_kv_a_norm, mla_w_kv_up, mla_q_norm, mla_k_norm, fox_b_f, fox_q_norm, fox_k_norm, s5_lambda_re, s5_lambda_im, s5_log_dt, s5_b_re, s5_b_im, s5_c_re, s5_c_im, s5_d, s5_w_glu, s5_b_glu, w_branch_out, w_out):
    params = dict(zip(_LAYER_KEYS, (norm_g, w_in, mla_q_a_norm, mla_w_q_up, mla_kv_a_norm, mla_w_kv_up, mla_q_norm,
                                    mla_k_norm, fox_b_f, fox_q_norm, fox_k_norm, s5_lambda_re, s5_lambda_im,
                                    s5_log_dt, s5_b_re, s5_b_im, s5_c_re, s5_c_im, s5_d, s5_w_glu, s5_b_glu,
                                    w_branch_out, w_out)))
    return _forward(x, positions, params)
```

```python
import functools
import math

import numpy as np
import jax
import jax.numpy as jnp
from jax import lax
from jax.experimental import pallas as pl
from jax.experimental.pallas import tpu as pltpu

F32 = jnp.float32
BF16 = jnp.bfloat16

D_MODEL = 1024
CHUNK = 64
EPS = 1e-6

MLA_HEADS = 8
MLA_NOPE = 64
MLA_ROPE = 32
MLA_V = 64
MLA_Q_RANK = 256
MLA_KV_RANK = 128
MLA_WIDTH = MLA_HEADS * MLA_V
MLA_QK_DIM = MLA_NOPE + MLA_ROPE
ROPE_THETA = 10000.0

FOX_HEADS = 8
FOX_HEAD_DIM = 64
FOX_WIDTH = FOX_HEADS * FOX_HEAD_DIM

S5_WIDTH = 512
S5_GROUP = 16
S5_GROUPS = S5_WIDTH // S5_GROUP
S5_STATE = 64
S5_STATES = S5_GROUPS * S5_STATE
S5_ROW_PAD = 8
S5_SCAN_LANES = 1024

BRANCH_WIDTH = MLA_WIDTH + FOX_WIDTH + S5_WIDTH

LANE = 128
HEAD_PAD = 128
PAIR = 2 * HEAD_PAD
N_PAIRS = 4
N_SPLIT = 3
LOG2E = math.log2(math.e)

_OFF = np.cumsum([0, MLA_Q_RANK, MLA_KV_RANK, MLA_ROPE, FOX_WIDTH, FOX_WIDTH, FOX_WIDTH, FOX_HEADS,
                  S5_WIDTH, MLA_WIDTH, FOX_WIDTH, S5_WIDTH, D_MODEL, D_MODEL, D_MODEL]).tolist()
(O_CQ, O_CKV, O_KPE, O_FQ, O_FK, O_FV, O_FF, O_S5U, O_GATE, _o1, _o2, O_MERGE, _o3, _o4, O_END) = _OFF

C_CQ = 0
C_CKV = C_CQ + MLA_Q_RANK
C_KPA = C_CKV + MLA_KV_RANK
C_KPB = C_KPA + LANE
C_FQ = C_KPB + LANE
C_FK = C_FQ + FOX_WIDTH
C_FV = C_FK + FOX_WIDTH
C_FF = C_FV + FOX_WIDTH
C_S5 = C_FF + LANE
C_END = C_S5 + S5_WIDTH

VMEM_LIMIT = 56 * 1024 * 1024


def _cparams(sem):
    return pltpu.CompilerParams(dimension_semantics=sem, vmem_limit_bytes=VMEM_LIMIT)


def _full(shape):
    n = len(shape)
    return pl.BlockSpec(shape, lambda *_: (0,) * n)


def _rms(x, g):
    return x * lax.rsqrt(jnp.mean(x * x, axis=-1, keepdims=True) + EPS) * g


def _dot(a, b):
    return jnp.dot(a, b, preferred_element_type=F32)


def _store_v_with_ones(v_ref, v):
    lane = lax.broadcasted_iota(jnp.int32, (v.shape[0], LANE), 1)
    for p in range(N_PAIRS):
        vp = v[:, p * LANE:(p + 1) * LANE]
        v_ref[0, :, p * PAIR:p * PAIR + LANE] = jnp.where(lane < LANE // 2, vp, 1.0).astype(v_ref.dtype)
        v_ref[0, :, p * PAIR + LANE:(p + 1) * PAIR] = jnp.where(lane < LANE // 2, 1.0, vp).astype(v_ref.dtype)


def _rope_kernel(pos_ref, inv_ref, sign_ref, ca_ref, sb_ref):
    ang = pos_ref[0].astype(F32) * inv_ref[...]
    ca_ref[0] = jnp.cos(ang)
    sb_ref[0] = jnp.sin(ang) * sign_ref[...]


def _rope_tables(positions):
    B, S = positions.shape
    ts = min(S, 512)
    inv = 1.0 / (ROPE_THETA ** (jnp.arange(0, MLA_ROPE, 2, dtype=F32) / MLA_ROPE))
    half = MLA_ROPE // 2
    inv_row = jnp.zeros((1, LANE), F32).at[0, MLA_NOPE:MLA_NOPE + half].set(inv)
    inv_row = inv_row.at[0, MLA_NOPE + half:MLA_NOPE + MLA_ROPE].set(inv)
    sign = np.zeros((1, LANE), np.float32)
    sign[0, MLA_NOPE:MLA_NOPE + half] = -1.0
    sign[0, MLA_NOPE + half:MLA_NOPE + MLA_ROPE] = 1.0
    blk = pl.BlockSpec((1, ts, LANE), lambda b, s: (b, s, 0))
    return pl.pallas_call(
        _rope_kernel,
        grid=(B, S // ts),
        in_specs=[pl.BlockSpec((1, ts, 1), lambda b, s: (b, s, 0)), _full((1, LANE)), _full((1, LANE))],
        out_specs=[blk, blk],
        out_shape=[jax.ShapeDtypeStruct((B, S, LANE), F32)] * 2,
        compiler_params=_cparams(("parallel", "parallel")),
        name="rope_tables",
    )(positions.reshape(B, S, 1), inv_row, jnp.asarray(sign))


def _split3(x):
    hi = x.astype(BF16)
    r1 = x - hi.astype(F32)
    mid = r1.astype(BF16)
    return hi, mid, (r1 - mid.astype(F32)).astype(BF16)


def _fox_qk_norm(x, bd, gain):
    sq = (x * x).astype(BF16)
    ss = jnp.concatenate([_dot(sq[:, c:c + PAIR], bd) for c in range(0, FOX_WIDTH, PAIR)], axis=1)
    return (x * lax.rsqrt(ss * (1.0 / FOX_HEAD_DIM) + EPS) * gain).astype(BF16)


def _prep_kernel(x_ref, ca_ref, sb_ref, ng_ref, wcat_ref, qan_ref, wqa_ref, wqb_ref, kvn_ref, wk_ref, wv_ref,
                 gqm_ref, gkm_ref, bf_ref, gqf_ref, gkf_ref, bd_ref, tri_ref,
                 eq_ref, ek_ref, oq_ref, ok_ref,
                 qm_ref, km_ref, vm_ref, qf_ref, kf_ref, vf_ref, u_ref, carry_ref):
    @pl.when(pl.program_id(1) == 0)
    def _():
        carry_ref[...] = jnp.zeros_like(carry_ref)

    x = x_ref[0]
    h = _rms(x, ng_ref[...]).astype(BF16)
    proj = _dot(h, wcat_ref[...])
    ca = ca_ref[0]
    sb = sb_ref[0]

    cqn = _rms(proj[:, C_CQ:C_CQ + MLA_Q_RANK], qan_ref[...]).astype(BF16)
    qa = _dot(cqn, wqa_ref[...])
    qb = _dot(cqn, wqb_ref[...])
    ckvn = _rms(proj[:, C_CKV:C_CKV + MLA_KV_RANK], kvn_ref[...]).astype(BF16)
    kc = _dot(ckvn, wk_ref[...])
    _store_v_with_ones(vm_ref, _dot(ckvn, wv_ref[...]))
    kpe = proj[:, C_KPA:C_KPA + LANE] * ca + proj[:, C_KPB:C_KPB + LANE] * sb
    inv_d = 1.0 / MLA_QK_DIM
    for hd in range(MLA_HEADS):
        sl = slice(hd * HEAD_PAD, (hd + 1) * HEAD_PAD)
        qh = qa[:, sl] * ca + qb[:, sl] * sb
        ss = jnp.sum(qh * qh, axis=-1, keepdims=True)
        qm_ref[0, :, sl] = (qh * lax.rsqrt(ss * inv_d + EPS) * gqm_ref[...]).astype(BF16)
        kh = kc[:, sl] + kpe
        ss = jnp.sum(kh * kh, axis=-1, keepdims=True)
        km_ref[0, :, sl] = (kh * lax.rsqrt(ss * inv_d + EPS) * gkm_ref[...]).astype(BF16)

    fqn = _fox_qk_norm(proj[:, C_FQ:C_FQ + FOX_WIDTH], bd_ref[...], gqf_ref[...])
    fkn = _fox_qk_norm(proj[:, C_FK:C_FK + FOX_WIDTH], bd_ref[...], gkf_ref[...])
    _store_v_with_ones(vf_ref, proj[:, C_FV:C_FV + FOX_WIDTH])

    z = proj[:, C_FF:C_FF + LANE] + bf_ref[...]
    log_f = jnp.minimum(z, 0.0) - jnp.log1p(jnp.exp(-jnp.abs(z)))
    sums = _dot(tri_ref[...], jnp.concatenate(_split3(log_f), axis=1))
    cum = sums[:, :LANE] + sums[:, LANE:2 * LANE] + sums[:, 2 * LANE:] + carry_ref[...]
    carry_ref[...] = cum[cum.shape[0] - 1:, :]
    hi, mid, lo = _split3(cum * LOG2E)
    lane = lax.broadcasted_iota(jnp.int32, hi.shape, 1)
    pieces = jnp.where(lane < FOX_HEADS, hi, jnp.where(lane < 2 * FOX_HEADS, mid, lo))
    auxq = (_dot(pieces, eq_ref[...]) + oq_ref[...]).astype(BF16)
    auxk = (_dot(pieces, ek_ref[...]) + ok_ref[...]).astype(BF16)
    for p in range(N_PAIRS):
        src = slice(p * LANE, (p + 1) * LANE)
        qf_ref[0, :, p * PAIR:p * PAIR + LANE] = fqn[:, src]
        qf_ref[0, :, p * PAIR + LANE:(p + 1) * PAIR] = auxq[:, src]
        kf_ref[0, :, p * PAIR:p * PAIR + LANE] = fkn[:, src]
        kf_ref[0, :, p * PAIR + LANE:(p + 1) * PAIR] = auxk[:, src]

    u_ref[0] = proj[:, C_S5:C_S5 + S5_WIDTH]


def _layer(arr, l):
    n = arr.ndim - 1
    return pl.BlockSpec((None,) + arr.shape[1:], lambda *_: (l,) + (0,) * n)


def _prep_weights(pr):
    w_in = pr["w_in"]
    L = w_in.shape[0]
    row = lambda a: a[:, None, :]
    zpad = jnp.zeros((L, D_MODEL, MLA_NOPE), F32)
    kpe = w_in[:, :, O_KPE:O_KPE + MLA_ROPE]
    half = MLA_ROPE // 2
    kpe_sw = jnp.concatenate([kpe[:, :, half:], kpe[:, :, :half]], axis=2)
    tail = jnp.zeros((L, D_MODEL, LANE - MLA_NOPE - MLA_ROPE), F32)
    ff = jnp.concatenate([w_in[:, :, O_FF:O_FF + FOX_HEADS]] * N_SPLIT
                         + [jnp.zeros((L, D_MODEL, LANE - N_SPLIT * FOX_HEADS), F32)], axis=2)
    wcat = jnp.concatenate([
        w_in[:, :, O_CQ:O_CQ + MLA_Q_RANK + MLA_KV_RANK],
        zpad, kpe, tail, zpad, kpe_sw, tail,
        w_in[:, :, O_FQ:O_FQ + 3 * FOX_WIDTH], ff, w_in[:, :, O_S5U:O_S5U + S5_WIDTH]], axis=2).astype(BF16)

    wq = pr["mla_w_q_up"].reshape(L, MLA_Q_RANK, MLA_HEADS, MLA_QK_DIM)
    nope, pe = wq[..., :MLA_NOPE], wq[..., MLA_NOPE:]
    pe_sw = jnp.concatenate([pe[..., half:], pe[..., :half]], axis=3)
    z32 = jnp.zeros((L, MLA_Q_RANK, MLA_HEADS, HEAD_PAD - MLA_QK_DIM), F32)
    wqa = jnp.concatenate([nope, pe, z32], axis=3).reshape(L, MLA_Q_RANK, MLA_HEADS * HEAD_PAD).astype(BF16)
    wqb = jnp.concatenate([jnp.zeros_like(nope), pe_sw, z32], axis=3).reshape(L, MLA_Q_RANK, -1).astype(BF16)

    wkv = pr["mla_w_kv_up"].reshape(L, MLA_KV_RANK, MLA_HEADS, MLA_NOPE + MLA_V)
    wk = jnp.concatenate([wkv[..., :MLA_NOPE], jnp.zeros((L, MLA_KV_RANK, MLA_HEADS, HEAD_PAD - MLA_NOPE), F32)],
                         axis=3).reshape(L, MLA_KV_RANK, -1).astype(BF16)
    wv = wkv[..., MLA_NOPE:].reshape(L, MLA_KV_RANK, MLA_WIDTH).astype(BF16)

    pad = jnp.zeros((L, HEAD_PAD - MLA_QK_DIM), F32)
    gqm = row(jnp.concatenate([pr["mla_q_norm"], pad], axis=1) * (LOG2E / math.sqrt(MLA_QK_DIM)))
    gkm = row(jnp.concatenate([pr["mla_k_norm"], pad], axis=1))
    bf = row(jnp.concatenate([pr["fox_b_f"]] * N_SPLIT + [jnp.zeros((L, LANE - N_SPLIT * FOX_HEADS), F32)], axis=1))
    gqf = row(jnp.tile(pr["fox_q_norm"], (1, FOX_HEADS)) * (LOG2E / math.sqrt(FOX_HEAD_DIM)))
    gkf = row(jnp.tile(pr["fox_k_norm"], (1, FOX_HEADS)))
    return dict(ng=row(pr["norm_g"]), wcat=wcat, qan=row(pr["mla_q_a_norm"]), wqa=wqa, wqb=wqb,
                kvn=row(pr["mla_kv_a_norm"]), wk=wk, wv=wv, gqm=gqm, gkm=gkm, bf=bf, gqf=gqf, gkf=gkf)


def _prep_consts(tm):
    bd = np.kron(np.eye(PAIR // FOX_HEAD_DIM, dtype=np.float32), np.ones((FOX_HEAD_DIM, FOX_HEAD_DIM), np.float32))
    tri = np.tril(np.ones((tm, tm), np.float32))
    eq = np.zeros((LANE, N_PAIRS * LANE), np.float32)
    ek = np.zeros_like(eq)
    oq = np.zeros((1, N_PAIRS * LANE), np.float32)
    ok = np.zeros_like(oq)
    for hd in range(FOX_HEADS):
        p, a = divmod(hd, 2)
        base = p * LANE + 2 * N_SPLIT * a
        for i in range(N_SPLIT):
            ek[i * FOX_HEADS + hd, base + i] = -1.0
            oq[0, base + i] = 1.0
            eq[i * FOX_HEADS + hd, base + N_SPLIT + i] = 1.0
            ok[0, base + N_SPLIT + i] = 1.0
    return dict(bd=jnp.asarray(bd, BF16), tri=jnp.asarray(tri, BF16),
                eq=jnp.asarray(eq, BF16), ek=jnp.asarray(ek, BF16), oq=jnp.asarray(oq), ok=jnp.asarray(ok))


def _prep_call(x, ca, sb, w, c, l, tm):
    B, S, _ = x.shape
    tok = lambda width: pl.BlockSpec((1, tm, width), lambda b, s: (b, s, 0))
    stacked = [w["ng"], w["wcat"], w["qan"], w["wqa"], w["wqb"], w["kvn"], w["wk"], w["wv"],
               w["gqm"], w["gkm"], w["bf"], w["gqf"], w["gkf"]]
    consts = [c["bd"], c["tri"], c["eq"], c["ek"], c["oq"], c["ok"]]
    args = [x, ca, sb] + stacked + consts
    in_specs = ([tok(D_MODEL), tok(LANE), tok(LANE)] + [_layer(a, l) for a in stacked]
                + [_full(a.shape) for a in consts])
    widths = [MLA_HEADS * HEAD_PAD, MLA_HEADS * HEAD_PAD, N_PAIRS * PAIR, N_PAIRS * PAIR, N_PAIRS * PAIR, N_PAIRS * PAIR]
    out_shape = [jax.ShapeDtypeStruct((B, S, wd), BF16) for wd in widths]
    out_shape.append(jax.ShapeDtypeStruct((B, S, S5_WIDTH), F32))
    return pl.pallas_call(
        _prep_kernel,
        grid=(B, S // tm),
        in_specs=in_specs,
        out_specs=[tok(wd) for wd in widths] + [tok(S5_WIDTH)],
        out_shape=out_shape,
        scratch_shapes=[pltpu.VMEM((1, LANE), F32)],
        compiler_params=_cparams(("parallel", "arbitrary")),
        name="prep",
    )(*args)


def _attn_kernel(q_ref, k_ref, v_ref, hm_ref, o_ref, qm_ref, sa_ref, sb_ref, bias_ref, m_ref, acc_ref, *,
                 tq, chunk_causal):
    S = q_ref.shape[1]
    nq = S // tq
    steps = [(qi, j) for qi in range(nq) for j in range(qi + 1)]
    row = lax.broadcasted_iota(jnp.int32, (tq, tq), 0)
    col = lax.broadcasted_iota(jnp.int32, (tq, tq), 1)
    shift = int(math.log2(CHUNK)) if chunk_causal else 0
    bias_ref[...] = jnp.where((col >> shift) <= (row >> shift), 0.0, -jnp.inf)
    for a in range(2):
        qm_ref[a] = q_ref[0] * hm_ref[a:a + 1, :]
    lane = lax.broadcasted_iota(jnp.int32, (tq, LANE), 1)

    def qk(qi, j, s_ref):
        k = k_ref[0, j * tq:(j + 1) * tq, :]
        for a in range(2):
            q = qm_ref[a, qi * tq:(qi + 1) * tq, :]
            s_ref[a] = lax.dot_general(q, k, (((1,), (1,)), ((), ())), preferred_element_type=F32)

    qk(0, 0, sa_ref)
    bufs = (sa_ref, sb_ref)
    for t, (qi, j) in enumerate(steps):
        s_cur, s_next = bufs[t % 2], bufs[(t + 1) % 2]
        if t + 1 < len(steps):
            qk(*steps[t + 1], s_next)
        outs = []
        for a in range(2):
            s = s_cur[a]
            if j == qi:
                s = s + bias_ref[...]
            v = v_ref[0, j * tq:(j + 1) * tq, a * LANE:(a + 1) * LANE]
            m_cur = jnp.max(s, axis=-1, keepdims=True)
            if j == 0:
                m_new = jnp.broadcast_to(m_cur, (tq, LANE))
                p = jnp.exp2(s - m_cur)
                acc = _dot(p.astype(BF16), v)
            else:
                m_prev = m_ref[a]
                m_new = jnp.maximum(m_prev, m_cur)
                alpha = jnp.exp2(m_prev - m_new)
                p = jnp.exp2(s - jnp.concatenate([m_new] * (tq // LANE), axis=1))
                acc = alpha * acc_ref[a] + _dot(p.astype(BF16), v)
            if j == qi:
                outs.append(acc / pltpu.roll(acc, LANE // 2, axis=1))
            else:
                acc_ref[a] = acc
                m_ref[a] = m_new
        if j == qi:
            o_ref[0, qi * tq:(qi + 1) * tq, :] = jnp.where(lane < LANE // 2, outs[0], outs[1]).astype(o_ref.dtype)


def _attn_call(q, k, v, head_mask, chunk_causal, tq, name):
    B, S, _ = q.shape
    kern = functools.partial(_attn_kernel, tq=tq, chunk_causal=chunk_causal)
    return pl.pallas_call(
        kern,
        grid=(B, N_PAIRS),
        in_specs=[pl.BlockSpec((1, S, PAIR), lambda b, p: (b, 0, p)),
                  pl.BlockSpec((1, S, PAIR), lambda b, p: (b, 0, p)),
                  pl.BlockSpec((1, S, PAIR), lambda b, p: (b, 0, p)),
                  _full((2, PAIR))],
        out_specs=pl.BlockSpec((1, S, LANE), lambda b, p: (b, 0, p)),
        out_shape=jax.ShapeDtypeStruct((B, S, N_PAIRS * LANE), BF16),
        scratch_shapes=[pltpu.VMEM((2, S, PAIR), BF16), pltpu.VMEM((2, tq, tq), F32), pltpu.VMEM((2, tq, tq), F32),
                        pltpu.VMEM((tq, tq), F32), pltpu.VMEM((2, tq, LANE), F32), pltpu.VMEM((2, tq, LANE), F32)],
        compiler_params=_cparams(("parallel", "parallel")),
        name=name,
    )(q, k, v, head_mask)


def _head_masks():
    mla = np.zeros((2, PAIR), np.float32)
    fox = np.zeros((2, PAIR), np.float32)
    for a in range(2):
        mla[a, a * HEAD_PAD:(a + 1) * HEAD_PAD] = 1.0
        fox[a, a * FOX_HEAD_DIM:(a + 1) * FOX_HEAD_DIM] = 1.0
        fox[a, LANE + 2 * N_SPLIT * a:LANE + 2 * N_SPLIT * (a + 1)] = 1.0
    return jnp.asarray(mla, BF16), jnp.asarray(fox, BF16)


def _s5_param_kernel(lr_ref, li_ref, ldt_ref, br_ref, bi_ref, are_ref, aim_ref, bbr_ref, bbi_ref):
    lr = lr_ref[...]
    li = li_ref[...]
    dt = jnp.exp(ldt_ref[...])
    mag = jnp.exp(lr * dt)
    a_re = mag * jnp.cos(li * dt)
    a_im = mag * jnp.sin(li * dt)
    den = lr * lr + li * li
    f_re = ((a_re - 1.0) * lr + a_im * li) / den
    f_im = (a_im * lr - (a_re - 1.0) * li) / den
    br = br_ref[...]
    bi = bi_ref[...]
    are_ref[...] = a_re
    aim_ref[...] = a_im
    bbr_ref[...] = f_re * br - f_im * bi
    bbi_ref[...] = f_re * bi + f_im * br


def _s5_params(pr, n_batch):
    G, N, C = S5_GROUPS, S5_STATE, S5_GROUP
    L = pr["s5_lambda_re"].shape[0]
    rep = lambda a: jnp.repeat(a.reshape(L * G, N), C, axis=0)
    ldt = jnp.broadcast_to(pr["s5_log_dt"][:, :, None], (L, G, N))
    tr = lambda b: jnp.transpose(b, (0, 1, 3, 2)).reshape(L * G * C, N)
    shp = jax.ShapeDtypeStruct((L * G * C, N), F32)
    a_re, a_im, bb_re, bb_im = pl.pallas_call(
        _s5_param_kernel, out_shape=[shp] * 4, name="s5_params",
    )(rep(pr["s5_lambda_re"]), rep(pr["s5_lambda_im"]), rep(ldt), tr(pr["s5_b_re"]), tr(pr["s5_b_im"]))
    in_mask = jnp.asarray(np.arange(G * C)[:, None] // C == np.arange(G * N)[None, :] // N)
    out_mask = jnp.asarray(np.arange(G * N)[:, None] // N == np.arange(G * C)[None, :] // C)

    def blockdiag_in(bb):
        return jnp.where(in_mask, jnp.tile(bb.reshape(L, G * C, N), (1, 1, G)), 0.0)

    def blockdiag_out(cc):
        rows = jnp.transpose(cc, (0, 1, 3, 2)).reshape(L, G * N, C)
        return jnp.where(out_mask, jnp.tile(rows, (1, 1, G)), 0.0).astype(BF16)

    w_b = jnp.concatenate([blockdiag_in(bb_re), blockdiag_in(bb_im)], axis=2).astype(BF16)
    first = lambda a: a.reshape(L, G, C, N)[:, :, 0, :].reshape(L, 1, G * N)
    a_rep = jnp.broadcast_to(jnp.concatenate([first(a_re), first(a_im)], axis=2), (L, n_batch, 2 * G * N))
    return dict(a=a_rep, w_b=w_b, w_cre=blockdiag_out(pr["s5_c_re"]), w_cim=blockdiag_out(pr["s5_c_im"]),
                d=pr["s5_d"][:, None, :], w_glu=pr["s5_w_glu"].astype(BF16), b_glu=pr["s5_b_glu"][:, None, :])


def _s5_kernel(u_ref, a_ref, wb_ref, wcr_ref, wci_ref, d_ref, wg_ref, bg_ref, o_ref, st_ref, us_ref, zs_ref, bu_ref,
               xs_ref, *, nb, ts, n_sub):
    @pl.when(pl.program_id(0) == 0)
    def _():
        st_ref[...] = jnp.zeros_like(st_ref)

    N = S5_STATES
    pitch = ts + S5_ROW_PAD
    nk = S5_WIDTH // LANE

    for c in range(n_sub):
        for b in range(nb):
            for k in range(nk):
                us_ref[k, b * pitch:b * pitch + ts, :] = u_ref[b, c * ts:(c + 1) * ts, k * LANE:(k + 1) * LANE]
        u = jnp.concatenate(
            [jnp.concatenate([us_ref[k, pl.ds(t, nb, stride=pitch), :] for k in range(nk)], axis=1) for t in range(ts)],
            axis=0)
        ub = u.astype(BF16)
        for n in range(2 * N // PAIR):
            c0 = (n % (N // PAIR)) // 2 * LANE
            bu_ref[:, n * PAIR:(n + 1) * PAIR] = _dot(ub[:, c0:c0 + LANE], wb_ref[c0:c0 + LANE, n * PAIR:(n + 1) * PAIR])

        for g in range(N // S5_SCAN_LANES):
            re = slice(g * S5_SCAN_LANES, (g + 1) * S5_SCAN_LANES)
            im = slice(N + g * S5_SCAN_LANES, N + (g + 1) * S5_SCAN_LANES)
            ar = a_ref[:, re]
            ai = a_ref[:, im]

            def step(t, carry):
                xr, xi = carry
                r = pl.multiple_of(t * nb, nb)
                nr = ar * xr - ai * xi + bu_ref[pl.ds(r, nb), re]
                ni = ar * xi + ai * xr + bu_ref[pl.ds(r, nb), im]
                xs_ref[pl.ds(r, nb), re] = nr
                xs_ref[pl.ds(r, nb), im] = ni
                return nr, ni

            xr, xi = lax.fori_loop(0, ts, step, (st_ref[:, re], st_ref[:, im]), unroll=4)
            st_ref[:, re] = xr
            st_ref[:, im] = xi

        ys = []
        for t in range(S5_WIDTH // PAIR):
            r0 = t * (N // 2)
            xr = xs_ref[:, r0:r0 + N // 2].astype(BF16)
            xi = xs_ref[:, N + r0:N + r0 + N // 2].astype(BF16)
            cols = slice(t * PAIR, (t + 1) * PAIR)
            ys.append(_dot(xr, wcr_ref[r0:r0 + N // 2, cols]) - _dot(xi, wci_ref[r0:r0 + N // 2, cols]))
        y = jnp.concatenate(ys, axis=1) + d_ref[...] * u
        z = jax.nn.gelu(y, approximate=True)
        z = z * jax.nn.sigmoid(_dot(z.astype(BF16), wg_ref[...]) + bg_ref[...])
        for t in range(ts):
            for k in range(nk):
                zs_ref[k, pl.ds(t, nb, stride=pitch), :] = z[t * nb:(t + 1) * nb, k * LANE:(k + 1) * LANE]
        for b in range(nb):
            rows = jnp.concatenate([zs_ref[k, b * pitch:b * pitch + ts, :] for k in range(nk)], axis=1)
            o_ref[b, c * ts:(c + 1) * ts, :] = rows.astype(o_ref.dtype)


def _s5_call(u, p, l, ts, n_sub):
    B, S, _ = u.shape
    blk = ts * n_sub
    args = [u, p["a"], p["w_b"], p["w_cre"], p["w_cim"], p["d"], p["w_glu"], p["b_glu"]]
    kern = functools.partial(_s5_kernel, nb=B, ts=ts, n_sub=n_sub)
    once = lambda a: pl.BlockSpec((None,) + a.shape[1:], lambda i: (l,) + (0,) * (a.ndim - 1),
                                  pipeline_mode=pl.Buffered(1))
    slab_buf = pltpu.VMEM((S5_WIDTH // LANE, B * (ts + S5_ROW_PAD), LANE), F32)
    state_buf = pltpu.VMEM((ts * B, 2 * S5_STATES), F32)
    return pl.pallas_call(
        kern,
        grid=(S // blk,),
        in_specs=[pl.BlockSpec((B, blk, S5_WIDTH), lambda i: (0, i, 0))] + [once(a) for a in args[1:]],
        out_specs=pl.BlockSpec((B, blk, S5_WIDTH), lambda i: (0, i, 0)),
        out_shape=jax.ShapeDtypeStruct((B, S, S5_WIDTH), BF16),
        scratch_shapes=[pltpu.VMEM((B, 2 * S5_STATES), F32), slab_buf, slab_buf, state_buf, state_buf],
        compiler_params=_cparams(("arbitrary",)),
        name="s5",
    )(*args)


def _merge_kernel(x_ref, ym_ref, yf_ref, ys_ref, ng_ref, wgm_ref, wo_ref, wout_ref, o_ref):
    x = x_ref[0]
    h = _rms(x, ng_ref[...]).astype(BF16)
    merged = None
    for b, y_ref in enumerate((ym_ref, yf_ref, ys_ref)):
        g = _dot(h, wgm_ref[:, b * MLA_WIDTH:(b + 1) * MLA_WIDTH])
        gated = (y_ref[0].astype(F32) * (g * jax.nn.sigmoid(g))).astype(BF16)
        o = _dot(gated, wo_ref[b * MLA_WIDTH:(b + 1) * MLA_WIDTH, :])
        m = _dot(h, wgm_ref[:, BRANCH_WIDTH + b * D_MODEL:BRANCH_WIDTH + (b + 1) * D_MODEL])
        term = jax.nn.sigmoid(m) * o
        merged = term if merged is None else merged + term
    o_ref[0] = x + _dot(merged.astype(BF16), wout_ref[...])


def _merge_weights(pr):
    return [pr["norm_g"][:, None, :], pr["w_in"][:, :, O_GATE:O_END].astype(BF16),
            pr["w_branch_out"].astype(BF16), pr["w_out"].astype(BF16)]


def _merge_call(x, y_mla, y_fox, y_s5, w, l, tm):
    B, S, _ = x.shape
    tok = lambda width: pl.BlockSpec((1, tm, width), lambda b, s: (b, s, 0))
    args = [x, y_mla, y_fox, y_s5] + w
    return pl.pallas_call(
        _merge_kernel,
        grid=(B, S // tm),
        in_specs=[tok(D_MODEL), tok(MLA_WIDTH), tok(FOX_WIDTH), tok(S5_WIDTH)] + [_layer(a, l) for a in w],
        out_specs=tok(D_MODEL),
        out_shape=jax.ShapeDtypeStruct(x.shape, x.dtype),
        compiler_params=_cparams(("parallel", "parallel")),
        name="merge",
    )(*args)


_LAYER_KEYS = ("norm_g", "w_in", "mla_q_a_norm", "mla_w_q_up", "mla_kv_a_norm", "mla_w_kv_up", "mla_q_norm",
               "mla_k_norm", "fox_b_f", "fox_q_norm", "fox_k_norm", "s5_lambda_re", "s5_lambda_im", "s5_log_dt",
               "s5_b_re", "s5_b_im", "s5_c_re", "s5_c_im", "s5_d", "s5_w_glu", "s5_b_glu", "w_branch_out", "w_out")


def _forward(x, positions, params):
    B, S, _ = x.shape
    tm = min(S, 512)
    tq = min(S, 512)
    ts = min(S, 64)
    depth = params["w_in"].shape[0]
    ca, sb = _rope_tables(positions)
    consts = _prep_consts(tm)
    mask_mla, mask_fox = _head_masks()
    w_prep = _prep_weights(params)
    w_s5 = _s5_params(params, B)
    w_merge = _merge_weights(params)
    h = x
    for l in range(depth):
        qm, km, vm, qf, kf, vf, u = _prep_call(h, ca, sb, w_prep, consts, l, tm)
        y_mla = _attn_call(qm, km, vm, mask_mla, True, tq, "attn_mla")
        y_fox = _attn_call(qf, kf, vf, mask_fox, False, tq, "attn_fox")
        y_s5 = _s5_call(u, w_s5, l, ts, min(S // ts, 4))
        h = _merge_call(h, y_mla, y_fox, y_s5, w_merge, l, tm)
    return h


def kernel(x, positions, norm_g, w_in, mla_q_a_norm, mla_w_q_up, mla_kv_a_norm, mla_w_kv_up, mla_q_norm, mla_k_norm, fox_b_f, fox_q_norm, fox_k_norm, s5_lambda_re, s5_lambda_im, s5_log_dt, s5_b_re, s5_b_im, s5_c_re, s5_c_im, s5_d, s5_w_glu, s5_b_glu, w_branch_out, w_out):
    params = dict(zip(_LAYER_KEYS, (norm_g, w_in, mla_q_a_norm, mla_w_q_up, mla_kv_a_norm, mla_w_kv_up, mla_q_norm,
                                    mla_k_norm, fox_b_f, fox_q_norm, fox_k_norm, s5_lambda_re, s5_lambda_im,
                                    s5_log_dt, s5_b_re, s5_b_im, s5_c_re, s5_c_im, s5_d, s5_w_glu, s5_b_glu,
                                    w_branch_out, w_out)))
    return _forward(x, positions, params)
```

```python
import functools
import math

import numpy as np
import jax
import jax.numpy as jnp
from jax import lax
from jax.experimental import pallas as pl
from jax.experimental.pallas import tpu as pltpu

F32 = jnp.float32
BF16 = jnp.bfloat16

D_MODEL = 1024
CHUNK = 64
EPS = 1e-6

MLA_HEADS = 8
MLA_NOPE = 64
MLA_ROPE = 32
MLA_V = 64
MLA_Q_RANK = 256
MLA_KV_RANK = 128
MLA_WIDTH = MLA_HEADS * MLA_V
MLA_QK_DIM = MLA_NOPE + MLA_ROPE
ROPE_THETA = 10000.0

FOX_HEADS = 8
FOX_HEAD_DIM = 64
FOX_WIDTH = FOX_HEADS * FOX_HEAD_DIM

S5_WIDTH = 512
S5_GROUP = 16
S5_GROUPS = S5_WIDTH // S5_GROUP
S5_STATE = 64
S5_STATES = S5_GROUPS * S5_STATE
S5_ROW_PAD = 8
S5_SCAN_LANES = 1024

BRANCH_WIDTH = MLA_WIDTH + FOX_WIDTH + S5_WIDTH

LANE = 128
HEAD_PAD = 128
PAIR = 2 * HEAD_PAD
N_PAIRS = 4
N_SPLIT = 3
LOG2E = math.log2(math.e)

_OFF = np.cumsum([0, MLA_Q_RANK, MLA_KV_RANK, MLA_ROPE, FOX_WIDTH, FOX_WIDTH, FOX_WIDTH, FOX_HEADS,
                  S5_WIDTH, MLA_WIDTH, FOX_WIDTH, S5_WIDTH, D_MODEL, D_MODEL, D_MODEL]).tolist()
(O_CQ, O_CKV, O_KPE, O_FQ, O_FK, O_FV, O_FF, O_S5U, O_GATE, _o1, _o2, O_MERGE, _o3, _o4, O_END) = _OFF

C_CQ = 0
C_CKV = C_CQ + MLA_Q_RANK
C_KPA = C_CKV + MLA_KV_RANK
C_KPB = C_KPA + LANE
C_FQ = C_KPB + LANE
C_FK = C_FQ + FOX_WIDTH
C_FV = C_FK + FOX_WIDTH
C_FF = C_FV + FOX_WIDTH
C_S5 = C_FF + LANE
C_END = C_S5 + S5_WIDTH

VMEM_LIMIT = 56 * 1024 * 1024


def _cparams(sem):
    return pltpu.CompilerParams(dimension_semantics=sem, vmem_limit_bytes=VMEM_LIMIT)


def _full(shape):
    n = len(shape)
    return pl.BlockSpec(shape, lambda *_: (0,) * n)


def _rms(x, g):
    return x * lax.rsqrt(jnp.mean(x * x, axis=-1, keepdims=True) + EPS) * g


def _dot(a, b):
    return jnp.dot(a, b, preferred_element_type=F32)


def _store_v_with_ones(v_ref, v):
    lane = lax.broadcasted_iota(jnp.int32, (v.shape[0], LANE), 1)
    for p in range(N_PAIRS):
        vp = v[:, p * LANE:(p + 1) * LANE]
        v_ref[0, :, p * PAIR:p * PAIR + LANE] = jnp.where(lane < LANE // 2, vp, 1.0).astype(v_ref.dtype)
        v_ref[0, :, p * PAIR + LANE:(p + 1) * PAIR] = jnp.where(lane < LANE // 2, 1.0, vp).astype(v_ref.dtype)


def _rope_kernel(pos_ref, inv_ref, sign_ref, ca_ref, sb_ref):
    ang = pos_ref[0].astype(F32) * inv_ref[...]
    ca_ref[0] = jnp.cos(ang)
    sb_ref[0] = jnp.sin(ang) * sign_ref[...]


def _rope_tables(positions):
    B, S = positions.shape
    ts = min(S, 512)
    inv = 1.0 / (ROPE_THETA ** (jnp.arange(0, MLA_ROPE, 2, dtype=F32) / MLA_ROPE))
    half = MLA_ROPE // 2
    inv_row = jnp.zeros((1, LANE), F32).at[0, MLA_NOPE:MLA_NOPE + half].set(inv)
    inv_row = inv_row.at[0, MLA_NOPE + half:MLA_NOPE + MLA_ROPE].set(inv)
    sign = np.zeros((1, LANE), np.float32)
    sign[0, MLA_NOPE:MLA_NOPE + half] = -1.0
    sign[0, MLA_NOPE + half:MLA_NOPE + MLA_ROPE] = 1.0
    blk = pl.BlockSpec((1, ts, LANE), lambda b, s: (b, s, 0))
    return pl.pallas_call(
        _rope_kernel,
        grid=(B, S // ts),
        in_specs=[pl.BlockSpec((1, ts, 1), lambda b, s: (b, s, 0)), _full((1, LANE)), _full((1, LANE))],
        out_specs=[blk, blk],
        out_shape=[jax.ShapeDtypeStruct((B, S, LANE), F32)] * 2,
        compiler_params=_cparams(("parallel", "parallel")),
        name="rope_tables",
    )(positions.reshape(B, S, 1), inv_row, jnp.asarray(sign))


def _split3(x):
    hi = x.astype(BF16)
    r1 = x - hi.astype(F32)
    mid = r1.astype(BF16)
    return hi, mid, (r1 - mid.astype(F32)).astype(BF16)


def _fox_qk_norm(x, bd, gain):
    sq = (x * x).astype(BF16)
    ss = jnp.concatenate([_dot(sq[:, c:c + PAIR], bd) for c in range(0, FOX_WIDTH, PAIR)], axis=1)
    return (x * lax.rsqrt(ss * (1.0 / FOX_HEAD_DIM) + EPS) * gain).astype(BF16)


def _prep_kernel(x_ref, ca_ref, sb_ref, ng_ref, wcat_ref, qan_ref, wqa_ref, wqb_ref, kvn_ref, wk_ref, wv_ref,
                 gqm_ref, gkm_ref, bf_ref, gqf_ref, gkf_ref, bd_ref, tri_ref,
                 eq_ref, ek_ref, oq_ref, ok_ref,
                 qm_ref, km_ref, vm_ref, qf_ref, kf_ref, vf_ref, u_ref, carry_ref):
    @pl.when(pl.program_id(1) == 0)
    def _():
        carry_ref[...] = jnp.zeros_like(carry_ref)

    x = x_ref[0]
    h = _rms(x, ng_ref[...]).astype(BF16)
    proj = _dot(h, wcat_ref[...])
    ca = ca_ref[0]
    sb = sb_ref[0]

    cqn = _rms(proj[:, C_CQ:C_CQ + MLA_Q_RANK], qan_ref[...]).astype(BF16)
    qa = _dot(cqn, wqa_ref[...])
    qb = _dot(cqn, wqb_ref[...])
    ckvn = _rms(proj[:, C_CKV:C_CKV + MLA_KV_RANK], kvn_ref[...]).astype(BF16)
    kc = _dot(ckvn, wk_ref[...])
    _store_v_with_ones(vm_ref, _dot(ckvn, wv_ref[...]))
    kpe = proj[:, C_KPA:C_KPA + LANE] * ca + proj[:, C_KPB:C_KPB + LANE] * sb
    inv_d = 1.0 / MLA_QK_DIM
    for hd in range(MLA_HEADS):
        sl = slice(hd * HEAD_PAD, (hd + 1) * HEAD_PAD)
        qh = qa[:, sl] * ca + qb[:, sl] * sb
        ss = jnp.sum(qh * qh, axis=-1, keepdims=True)
        qm_ref[0, :, sl] = (qh * lax.rsqrt(ss * inv_d + EPS) * gqm_ref[...]).astype(BF16)
        kh = kc[:, sl] + kpe
        ss = jnp.sum(kh * kh, axis=-1, keepdims=True)
        km_ref[0, :, sl] = (kh * lax.rsqrt(ss * inv_d + EPS) * gkm_ref[...]).astype(BF16)

    fqn = _fox_qk_norm(proj[:, C_FQ:C_FQ + FOX_WIDTH], bd_ref[...], gqf_ref[...])
    fkn = _fox_qk_norm(proj[:, C_FK:C_FK + FOX_WIDTH], bd_ref[...], gkf_ref[...])
    _store_v_with_ones(vf_ref, proj[:, C_FV:C_FV + FOX_WIDTH])

    z = proj[:, C_FF:C_FF + LANE] + bf_ref[...]
    log_f = jnp.minimum(z, 0.0) - jnp.log1p(jnp.exp(-jnp.abs(z)))
    sums = _dot(tri_ref[...], jnp.concatenate(_split3(log_f), axis=1))
    cum = sums[:, :LANE] + sums[:, LANE:2 * LANE] + sums[:, 2 * LANE:] + carry_ref[...]
    carry_ref[...] = cum[cum.shape[0] - 1:, :]
    hi, mid, lo = _split3(cum * LOG2E)
    lane = lax.broadcasted_iota(jnp.int32, hi.shape, 1)
    pieces = jnp.where(lane < FOX_HEADS, hi, jnp.where(lane < 2 * FOX_HEADS, mid, lo))
    auxq = (_dot(pieces, eq_ref[...]) + oq_ref[...]).astype(BF16)
    auxk = (_dot(pieces, ek_ref[...]) + ok_ref[...]).astype(BF16)
    for p in range(N_PAIRS):
        src = slice(p * LANE, (p + 1) * LANE)
        qf_ref[0, :, p * PAIR:p * PAIR + LANE] = fqn[:, src]
        qf_ref[0, :, p * PAIR + LANE:(p + 1) * PAIR] = auxq[:, src]
        kf_ref[0, :, p * PAIR:p * PAIR + LANE] = fkn[:, src]
        kf_ref[0, :, p * PAIR + LANE:(p + 1) * PAIR] = auxk[:, src]

    u_ref[0] = proj[:, C_S5:C_S5 + S5_WIDTH]


def _layer(arr, l):
    n = arr.ndim - 1
    return pl.BlockSpec((None,) + arr.shape[1:], lambda *_: (l,) + (0,) * n)


def _prep_weights(pr):
    w_in = pr["w_in_bf16"]
    L = w_in.shape[0]
    row = lambda a: a[:, None, :]
    zpad = jnp.zeros((L, D_MODEL, MLA_NOPE), BF16)
    kpe = w_in[:, :, O_KPE:O_KPE + MLA_ROPE]
    half = MLA_ROPE // 2
    kpe_sw = jnp.concatenate([kpe[:, :, half:], kpe[:, :, :half]], axis=2)
    tail = jnp.zeros((L, D_MODEL, LANE - MLA_NOPE - MLA_ROPE), BF16)
    ff = jnp.concatenate([w_in[:, :, O_FF:O_FF + FOX_HEADS]] * N_SPLIT
                         + [jnp.zeros((L, D_MODEL, LANE - N_SPLIT * FOX_HEADS), BF16)], axis=2)
    wcat = jnp.concatenate([
        w_in[:, :, O_CQ:O_CQ + MLA_Q_RANK + MLA_KV_RANK],
        zpad, kpe, tail, zpad, kpe_sw, tail,
        w_in[:, :, O_FQ:O_FQ + 3 * FOX_WIDTH], ff, w_in[:, :, O_S5U:O_S5U + S5_WIDTH]], axis=2)

    wq = pr["mla_w_q_up"].reshape(L, MLA_Q_RANK, MLA_HEADS, MLA_QK_DIM)
    nope, pe = wq[..., :MLA_NOPE], wq[..., MLA_NOPE:]
    pe_sw = jnp.concatenate([pe[..., half:], pe[..., :half]], axis=3)
    z32 = jnp.zeros((L, MLA_Q_RANK, MLA_HEADS, HEAD_PAD - MLA_QK_DIM), F32)
    wqa = jnp.concatenate([nope, pe, z32], axis=3).reshape(L, MLA_Q_RANK, MLA_HEADS * HEAD_PAD).astype(BF16)
    wqb = jnp.concatenate([jnp.zeros_like(nope), pe_sw, z32], axis=3).reshape(L, MLA_Q_RANK, -1).astype(BF16)

    wkv = pr["mla_w_kv_up"].reshape(L, MLA_KV_RANK, MLA_HEADS, MLA_NOPE + MLA_V)
    wk = jnp.concatenate([wkv[..., :MLA_NOPE], jnp.zeros((L, MLA_KV_RANK, MLA_HEADS, HEAD_PAD - MLA_NOPE), F32)],
                         axis=3).reshape(L, MLA_KV_RANK, -1).astype(BF16)
    wv = wkv[..., MLA_NOPE:].reshape(L, MLA_KV_RANK, MLA_WIDTH).astype(BF16)

    pad = jnp.zeros((L, HEAD_PAD - MLA_QK_DIM), F32)
    gqm = row(jnp.concatenate([pr["mla_q_norm"], pad], axis=1) * (LOG2E / math.sqrt(MLA_QK_DIM)))
    gkm = row(jnp.concatenate([pr["mla_k_norm"], pad], axis=1))
    bf = row(jnp.concatenate([pr["fox_b_f"]] * N_SPLIT + [jnp.zeros((L, LANE - N_SPLIT * FOX_HEADS), F32)], axis=1))
    gqf = row(jnp.tile(pr["fox_q_norm"], (1, FOX_HEADS)) * (LOG2E / math.sqrt(FOX_HEAD_DIM)))
    gkf = row(jnp.tile(pr["fox_k_norm"], (1, FOX_HEADS)))
    return dict(ng=row(pr["norm_g"]), wcat=wcat, qan=row(pr["mla_q_a_norm"]), wqa=wqa, wqb=wqb,
                kvn=row(pr["mla_kv_a_norm"]), wk=wk, wv=wv, gqm=gqm, gkm=gkm, bf=bf, gqf=gqf, gkf=gkf)


def _prep_consts(tm):
    bd = np.kron(np.eye(PAIR // FOX_HEAD_DIM, dtype=np.float32), np.ones((FOX_HEAD_DIM, FOX_HEAD_DIM), np.float32))
    tri = np.tril(np.ones((tm, tm), np.float32))
    eq = np.zeros((LANE, N_PAIRS * LANE), np.float32)
    ek = np.zeros_like(eq)
    oq = np.zeros((1, N_PAIRS * LANE), np.float32)
    ok = np.zeros_like(oq)
    for hd in range(FOX_HEADS):
        p, a = divmod(hd, 2)
        base = p * LANE + 2 * N_SPLIT * a
        for i in range(N_SPLIT):
            ek[i * FOX_HEADS + hd, base + i] = -1.0
            oq[0, base + i] = 1.0
            eq[i * FOX_HEADS + hd, base + N_SPLIT + i] = 1.0
            ok[0, base + N_SPLIT + i] = 1.0
    return dict(bd=jnp.asarray(bd, BF16), tri=jnp.asarray(tri, BF16),
                eq=jnp.asarray(eq, BF16), ek=jnp.asarray(ek, BF16), oq=jnp.asarray(oq), ok=jnp.asarray(ok))


def _prep_call(x, ca, sb, w, c, l, tm):
    B, S, _ = x.shape
    tok = lambda width: pl.BlockSpec((1, tm, width), lambda b, s: (b, s, 0))
    stacked = [w["ng"], w["wcat"], w["qan"], w["wqa"], w["wqb"], w["kvn"], w["wk"], w["wv"],
               w["gqm"], w["gkm"], w["bf"], w["gqf"], w["gkf"]]
    consts = [c["bd"], c["tri"], c["eq"], c["ek"], c["oq"], c["ok"]]
    args = [x, ca, sb] + stacked + consts
    in_specs = ([tok(D_MODEL), tok(LANE), tok(LANE)] + [_layer(a, l) for a in stacked]
                + [_full(a.shape) for a in consts])
    widths = [MLA_HEADS * HEAD_PAD, MLA_HEADS * HEAD_PAD, N_PAIRS * PAIR, N_PAIRS * PAIR, N_PAIRS * PAIR, N_PAIRS * PAIR]
    out_shape = [jax.ShapeDtypeStruct((B, S, wd), BF16) for wd in widths]
    out_shape.append(jax.ShapeDtypeStruct((B, S, S5_WIDTH), F32))
    return pl.pallas_call(
        _prep_kernel,
        grid=(B, S // tm),
        in_specs=in_specs,
        out_specs=[tok(wd) for wd in widths] + [tok(S5_WIDTH)],
        out_shape=out_shape,
        scratch_shapes=[pltpu.VMEM((1, LANE), F32)],
        compiler_params=_cparams(("parallel", "arbitrary")),
        name="prep",
    )(*args)


def _diag_tile(tq):
    return tq // 2 if tq >= 2 * PAIR else tq


def _attn_kernel(q_ref, k_ref, v_ref, hm_ref, o_ref, qm_ref, sa_ref, sb_ref, bias_ref, m_ref, acc_ref, *,
                 tq, chunk_causal):
    S = q_ref.shape[1]
    nq = S // tq
    td = _diag_tile(tq)
    steps = [(qi, j) for qi in range(nq) for j in range(qi + 1)]
    row = lax.broadcasted_iota(jnp.int32, (td, td), 0)
    col = lax.broadcasted_iota(jnp.int32, (td, td), 1)
    shift = int(math.log2(CHUNK)) if chunk_causal else 0
    bias_ref[...] = jnp.where((col >> shift) <= (row >> shift), 0.0, -jnp.inf)
    for a in range(2):
        qm_ref[a] = q_ref[0] * hm_ref[a:a + 1, :]

    def row_blocks(qi, j):
        if j < qi:
            return [(0, tq, tq)]
        return [(r * td, td, (r + 1) * td) for r in range(tq // td)]

    def qk(qi, j, s_ref):
        for r0, nr, nk in row_blocks(qi, j):
            k = k_ref[0, j * tq:j * tq + nk, :]
            for a in range(2):
                q = qm_ref[a, qi * tq + r0:qi * tq + r0 + nr, :]
                s_ref[a, r0:r0 + nr, :nk] = lax.dot_general(q, k, (((1,), (1,)), ((), ())),
                                                            preferred_element_type=F32)

    qk(0, 0, sa_ref)
    bufs = (sa_ref, sb_ref)
    for t, (qi, j) in enumerate(steps):
        s_cur, s_next = bufs[t % 2], bufs[(t + 1) % 2]
        if t + 1 < len(steps):
            qk(*steps[t + 1], s_next)
        for r0, nr, nk in row_blocks(qi, j):
            rows = slice(r0, r0 + nr)
            lane = lax.broadcasted_iota(jnp.int32, (nr, LANE), 1)
            outs = []
            for a in range(2):
                s = s_cur[a, rows, :nk]
                if j == qi:
                    tail = s[:, nk - td:] + bias_ref[...]
                    s = tail if nk == td else jnp.concatenate([s[:, :nk - td], tail], axis=1)
                v = v_ref[0, j * tq:j * tq + nk, a * LANE:(a + 1) * LANE]
                m_cur = jnp.max(s, axis=-1, keepdims=True)
                if j == 0:
                    m_new = jnp.broadcast_to(m_cur, (nr, LANE))
                    p = jnp.exp2(s - m_cur)
                    acc = _dot(p.astype(BF16), v)
                else:
                    m_prev = m_ref[a, rows]
                    m_new = jnp.maximum(m_prev, m_cur)
                    alpha = jnp.exp2(m_prev - m_new)
                    p = jnp.exp2(s - jnp.concatenate([m_new] * (nk // LANE), axis=1))
                    acc = alpha * acc_ref[a, rows] + _dot(p.astype(BF16), v)
                if j == qi:
                    outs.append(acc / pltpu.roll(acc, LANE // 2, axis=1))
                else:
                    acc_ref[a, rows] = acc
                    m_ref[a, rows] = m_new
            if j == qi:
                out = jnp.where(lane < LANE // 2, outs[0], outs[1]).astype(o_ref.dtype)
                o_ref[0, qi * tq + r0:qi * tq + r0 + nr, :] = out


def _attn_call(q, k, v, head_mask, chunk_causal, tq, name):
    B, S, _ = q.shape
    kern = functools.partial(_attn_kernel, tq=tq, chunk_causal=chunk_causal)
    td = _diag_tile(tq)
    return pl.pallas_call(
        kern,
        grid=(B, N_PAIRS),
        in_specs=[pl.BlockSpec((1, S, PAIR), lambda b, p: (b, 0, p)),
                  pl.BlockSpec((1, S, PAIR), lambda b, p: (b, 0, p)),
                  pl.BlockSpec((1, S, PAIR), lambda b, p: (b, 0, p)),
                  _full((2, PAIR))],
        out_specs=pl.BlockSpec((1, S, LANE), lambda b, p: (b, 0, p)),
        out_shape=jax.ShapeDtypeStruct((B, S, N_PAIRS * LANE), BF16),
        scratch_shapes=[pltpu.VMEM((2, S, PAIR), BF16), pltpu.VMEM((2, tq, tq), F32), pltpu.VMEM((2, tq, tq), F32),
                        pltpu.VMEM((td, td), F32), pltpu.VMEM((2, tq, LANE), F32), pltpu.VMEM((2, tq, LANE), F32)],
        compiler_params=_cparams(("parallel", "parallel")),
        name=name,
    )(q, k, v, head_mask)


def _head_masks():
    mla = np.zeros((2, PAIR), np.float32)
    fox = np.zeros((2, PAIR), np.float32)
    for a in range(2):
        mla[a, a * HEAD_PAD:(a + 1) * HEAD_PAD] = 1.0
        fox[a, a * FOX_HEAD_DIM:(a + 1) * FOX_HEAD_DIM] = 1.0
        fox[a, LANE + 2 * N_SPLIT * a:LANE + 2 * N_SPLIT * (a + 1)] = 1.0
    return jnp.asarray(mla, BF16), jnp.asarray(fox, BF16)


def _s5_param_kernel(lr_ref, li_ref, ldt_ref, br_ref, bi_ref, are_ref, aim_ref, bbr_ref, bbi_ref):
    lr = lr_ref[...]
    li = li_ref[...]
    dt = jnp.exp(ldt_ref[...])
    mag = jnp.exp(lr * dt)
    a_re = mag * jnp.cos(li * dt)
    a_im = mag * jnp.sin(li * dt)
    den = lr * lr + li * li
    f_re = ((a_re - 1.0) * lr + a_im * li) / den
    f_im = (a_im * lr - (a_re - 1.0) * li) / den
    br = br_ref[...]
    bi = bi_ref[...]
    are_ref[...] = a_re
    aim_ref[...] = a_im
    bbr_ref[...] = f_re * br - f_im * bi
    bbi_ref[...] = f_re * bi + f_im * br


def _s5_params(pr, n_batch):
    G, N, C = S5_GROUPS, S5_STATE, S5_GROUP
    L = pr["s5_lambda_re"].shape[0]
    rep = lambda a: jnp.repeat(a.reshape(L * G, N), C, axis=0)
    ldt = jnp.broadcast_to(pr["s5_log_dt"][:, :, None], (L, G, N))
    tr = lambda b: jnp.transpose(b, (0, 1, 3, 2)).reshape(L * G * C, N)
    shp = jax.ShapeDtypeStruct((L * G * C, N), F32)
    a_re, a_im, bb_re, bb_im = pl.pallas_call(
        _s5_param_kernel, out_shape=[shp] * 4, name="s5_params",
    )(rep(pr["s5_lambda_re"]), rep(pr["s5_lambda_im"]), rep(ldt), tr(pr["s5_b_re"]), tr(pr["s5_b_im"]))
    in_mask = jnp.asarray(np.arange(G * C)[:, None] // C == np.arange(G * N)[None, :] // N)
    out_mask = jnp.asarray(np.arange(G * N)[:, None] // N == np.arange(G * C)[None, :] // C)

    def blockdiag_in(bb):
        return jnp.where(in_mask, jnp.tile(bb.reshape(L, G * C, N), (1, 1, G)), 0.0)

    def blockdiag_out(cc):
        rows = jnp.transpose(cc, (0, 1, 3, 2)).reshape(L, G * N, C)
        return jnp.where(out_mask, jnp.tile(rows, (1, 1, G)), 0.0).astype(BF16)

    w_b = jnp.concatenate([blockdiag_in(bb_re), blockdiag_in(bb_im)], axis=2).astype(BF16)
    first = lambda a: a.reshape(L, G, C, N)[:, :, 0, :].reshape(L, 1, G * N)
    a_rep = jnp.broadcast_to(jnp.concatenate([first(a_re), first(a_im)], axis=2), (L, n_batch, 2 * G * N))
    return dict(a=a_rep, w_b=w_b, w_cre=blockdiag_out(pr["s5_c_re"]), w_cim=blockdiag_out(pr["s5_c_im"]),
                d=pr["s5_d"][:, None, :], w_glu=pr["s5_w_glu"].astype(BF16), b_glu=pr["s5_b_glu"][:, None, :])


def _s5_kernel(u_ref, a_ref, wb_ref, wcr_ref, wci_ref, d_ref, wg_ref, bg_ref, o_ref, st_ref, us_ref, zs_ref, bu_ref,
               xs_ref, *, nb, ts, n_sub):
    @pl.when(pl.program_id(0) == 0)
    def _():
        st_ref[...] = jnp.zeros_like(st_ref)

    N = S5_STATES
    pitch = ts + S5_ROW_PAD
    nk = S5_WIDTH // LANE

    for c in range(n_sub):
        for b in range(nb):
            for k in range(nk):
                us_ref[k, b * pitch:b * pitch + ts, :] = u_ref[b, c * ts:(c + 1) * ts, k * LANE:(k + 1) * LANE]
        u = jnp.concatenate(
            [jnp.concatenate([us_ref[k, pl.ds(t, nb, stride=pitch), :] for k in range(nk)], axis=1) for t in range(ts)],
            axis=0)
        ub = u.astype(BF16)
        for n in range(2 * N // PAIR):
            c0 = (n % (N // PAIR)) // 2 * LANE
            bu_ref[:, n * PAIR:(n + 1) * PAIR] = _dot(ub[:, c0:c0 + LANE], wb_ref[c0:c0 + LANE, n * PAIR:(n + 1) * PAIR])

        for g in range(N // S5_SCAN_LANES):
            re = slice(g * S5_SCAN_LANES, (g + 1) * S5_SCAN_LANES)
            im = slice(N + g * S5_SCAN_LANES, N + (g + 1) * S5_SCAN_LANES)
            ar = a_ref[:, re]
            ai = a_ref[:, im]

            def step(t, carry):
                xr, xi = carry
                r = pl.multiple_of(t * nb, nb)
                nr = ar * xr - ai * xi + bu_ref[pl.ds(r, nb), re]
                ni = ar * xi + ai * xr + bu_ref[pl.ds(r, nb), im]
                xs_ref[pl.ds(r, nb), re] = nr
                xs_ref[pl.ds(r, nb), im] = ni
                return nr, ni

            xr, xi = lax.fori_loop(0, ts, step, (st_ref[:, re], st_ref[:, im]), unroll=4)
            st_ref[:, re] = xr
            st_ref[:, im] = xi

        ys = []
        for t in range(S5_WIDTH // PAIR):
            r0 = t * (N // 2)
            xr = xs_ref[:, r0:r0 + N // 2].astype(BF16)
            xi = xs_ref[:, N + r0:N + r0 + N // 2].astype(BF16)
            cols = slice(t * PAIR, (t + 1) * PAIR)
            ys.append(_dot(xr, wcr_ref[r0:r0 + N // 2, cols]) - _dot(xi, wci_ref[r0:r0 + N // 2, cols]))
        y = jnp.concatenate(ys, axis=1) + d_ref[...] * u
        z = jax.nn.gelu(y, approximate=True)
        z = z * jax.nn.sigmoid(_dot(z.astype(BF16), wg_ref[...]) + bg_ref[...])
        for t in range(ts):
            for k in range(nk):
                zs_ref[k, pl.ds(t, nb, stride=pitch), :] = z[t * nb:(t + 1) * nb, k * LANE:(k + 1) * LANE]
        for b in range(nb):
            rows = jnp.concatenate([zs_ref[k, b * pitch:b * pitch + ts, :] for k in range(nk)], axis=1)
            o_ref[b, c * ts:(c + 1) * ts, :] = rows.astype(o_ref.dtype)


def _s5_call(u, p, l, ts, n_sub):
    B, S, _ = u.shape
    blk = ts * n_sub
    args = [u, p["a"], p["w_b"], p["w_cre"], p["w_cim"], p["d"], p["w_glu"], p["b_glu"]]
    kern = functools.partial(_s5_kernel, nb=B, ts=ts, n_sub=n_sub)
    once = lambda a: pl.BlockSpec((None,) + a.shape[1:], lambda i: (l,) + (0,) * (a.ndim - 1),
                                  pipeline_mode=pl.Buffered(1))
    slab_buf = pltpu.VMEM((S5_WIDTH // LANE, B * (ts + S5_ROW_PAD), LANE), F32)
    state_buf = pltpu.VMEM((ts * B, 2 * S5_STATES), F32)
    return pl.pallas_call(
        kern,
        grid=(S // blk,),
        in_specs=[pl.BlockSpec((B, blk, S5_WIDTH), lambda i: (0, i, 0))] + [once(a) for a in args[1:]],
        out_specs=pl.BlockSpec((B, blk, S5_WIDTH), lambda i: (0, i, 0)),
        out_shape=jax.ShapeDtypeStruct((B, S, S5_WIDTH), BF16),
        scratch_shapes=[pltpu.VMEM((B, 2 * S5_STATES), F32), slab_buf, slab_buf, state_buf, state_buf],
        compiler_params=_cparams(("arbitrary",)),
        name="s5",
    )(*args)


def _merge_kernel(x_ref, ym_ref, yf_ref, ys_ref, ng_ref, wgm_ref, wo_ref, wout_ref, o_ref):
    x = x_ref[0]
    h = _rms(x, ng_ref[...]).astype(BF16)
    merged = None
    for b, y_ref in enumerate((ym_ref, yf_ref, ys_ref)):
        g = _dot(h, wgm_ref[:, b * MLA_WIDTH:(b + 1) * MLA_WIDTH])
        gated = (y_ref[0].astype(F32) * (g * jax.nn.sigmoid(g))).astype(BF16)
        o = _dot(gated, wo_ref[b * MLA_WIDTH:(b + 1) * MLA_WIDTH, :])
        m = _dot(h, wgm_ref[:, BRANCH_WIDTH + b * D_MODEL:BRANCH_WIDTH + (b + 1) * D_MODEL])
        term = jax.nn.sigmoid(m) * o
        merged = term if merged is None else merged + term
    o_ref[0] = x + _dot(merged.astype(BF16), wout_ref[...])


def _merge_weights(pr):
    return [pr["norm_g"][:, None, :], pr["w_in_bf16"][:, :, O_GATE:O_END],
            pr["w_branch_out"].astype(BF16), pr["w_out"].astype(BF16)]


def _merge_call(x, y_mla, y_fox, y_s5, w, l, tm):
    B, S, _ = x.shape
    tok = lambda width: pl.BlockSpec((1, tm, width), lambda b, s: (b, s, 0))
    args = [x, y_mla, y_fox, y_s5] + w
    return pl.pallas_call(
        _merge_kernel,
        grid=(B, S // tm),
        in_specs=[tok(D_MODEL), tok(MLA_WIDTH), tok(FOX_WIDTH), tok(S5_WIDTH)] + [_layer(a, l) for a in w],
        out_specs=tok(D_MODEL),
        out_shape=jax.ShapeDtypeStruct(x.shape, x.dtype),
        compiler_params=_cparams(("parallel", "parallel")),
        name="merge",
    )(*args)


_LAYER_KEYS = ("norm_g", "w_in", "mla_q_a_norm", "mla_w_q_up", "mla_kv_a_norm", "mla_w_kv_up", "mla_q_norm",
               "mla_k_norm", "fox_b_f", "fox_q_norm", "fox_k_norm", "s5_lambda_re", "s5_lambda_im", "s5_log_dt",
               "s5_b_re", "s5_b_im", "s5_c_re", "s5_c_im", "s5_d", "s5_w_glu", "s5_b_glu", "w_branch_out", "w_out")


def _forward(x, positions, params):
    B, S, _ = x.shape
    tm = min(S, 512)
    tq = min(S, 512)
    ts = min(S, 64)
    depth = params["w_in"].shape[0]
    ca, sb = _rope_tables(positions)
    consts = _prep_consts(tm)
    mask_mla, mask_fox = _head_masks()
    params = dict(params, w_in_bf16=params["w_in"].astype(BF16))
    w_prep = _prep_weights(params)
    w_s5 = _s5_params(params, B)
    w_merge = _merge_weights(params)
    h = x
    for l in range(depth):
        qm, km, vm, qf, kf, vf, u = _prep_call(h, ca, sb, w_prep, consts, l, tm)
        y_mla = _attn_call(qm, km, vm, mask_mla, True, tq, "attn_mla")
        y_fox = _attn_call(qf, kf, vf, mask_fox, False, tq, "attn_fox")
        y_s5 = _s5_call(u, w_s5, l, ts, min(S // ts, 4))
        h = _merge_call(h, y_mla, y_fox, y_s5, w_merge, l, tm)
    return h


def kernel(x, positions, norm_g, w_in, mla_q_a_norm, mla_w_q_up, mla_kv_a_norm, mla_w_kv_up, mla_q_norm, mla_k_norm, fox_b_f, fox_q_norm, fox_k_norm, s5_lambda_re, s5_lambda_im, s5_log_dt, s5_b_re, s5_b_im, s5_c_re, s5_c_im, s5_d, s5_w_glu, s5_b_glu, w_branch_out, w_out):
    params = dict(zip(_LAYER_KEYS, (norm_g, w_in, mla_q_a_norm, mla_w_q_up, mla_kv_a_norm, mla_w_kv_up, mla_q_norm,
                                    mla_k_norm, fox_b_f, fox_q_norm, fox_k_norm, s5_lambda_re, s5_lambda_im,
                                    s5_log_dt, s5_b_re, s5_b_im, s5_c_re, s5_c_im, s5_d, s5_w_glu, s5_b_glu,
                                    w_branch_out, w_out)))
    return _forward(x, positions, params)
```

```python
import functools
import math

import numpy as np
import jax
import jax.numpy as jnp
from jax import lax
from jax.experimental import pallas as pl
from jax.experimental.pallas import tpu as pltpu

F32 = jnp.float32
BF16 = jnp.bfloat16

D_MODEL = 1024
CHUNK = 64
EPS = 1e-6

MLA_HEADS = 8
MLA_NOPE = 64
MLA_ROPE = 32
MLA_V = 64
MLA_Q_RANK = 256
MLA_KV_RANK = 128
MLA_WIDTH = MLA_HEADS * MLA_V
MLA_QK_DIM = MLA_NOPE + MLA_ROPE
ROPE_THETA = 10000.0

FOX_HEADS = 8
FOX_HEAD_DIM = 64
FOX_WIDTH = FOX_HEADS * FOX_HEAD_DIM

S5_WIDTH = 512
S5_GROUP = 16
S5_GROUPS = S5_WIDTH // S5_GROUP
S5_STATE = 64
S5_STATES = S5_GROUPS * S5_STATE
S5_ROW_PAD = 8
S5_SCAN_LANES = 1024

BRANCH_WIDTH = MLA_WIDTH + FOX_WIDTH + S5_WIDTH

LANE = 128
HEAD_PAD = 128
PAIR = 2 * HEAD_PAD
N_PAIRS = 4
N_SPLIT = 3
LOG2E = math.log2(math.e)

_OFF = np.cumsum([0, MLA_Q_RANK, MLA_KV_RANK, MLA_ROPE, FOX_WIDTH, FOX_WIDTH, FOX_WIDTH, FOX_HEADS,
                  S5_WIDTH, MLA_WIDTH, FOX_WIDTH, S5_WIDTH, D_MODEL, D_MODEL, D_MODEL]).tolist()
(O_CQ, O_CKV, O_KPE, O_FQ, O_FK, O_FV, O_FF, O_S5U, O_GATE, _o1, _o2, O_MERGE, _o3, _o4, O_END) = _OFF

C_CQ = 0
C_CKV = C_CQ + MLA_Q_RANK
C_KPA = C_CKV + MLA_KV_RANK
C_KPB = C_KPA + LANE
C_FQ = C_KPB + LANE
C_FK = C_FQ + FOX_WIDTH
C_FV = C_FK + FOX_WIDTH
C_FF = C_FV + FOX_WIDTH
C_S5 = C_FF + LANE
C_END = C_S5 + S5_WIDTH

VMEM_LIMIT = 56 * 1024 * 1024


def _cparams(sem):
    return pltpu.CompilerParams(dimension_semantics=sem, vmem_limit_bytes=VMEM_LIMIT)


def _full(shape):
    n = len(shape)
    return pl.BlockSpec(shape, lambda *_: (0,) * n)


def _rms(x, g):
    return x * lax.rsqrt(jnp.mean(x * x, axis=-1, keepdims=True) + EPS) * g


def _dot(a, b):
    return jnp.dot(a, b, preferred_element_type=F32)


def _store_v_with_ones(v_ref, v):
    lane = lax.broadcasted_iota(jnp.int32, (v.shape[0], LANE), 1)
    for p in range(N_PAIRS):
        vp = v[:, p * LANE:(p + 1) * LANE]
        v_ref[0, :, p * PAIR:p * PAIR + LANE] = jnp.where(lane < LANE // 2, vp, 1.0).astype(v_ref.dtype)
        v_ref[0, :, p * PAIR + LANE:(p + 1) * PAIR] = jnp.where(lane < LANE // 2, 1.0, vp).astype(v_ref.dtype)


def _rope_kernel(pos_ref, inv_ref, sign_ref, ca_ref, sb_ref):
    ang = pos_ref[0].astype(F32) * inv_ref[...]
    ca_ref[0] = jnp.cos(ang)
    sb_ref[0] = jnp.sin(ang) * sign_ref[...]


def _rope_tables(positions):
    B, S = positions.shape
    ts = min(S, 512)
    inv = 1.0 / (ROPE_THETA ** (jnp.arange(0, MLA_ROPE, 2, dtype=F32) / MLA_ROPE))
    half = MLA_ROPE // 2
    inv_row = jnp.zeros((1, LANE), F32).at[0, MLA_NOPE:MLA_NOPE + half].set(inv)
    inv_row = inv_row.at[0, MLA_NOPE + half:MLA_NOPE + MLA_ROPE].set(inv)
    sign = np.zeros((1, LANE), np.float32)
    sign[0, MLA_NOPE:MLA_NOPE + half] = -1.0
    sign[0, MLA_NOPE + half:MLA_NOPE + MLA_ROPE] = 1.0
    blk = pl.BlockSpec((1, ts, LANE), lambda b, s: (b, s, 0))
    return pl.pallas_call(
        _rope_kernel,
        grid=(B, S // ts),
        in_specs=[pl.BlockSpec((1, ts, 1), lambda b, s: (b, s, 0)), _full((1, LANE)), _full((1, LANE))],
        out_specs=[blk, blk],
        out_shape=[jax.ShapeDtypeStruct((B, S, LANE), F32)] * 2,
        compiler_params=_cparams(("parallel", "parallel")),
        name="rope_tables",
    )(positions.reshape(B, S, 1), inv_row, jnp.asarray(sign))


def _split3(x):
    hi = x.astype(BF16)
    r1 = x - hi.astype(F32)
    mid = r1.astype(BF16)
    return hi, mid, (r1 - mid.astype(F32)).astype(BF16)


def _fox_qk_norm(x, bd, gain):
    sq = (x * x).astype(BF16)
    ss = jnp.concatenate([_dot(sq[:, c:c + PAIR], bd) for c in range(0, FOX_WIDTH, PAIR)], axis=1)
    return (x * lax.rsqrt(ss * (1.0 / FOX_HEAD_DIM) + EPS) * gain).astype(BF16)


def _prep_kernel(x_ref, ca_ref, sb_ref, ng_ref, wcat_ref, qan_ref, wqa_ref, wqb_ref, kvn_ref, wk_ref, wv_ref,
                 gqm_ref, gkm_ref, bf_ref, gqf_ref, gkf_ref, bd_ref, tri_ref,
                 eq_ref, ek_ref, oq_ref, ok_ref,
                 qm_ref, km_ref, vm_ref, qf_ref, kf_ref, vf_ref, u_ref, carry_ref):
    @pl.when(pl.program_id(1) == 0)
    def _():
        carry_ref[...] = jnp.zeros_like(carry_ref)

    x = x_ref[0]
    h = _rms(x, ng_ref[...]).astype(BF16)
    proj = _dot(h, wcat_ref[...])
    ca = ca_ref[0]
    sb = sb_ref[0]

    cqn = _rms(proj[:, C_CQ:C_CQ + MLA_Q_RANK], qan_ref[...]).astype(BF16)
    qa = _dot(cqn, wqa_ref[...])
    qb = _dot(cqn, wqb_ref[...])
    ckvn = _rms(proj[:, C_CKV:C_CKV + MLA_KV_RANK], kvn_ref[...]).astype(BF16)
    kc = _dot(ckvn, wk_ref[...])
    _store_v_with_ones(vm_ref, _dot(ckvn, wv_ref[...]))
    kpe = proj[:, C_KPA:C_KPA + LANE] * ca + proj[:, C_KPB:C_KPB + LANE] * sb
    inv_d = 1.0 / MLA_QK_DIM
    for hd in range(MLA_HEADS):
        sl = slice(hd * HEAD_PAD, (hd + 1) * HEAD_PAD)
        qh = qa[:, sl] * ca + qb[:, sl] * sb
        ss = jnp.sum(qh * qh, axis=-1, keepdims=True)
        qm_ref[0, :, sl] = (qh * lax.rsqrt(ss * inv_d + EPS) * gqm_ref[...]).astype(BF16)
        kh = kc[:, sl] + kpe
        ss = jnp.sum(kh * kh, axis=-1, keepdims=True)
        km_ref[0, :, sl] = (kh * lax.rsqrt(ss * inv_d + EPS) * gkm_ref[...]).astype(BF16)

    fqn = _fox_qk_norm(proj[:, C_FQ:C_FQ + FOX_WIDTH], bd_ref[...], gqf_ref[...])
    fkn = _fox_qk_norm(proj[:, C_FK:C_FK + FOX_WIDTH], bd_ref[...], gkf_ref[...])
    _store_v_with_ones(vf_ref, proj[:, C_FV:C_FV + FOX_WIDTH])

    z = proj[:, C_FF:C_FF + LANE] + bf_ref[...]
    log_f = jnp.minimum(z, 0.0) - jnp.log1p(jnp.exp(-jnp.abs(z)))
    sums = _dot(tri_ref[...], jnp.concatenate(_split3(log_f), axis=1))
    cum = sums[:, :LANE] + sums[:, LANE:2 * LANE] + sums[:, 2 * LANE:] + carry_ref[...]
    carry_ref[...] = cum[cum.shape[0] - 1:, :]
    hi, mid, lo = _split3(cum * LOG2E)
    lane = lax.broadcasted_iota(jnp.int32, hi.shape, 1)
    pieces = jnp.where(lane < FOX_HEADS, hi, jnp.where(lane < 2 * FOX_HEADS, mid, lo))
    auxq = (_dot(pieces, eq_ref[...]) + oq_ref[...]).astype(BF16)
    auxk = (_dot(pieces, ek_ref[...]) + ok_ref[...]).astype(BF16)
    for p in range(N_PAIRS):
        src = slice(p * LANE, (p + 1) * LANE)
        qf_ref[0, :, p * PAIR:p * PAIR + LANE] = fqn[:, src]
        qf_ref[0, :, p * PAIR + LANE:(p + 1) * PAIR] = auxq[:, src]
        kf_ref[0, :, p * PAIR:p * PAIR + LANE] = fkn[:, src]
        kf_ref[0, :, p * PAIR + LANE:(p + 1) * PAIR] = auxk[:, src]

    u_ref[0] = proj[:, C_S5:C_S5 + S5_WIDTH]


def _layer(arr, l, single_buffer=False):
    n = arr.ndim - 1
    mode = dict(pipeline_mode=pl.Buffered(1)) if single_buffer else {}
    return pl.BlockSpec((None,) + arr.shape[1:], lambda *_: (l,) + (0,) * n, **mode)


def _pack_kernel(w_ref, wcat_ref, wgm_ref):
    w = w_ref[...]
    rows = w.shape[0]
    half = MLA_ROPE // 2
    zeros = lambda n: jnp.zeros((rows, n), F32)
    kpe_lo = w[:, O_KPE:O_KPE + half]
    kpe_hi = w[:, O_KPE + half:O_KPE + MLA_ROPE]
    tail = zeros(LANE - MLA_NOPE - MLA_ROPE)
    ff = w[:, O_FF:O_FF + FOX_HEADS]
    pieces = [w[:, O_CQ:O_CQ + MLA_Q_RANK + MLA_KV_RANK],
              zeros(MLA_NOPE), kpe_lo, kpe_hi, tail,
              zeros(MLA_NOPE), kpe_hi, kpe_lo, tail,
              w[:, O_FQ:O_FQ + 3 * FOX_WIDTH],
              ff, ff, ff, zeros(LANE - N_SPLIT * FOX_HEADS),
              w[:, O_S5U:O_S5U + S5_WIDTH]]
    wcat_ref[...] = jnp.concatenate(pieces, axis=1).astype(wcat_ref.dtype)
    wgm_ref[...] = w[:, O_GATE:O_END].astype(wgm_ref.dtype)


def _pack_w_in(w_in):
    L = w_in.shape[0]
    tr = 128
    blk = lambda width: pl.BlockSpec((None, tr, width), lambda l, r: (l, r, 0))
    return pl.pallas_call(
        _pack_kernel,
        grid=(L, D_MODEL // tr),
        in_specs=[blk(O_END)],
        out_specs=[blk(C_END), blk(O_END - O_GATE)],
        out_shape=[jax.ShapeDtypeStruct((L, D_MODEL, C_END), BF16),
                   jax.ShapeDtypeStruct((L, D_MODEL, O_END - O_GATE), BF16)],
        compiler_params=_cparams(("parallel", "parallel")),
        name="pack_w_in",
    )(w_in)


def _prep_weights(pr, wcat):
    L = wcat.shape[0]
    row = lambda a: a[:, None, :]
    half = MLA_ROPE // 2

    wq = pr["mla_w_q_up"].reshape(L, MLA_Q_RANK, MLA_HEADS, MLA_QK_DIM)
    nope, pe = wq[..., :MLA_NOPE], wq[..., MLA_NOPE:]
    pe_sw = jnp.concatenate([pe[..., half:], pe[..., :half]], axis=3)
    z32 = jnp.zeros((L, MLA_Q_RANK, MLA_HEADS, HEAD_PAD - MLA_QK_DIM), F32)
    wqa = jnp.concatenate([nope, pe, z32], axis=3).reshape(L, MLA_Q_RANK, MLA_HEADS * HEAD_PAD).astype(BF16)
    wqb = jnp.concatenate([jnp.zeros_like(nope), pe_sw, z32], axis=3).reshape(L, MLA_Q_RANK, -1).astype(BF16)

    wkv = pr["mla_w_kv_up"].reshape(L, MLA_KV_RANK, MLA_HEADS, MLA_NOPE + MLA_V)
    wk = jnp.concatenate([wkv[..., :MLA_NOPE], jnp.zeros((L, MLA_KV_RANK, MLA_HEADS, HEAD_PAD - MLA_NOPE), F32)],
                         axis=3).reshape(L, MLA_KV_RANK, -1).astype(BF16)
    wv = wkv[..., MLA_NOPE:].reshape(L, MLA_KV_RANK, MLA_WIDTH).astype(BF16)

    pad = jnp.zeros((L, HEAD_PAD - MLA_QK_DIM), F32)
    gqm = row(jnp.concatenate([pr["mla_q_norm"], pad], axis=1) * (LOG2E / math.sqrt(MLA_QK_DIM)))
    gkm = row(jnp.concatenate([pr["mla_k_norm"], pad], axis=1))
    bf = row(jnp.concatenate([pr["fox_b_f"]] * N_SPLIT + [jnp.zeros((L, LANE - N_SPLIT * FOX_HEADS), F32)], axis=1))
    gqf = row(jnp.tile(pr["fox_q_norm"], (1, FOX_HEADS)) * (LOG2E / math.sqrt(FOX_HEAD_DIM)))
    gkf = row(jnp.tile(pr["fox_k_norm"], (1, FOX_HEADS)))
    return dict(ng=row(pr["norm_g"]), wcat=wcat, qan=row(pr["mla_q_a_norm"]), wqa=wqa, wqb=wqb,
                kvn=row(pr["mla_kv_a_norm"]), wk=wk, wv=wv, gqm=gqm, gkm=gkm, bf=bf, gqf=gqf, gkf=gkf)


def _prep_consts(tm):
    bd = np.kron(np.eye(PAIR // FOX_HEAD_DIM, dtype=np.float32), np.ones((FOX_HEAD_DIM, FOX_HEAD_DIM), np.float32))
    tri = np.tril(np.ones((tm, tm), np.float32))
    eq = np.zeros((LANE, N_PAIRS * LANE), np.float32)
    ek = np.zeros_like(eq)
    oq = np.zeros((1, N_PAIRS * LANE), np.float32)
    ok = np.zeros_like(oq)
    for hd in range(FOX_HEADS):
        p, a = divmod(hd, 2)
        base = p * LANE + 2 * N_SPLIT * a
        for i in range(N_SPLIT):
            ek[i * FOX_HEADS + hd, base + i] = -1.0
            oq[0, base + i] = 1.0
            eq[i * FOX_HEADS + hd, base + N_SPLIT + i] = 1.0
            ok[0, base + N_SPLIT + i] = 1.0
    return dict(bd=jnp.asarray(bd, BF16), tri=jnp.asarray(tri, BF16),
                eq=jnp.asarray(eq, BF16), ek=jnp.asarray(ek, BF16), oq=jnp.asarray(oq), ok=jnp.asarray(ok))


def _prep_call(x, ca, sb, w, c, l, tm):
    B, S, _ = x.shape
    tok = lambda width: pl.BlockSpec((1, tm, width), lambda b, s: (b, s, 0))
    stacked = [w["ng"], w["wcat"], w["qan"], w["wqa"], w["wqb"], w["kvn"], w["wk"], w["wv"],
               w["gqm"], w["gkm"], w["bf"], w["gqf"], w["gkf"]]
    consts = [c["bd"], c["tri"], c["eq"], c["ek"], c["oq"], c["ok"]]
    args = [x, ca, sb] + stacked + consts
    in_specs = ([tok(D_MODEL), tok(LANE), tok(LANE)] + [_layer(a, l) for a in stacked]
                + [_full(a.shape) for a in consts])
    widths = [MLA_HEADS * HEAD_PAD, MLA_HEADS * HEAD_PAD, N_PAIRS * PAIR, N_PAIRS * PAIR, N_PAIRS * PAIR, N_PAIRS * PAIR]
    out_shape = [jax.ShapeDtypeStruct((B, S, wd), BF16) for wd in widths]
    out_shape.append(jax.ShapeDtypeStruct((B, S, S5_WIDTH), F32))
    return pl.pallas_call(
        _prep_kernel,
        grid=(B, S // tm),
        in_specs=in_specs,
        out_specs=[tok(wd) for wd in widths] + [tok(S5_WIDTH)],
        out_shape=out_shape,
        scratch_shapes=[pltpu.VMEM((1, LANE), F32)],
        compiler_params=_cparams(("parallel", "arbitrary")),
        name="prep",
    )(*args)


def _diag_tile(tq):
    return tq // 2 if tq >= 2 * PAIR else tq


def _attn_kernel(q_ref, k_ref, v_ref, hm_ref, o_ref, qm_ref, sa_ref, sb_ref, bias_ref, m_ref, acc_ref, *,
                 tq, chunk_causal):
    S = q_ref.shape[1]
    nq = S // tq
    td = _diag_tile(tq)
    steps = [(qi, j) for qi in range(nq) for j in range(qi + 1)]
    row = lax.broadcasted_iota(jnp.int32, (td, td), 0)
    col = lax.broadcasted_iota(jnp.int32, (td, td), 1)
    shift = int(math.log2(CHUNK)) if chunk_causal else 0
    bias_ref[...] = jnp.where((col >> shift) <= (row >> shift), 0.0, -jnp.inf)
    for a in range(2):
        qm_ref[a] = q_ref[0] * hm_ref[a:a + 1, :]

    def row_blocks(qi, j):
        if j < qi:
            return [(0, tq, tq)]
        return [(r * td, td, (r + 1) * td) for r in range(tq // td)]

    def qk(qi, j, s_ref):
        for r0, nr, nk in row_blocks(qi, j):
            k = k_ref[0, j * tq:j * tq + nk, :]
            for a in range(2):
                q = qm_ref[a, qi * tq + r0:qi * tq + r0 + nr, :]
                s_ref[a, r0:r0 + nr, :nk] = lax.dot_general(q, k, (((1,), (1,)), ((), ())),
                                                            preferred_element_type=F32)

    qk(0, 0, sa_ref)
    bufs = (sa_ref, sb_ref)
    for t, (qi, j) in enumerate(steps):
        s_cur, s_next = bufs[t % 2], bufs[(t + 1) % 2]
        if t + 1 < len(steps):
            qk(*steps[t + 1], s_next)
        for r0, nr, nk in row_blocks(qi, j):
            rows = slice(r0, r0 + nr)
            lane = lax.broadcasted_iota(jnp.int32, (nr, LANE), 1)
            outs = []
            for a in range(2):
                s = s_cur[a, rows, :nk]
                if j == qi:
                    tail = s[:, nk - td:] + bias_ref[...]
                    s = tail if nk == td else jnp.concatenate([s[:, :nk - td], tail], axis=1)
                v = v_ref[0, j * tq:j * tq + nk, a * LANE:(a + 1) * LANE]
                m_cur = jnp.max(s, axis=-1, keepdims=True)
                if j == 0:
                    m_new = jnp.broadcast_to(m_cur, (nr, LANE))
                    p = jnp.exp2(s - m_cur)
                    acc = _dot(p.astype(BF16), v)
                else:
                    m_prev = m_ref[a, rows]
                    m_new = jnp.maximum(m_prev, m_cur)
                    alpha = jnp.exp2(m_prev - m_new)
                    p = jnp.exp2(s - jnp.concatenate([m_new] * (nk // LANE), axis=1))
                    acc = alpha * acc_ref[a, rows] + _dot(p.astype(BF16), v)
                if j == qi:
                    outs.append(acc / pltpu.roll(acc, LANE // 2, axis=1))
                else:
                    acc_ref[a, rows] = acc
                    m_ref[a, rows] = m_new
            if j == qi:
                out = jnp.where(lane < LANE // 2, outs[0], outs[1]).astype(o_ref.dtype)
                o_ref[0, qi * tq + r0:qi * tq + r0 + nr, :] = out


def _attn_call(q, k, v, head_mask, chunk_causal, tq, name):
    B, S, _ = q.shape
    kern = functools.partial(_attn_kernel, tq=tq, chunk_causal=chunk_causal)
    td = _diag_tile(tq)
    return pl.pallas_call(
        kern,
        grid=(B, N_PAIRS),
        in_specs=[pl.BlockSpec((1, S, PAIR), lambda b, p: (b, 0, p)),
                  pl.BlockSpec((1, S, PAIR), lambda b, p: (b, 0, p)),
                  pl.BlockSpec((1, S, PAIR), lambda b, p: (b, 0, p)),
                  _full((2, PAIR))],
        out_specs=pl.BlockSpec((1, S, LANE), lambda b, p: (b, 0, p)),
        out_shape=jax.ShapeDtypeStruct((B, S, N_PAIRS * LANE), BF16),
        scratch_shapes=[pltpu.VMEM((2, S, PAIR), BF16), pltpu.VMEM((2, tq, tq), F32), pltpu.VMEM((2, tq, tq), F32),
                        pltpu.VMEM((td, td), F32), pltpu.VMEM((2, tq, LANE), F32), pltpu.VMEM((2, tq, LANE), F32)],
        compiler_params=_cparams(("parallel", "parallel")),
        name=name,
    )(q, k, v, head_mask)


def _head_masks():
    mla = np.zeros((2, PAIR), np.float32)
    fox = np.zeros((2, PAIR), np.float32)
    for a in range(2):
        mla[a, a * HEAD_PAD:(a + 1) * HEAD_PAD] = 1.0
        fox[a, a * FOX_HEAD_DIM:(a + 1) * FOX_HEAD_DIM] = 1.0
        fox[a, LANE + 2 * N_SPLIT * a:LANE + 2 * N_SPLIT * (a + 1)] = 1.0
    return jnp.asarray(mla, BF16), jnp.asarray(fox, BF16)


def _s5_param_kernel(lr_ref, li_ref, ldt_ref, br_ref, bi_ref, are_ref, aim_ref, bbr_ref, bbi_ref):
    lr = lr_ref[...]
    li = li_ref[...]
    dt = jnp.exp(ldt_ref[...])
    mag = jnp.exp(lr * dt)
    a_re = mag * jnp.cos(li * dt)
    a_im = mag * jnp.sin(li * dt)
    den = lr * lr + li * li
    f_re = ((a_re - 1.0) * lr + a_im * li) / den
    f_im = (a_im * lr - (a_re - 1.0) * li) / den
    br = br_ref[...]
    bi = bi_ref[...]
    are_ref[...] = a_re
    aim_ref[...] = a_im
    bbr_ref[...] = f_re * br - f_im * bi
    bbi_ref[...] = f_re * bi + f_im * br


def _s5_params(pr, n_batch):
    G, N, C = S5_GROUPS, S5_STATE, S5_GROUP
    L = pr["s5_lambda_re"].shape[0]
    rep = lambda a: jnp.repeat(a.reshape(L * G, N), C, axis=0)
    ldt = jnp.broadcast_to(pr["s5_log_dt"][:, :, None], (L, G, N))
    tr = lambda b: jnp.transpose(b, (0, 1, 3, 2)).reshape(L * G * C, N)
    shp = jax.ShapeDtypeStruct((L * G * C, N), F32)
    a_re, a_im, bb_re, bb_im = pl.pallas_call(
        _s5_param_kernel, out_shape=[shp] * 4, name="s5_params",
    )(rep(pr["s5_lambda_re"]), rep(pr["s5_lambda_im"]), rep(ldt), tr(pr["s5_b_re"]), tr(pr["s5_b_im"]))
    in_mask = jnp.asarray(np.arange(G * C)[:, None] // C == np.arange(G * N)[None, :] // N)
    out_mask = jnp.asarray(np.arange(G * N)[:, None] // N == np.arange(G * C)[None, :] // C)

    def blockdiag_in(bb):
        return jnp.where(in_mask, jnp.tile(bb.reshape(L, G * C, N), (1, 1, G)), 0.0)

    def blockdiag_out(cc):
        rows = jnp.transpose(cc, (0, 1, 3, 2)).reshape(L, G * N, C)
        return jnp.where(out_mask, jnp.tile(rows, (1, 1, G)), 0.0).astype(BF16)

    w_b = jnp.concatenate([blockdiag_in(bb_re), blockdiag_in(bb_im)], axis=2).astype(BF16)
    first = lambda a: a.reshape(L, G, C, N)[:, :, 0, :].reshape(L, 1, G * N)
    a_rep = jnp.broadcast_to(jnp.concatenate([first(a_re), first(a_im)], axis=2), (L, n_batch, 2 * G * N))
    return dict(a=a_rep, w_b=w_b, w_cre=blockdiag_out(pr["s5_c_re"]), w_cim=blockdiag_out(pr["s5_c_im"]),
                d=pr["s5_d"][:, None, :], w_glu=pr["s5_w_glu"].astype(BF16), b_glu=pr["s5_b_glu"][:, None, :])


def _s5_kernel(u_ref, a_ref, wb_ref, wcr_ref, wci_ref, d_ref, wg_ref, bg_ref, o_ref, st_ref, us_ref, zs_ref, bu_ref,
               xs_ref, *, nb, ts, n_sub):
    @pl.when(pl.program_id(0) == 0)
    def _():
        st_ref[...] = jnp.zeros_like(st_ref)

    N = S5_STATES
    pitch = ts + S5_ROW_PAD
    nk = S5_WIDTH // LANE

    for c in range(n_sub):
        for b in range(nb):
            for k in range(nk):
                us_ref[k, b * pitch:b * pitch + ts, :] = u_ref[b, c * ts:(c + 1) * ts, k * LANE:(k + 1) * LANE]
        u = jnp.concatenate(
            [jnp.concatenate([us_ref[k, pl.ds(t, nb, stride=pitch), :] for k in range(nk)], axis=1) for t in range(ts)],
            axis=0)
        ub = u.astype(BF16)
        for n in range(2 * N // PAIR):
            c0 = (n % (N // PAIR)) // 2 * LANE
            bu_ref[:, n * PAIR:(n + 1) * PAIR] = _dot(ub[:, c0:c0 + LANE], wb_ref[c0:c0 + LANE, n * PAIR:(n + 1) * PAIR])

        for g in range(N // S5_SCAN_LANES):
            re = slice(g * S5_SCAN_LANES, (g + 1) * S5_SCAN_LANES)
            im = slice(N + g * S5_SCAN_LANES, N + (g + 1) * S5_SCAN_LANES)
            ar = a_ref[:, re]
            ai = a_ref[:, im]

            def step(t, carry):
                xr, xi = carry
                r = pl.multiple_of(t * nb, nb)
                nr = ar * xr - ai * xi + bu_ref[pl.ds(r, nb), re]
                ni = ar * xi + ai * xr + bu_ref[pl.ds(r, nb), im]
                xs_ref[pl.ds(r, nb), re] = nr
                xs_ref[pl.ds(r, nb), im] = ni
                return nr, ni

            xr, xi = lax.fori_loop(0, ts, step, (st_ref[:, re], st_ref[:, im]), unroll=4)
            st_ref[:, re] = xr
            st_ref[:, im] = xi

        ys = []
        for t in range(S5_WIDTH // PAIR):
            r0 = t * (N // 2)
            xr = xs_ref[:, r0:r0 + N // 2].astype(BF16)
            xi = xs_ref[:, N + r0:N + r0 + N // 2].astype(BF16)
            cols = slice(t * PAIR, (t + 1) * PAIR)
            ys.append(_dot(xr, wcr_ref[r0:r0 + N // 2, cols]) - _dot(xi, wci_ref[r0:r0 + N // 2, cols]))
        y = jnp.concatenate(ys, axis=1) + d_ref[...] * u
        z = jax.nn.gelu(y, approximate=True)
        z = z * jax.nn.sigmoid(_dot(z.astype(BF16), wg_ref[...]) + bg_ref[...])
        for t in range(ts):
            for k in range(nk):
                zs_ref[k, pl.ds(t, nb, stride=pitch), :] = z[t * nb:(t + 1) * nb, k * LANE:(k + 1) * LANE]
        for b in range(nb):
            rows = jnp.concatenate([zs_ref[k, b * pitch:b * pitch + ts, :] for k in range(nk)], axis=1)
            o_ref[b, c * ts:(c + 1) * ts, :] = rows.astype(o_ref.dtype)


def _s5_call(u, p, l, ts, n_sub):
    B, S, _ = u.shape
    blk = ts * n_sub
    args = [u, p["a"], p["w_b"], p["w_cre"], p["w_cim"], p["d"], p["w_glu"], p["b_glu"]]
    kern = functools.partial(_s5_kernel, nb=B, ts=ts, n_sub=n_sub)
    once = lambda a: pl.BlockSpec((None,) + a.shape[1:], lambda i: (l,) + (0,) * (a.ndim - 1),
                                  pipeline_mode=pl.Buffered(1))
    slab_buf = pltpu.VMEM((S5_WIDTH // LANE, B * (ts + S5_ROW_PAD), LANE), F32)
    state_buf = pltpu.VMEM((ts * B, 2 * S5_STATES), F32)
    return pl.pallas_call(
        kern,
        grid=(S // blk,),
        in_specs=[pl.BlockSpec((B, blk, S5_WIDTH), lambda i: (0, i, 0))] + [once(a) for a in args[1:]],
        out_specs=pl.BlockSpec((B, blk, S5_WIDTH), lambda i: (0, i, 0)),
        out_shape=jax.ShapeDtypeStruct((B, S, S5_WIDTH), BF16),
        scratch_shapes=[pltpu.VMEM((B, 2 * S5_STATES), F32), slab_buf, slab_buf, state_buf, state_buf],
        compiler_params=_cparams(("arbitrary",)),
        name="s5",
    )(*args)


def _merge_kernel(x_ref, ym_ref, yf_ref, ys_ref, ng_ref, wgm_ref, wo_ref, wout_ref, o_ref):
    x = x_ref[0]
    h = _rms(x, ng_ref[...]).astype(BF16)
    merged = None
    for b, y_ref in enumerate((ym_ref, yf_ref, ys_ref)):
        g = _dot(h, wgm_ref[:, b * MLA_WIDTH:(b + 1) * MLA_WIDTH])
        gated = (y_ref[0].astype(F32) * (g * jax.nn.sigmoid(g))).astype(BF16)
        o = _dot(gated, wo_ref[b * MLA_WIDTH:(b + 1) * MLA_WIDTH, :])
        m = _dot(h, wgm_ref[:, BRANCH_WIDTH + b * D_MODEL:BRANCH_WIDTH + (b + 1) * D_MODEL])
        term = jax.nn.sigmoid(m) * o
        merged = term if merged is None else merged + term
    o_ref[0] = x + _dot(merged.astype(BF16), wout_ref[...])


def _merge_weights(pr, wgm):
    return [pr["norm_g"][:, None, :], wgm, pr["w_branch_out"].astype(BF16), pr["w_out"].astype(BF16)]


def _merge_call(x, y_mla, y_fox, y_s5, w, l, tm):
    B, S, _ = x.shape
    tok = lambda width: pl.BlockSpec((1, tm, width), lambda b, s: (b, s, 0))
    args = [x, y_mla, y_fox, y_s5] + w
    return pl.pallas_call(
        _merge_kernel,
        grid=(B, S // tm),
        in_specs=([tok(D_MODEL), tok(MLA_WIDTH), tok(FOX_WIDTH), tok(S5_WIDTH)]
                  + [_layer(a, l, single_buffer=True) for a in w]),
        out_specs=tok(D_MODEL),
        out_shape=jax.ShapeDtypeStruct(x.shape, x.dtype),
        compiler_params=_cparams(("parallel", "parallel")),
        name="merge",
    )(*args)


_LAYER_KEYS = ("norm_g", "w_in", "mla_q_a_norm", "mla_w_q_up", "mla_kv_a_norm", "mla_w_kv_up", "mla_q_norm",
               "mla_k_norm", "fox_b_f", "fox_q_norm", "fox_k_norm", "s5_lambda_re", "s5_lambda_im", "s5_log_dt",
               "s5_b_re", "s5_b_im", "s5_c_re", "s5_c_im", "s5_d", "s5_w_glu", "s5_b_glu", "w_branch_out", "w_out")


def _forward(x, positions, params):
    B, S, _ = x.shape
    tm = min(S, 512)
    tq = min(S, 512)
    ts = min(S, 64)
    depth = params["w_in"].shape[0]
    ca, sb = _rope_tables(positions)
    consts = _prep_consts(tm)
    mask_mla, mask_fox = _head_masks()
    wcat, wgm = _pack_w_in(params["w_in"])
    w_prep = _prep_weights(params, wcat)
    w_s5 = _s5_params(params, B)
    w_merge = _merge_weights(params, wgm)
    h = x
    for l in range(depth):
        qm, km, vm, qf, kf, vf, u = _prep_call(h, ca, sb, w_prep, consts, l, tm)
        y_mla = _attn_call(qm, km, vm, mask_mla, True, tq, "attn_mla")
        y_fox = _attn_call(qf, kf, vf, mask_fox, False, tq, "attn_fox")
        y_s5 = _s5_call(u, w_s5, l, ts, min(S // ts, 4))
        h = _merge_call(h, y_mla, y_fox, y_s5, w_merge, l, min(S, 2 * tm))
    return h


def kernel(x, positions, norm_g, w_in, mla_q_a_norm, mla_w_q_up, mla_kv_a_norm, mla_w_kv_up, mla_q_norm, mla_k_norm, fox_b_f, fox_q_norm, fox_k_norm, s5_lambda_re, s5_lambda_im, s5_log_dt, s5_b_re, s5_b_im, s5_c_re, s5_c_im, s5_d, s5_w_glu, s5_b_glu, w_branch_out, w_out):
    params = dict(zip(_LAYER_KEYS, (norm_g, w_in, mla_q_a_norm, mla_w_q_up, mla_kv_a_norm, mla_w_kv_up, mla_q_norm,
                                    mla_k_norm, fox_b_f, fox_q_norm, fox_k_norm, s5_lambda_re, s5_lambda_im,
                                    s5_log_dt, s5_b_re, s5_b_im, s5_c_re, s5_c_im, s5_d, s5_w_glu, s5_b_glu,
                                    w_branch_out, w_out)))
    return _forward(x, positions, params)
```

```python
import functools
import math

import numpy as np
import jax
import jax.numpy as jnp
from jax import lax
from jax.experimental import pallas as pl
from jax.experimental.pallas import tpu as pltpu

F32 = jnp.float32
BF16 = jnp.bfloat16

D_MODEL = 1024
CHUNK = 64
EPS = 1e-6

MLA_HEADS = 8
MLA_NOPE = 64
MLA_ROPE = 32
MLA_V = 64
MLA_Q_RANK = 256
MLA_KV_RANK = 128
MLA_WIDTH = MLA_HEADS * MLA_V
MLA_QK_DIM = MLA_NOPE + MLA_ROPE
ROPE_THETA = 10000.0

FOX_HEADS = 8
FOX_HEAD_DIM = 64
FOX_WIDTH = FOX_HEADS * FOX_HEAD_DIM

S5_WIDTH = 512
S5_GROUP = 16
S5_GROUPS = S5_WIDTH // S5_GROUP
S5_STATE = 64
S5_STATES = S5_GROUPS * S5_STATE
S5_ROW_PAD = 8
S5_SCAN_GROUPS = 2
S5_GROUP_TILES = 4
S5_SCAN_UNROLL = 64

BRANCH_WIDTH = MLA_WIDTH + FOX_WIDTH + S5_WIDTH

LANE = 128
HEAD_PAD = 128
PAIR = 2 * HEAD_PAD
N_PAIRS = 4
N_SPLIT = 3
LOG2E = math.log2(math.e)

_OFF = np.cumsum([0, MLA_Q_RANK, MLA_KV_RANK, MLA_ROPE, FOX_WIDTH, FOX_WIDTH, FOX_WIDTH, FOX_HEADS,
                  S5_WIDTH, MLA_WIDTH, FOX_WIDTH, S5_WIDTH, D_MODEL, D_MODEL, D_MODEL]).tolist()
(O_CQ, O_CKV, O_KPE, O_FQ, O_FK, O_FV, O_FF, O_S5U, O_GATE, _o1, _o2, O_MERGE, _o3, _o4, O_END) = _OFF

C_CQ = 0
C_CKV = C_CQ + MLA_Q_RANK
C_KPA = C_CKV + MLA_KV_RANK
C_KPB = C_KPA + LANE
C_FQ = C_KPB + LANE
C_FK = C_FQ + FOX_WIDTH
C_FV = C_FK + FOX_WIDTH
C_FF = C_FV + FOX_WIDTH
C_S5 = C_FF + LANE
C_END = C_S5 + S5_WIDTH

VMEM_LIMIT = 56 * 1024 * 1024


def _cparams(sem):
    return pltpu.CompilerParams(dimension_semantics=sem, vmem_limit_bytes=VMEM_LIMIT)


def _full(shape):
    n = len(shape)
    return pl.BlockSpec(shape, lambda *_: (0,) * n)


def _rms(x, g):
    return x * lax.rsqrt(jnp.mean(x * x, axis=-1, keepdims=True) + EPS) * g


def _dot(a, b):
    return jnp.dot(a, b, preferred_element_type=F32)


def _store_v_with_ones(v_ref, v):
    lane = lax.broadcasted_iota(jnp.int32, (v.shape[0], LANE), 1)
    for p in range(N_PAIRS):
        vp = v[:, p * LANE:(p + 1) * LANE]
        v_ref[0, :, p * PAIR:p * PAIR + LANE] = jnp.where(lane < LANE // 2, vp, 1.0).astype(v_ref.dtype)
        v_ref[0, :, p * PAIR + LANE:(p + 1) * PAIR] = jnp.where(lane < LANE // 2, 1.0, vp).astype(v_ref.dtype)


def _rope_kernel(pos_ref, inv_ref, sign_ref, ca_ref, sb_ref):
    ang = pos_ref[0].astype(F32) * inv_ref[...]
    ca_ref[0] = jnp.cos(ang)
    sb_ref[0] = jnp.sin(ang) * sign_ref[...]


def _rope_tables(positions):
    B, S = positions.shape
    ts = min(S, 512)
    inv = 1.0 / (ROPE_THETA ** (jnp.arange(0, MLA_ROPE, 2, dtype=F32) / MLA_ROPE))
    half = MLA_ROPE // 2
    inv_row = jnp.zeros((1, LANE), F32).at[0, MLA_NOPE:MLA_NOPE + half].set(inv)
    inv_row = inv_row.at[0, MLA_NOPE + half:MLA_NOPE + MLA_ROPE].set(inv)
    sign = np.zeros((1, LANE), np.float32)
    sign[0, MLA_NOPE:MLA_NOPE + half] = -1.0
    sign[0, MLA_NOPE + half:MLA_NOPE + MLA_ROPE] = 1.0
    blk = pl.BlockSpec((1, ts, LANE), lambda b, s: (b, s, 0))
    return pl.pallas_call(
        _rope_kernel,
        grid=(B, S // ts),
        in_specs=[pl.BlockSpec((1, ts, 1), lambda b, s: (b, s, 0)), _full((1, LANE)), _full((1, LANE))],
        out_specs=[blk, blk],
        out_shape=[jax.ShapeDtypeStruct((B, S, LANE), F32)] * 2,
        compiler_params=_cparams(("parallel", "parallel")),
        name="rope_tables",
    )(positions.reshape(B, S, 1), inv_row, jnp.asarray(sign))


def _split3(x):
    hi = x.astype(BF16)
    r1 = x - hi.astype(F32)
    mid = r1.astype(BF16)
    return hi, mid, (r1 - mid.astype(F32)).astype(BF16)


def _fox_qk_norm(x, bd, gain):
    sq = (x * x).astype(BF16)
    ss = jnp.concatenate([_dot(sq[:, c:c + PAIR], bd) for c in range(0, FOX_WIDTH, PAIR)], axis=1)
    return (x * lax.rsqrt(ss * (1.0 / FOX_HEAD_DIM) + EPS) * gain).astype(BF16)


def _prep_kernel(x_ref, ca_ref, sb_ref, ng_ref, wcat_ref, qan_ref, wqa_ref, wqb_ref, kvn_ref, wk_ref, wv_ref,
                 gqm_ref, gkm_ref, bf_ref, gqf_ref, gkf_ref, bd_ref, tri_ref,
                 eq_ref, ek_ref, oq_ref, ok_ref,
                 qm_ref, km_ref, vm_ref, qf_ref, kf_ref, vf_ref, u_ref, carry_ref):
    @pl.when(pl.program_id(1) == 0)
    def _():
        carry_ref[...] = jnp.zeros_like(carry_ref)

    x = x_ref[0]
    h = _rms(x, ng_ref[...]).astype(BF16)
    proj = _dot(h, wcat_ref[...])
    ca = ca_ref[0]
    sb = sb_ref[0]

    cqn = _rms(proj[:, C_CQ:C_CQ + MLA_Q_RANK], qan_ref[...]).astype(BF16)
    qa = _dot(cqn, wqa_ref[...])
    qb = _dot(cqn, wqb_ref[...])
    ckvn = _rms(proj[:, C_CKV:C_CKV + MLA_KV_RANK], kvn_ref[...]).astype(BF16)
    kc = _dot(ckvn, wk_ref[...])
    _store_v_with_ones(vm_ref, _dot(ckvn, wv_ref[...]))
    kpe = proj[:, C_KPA:C_KPA + LANE] * ca + proj[:, C_KPB:C_KPB + LANE] * sb
    inv_d = 1.0 / MLA_QK_DIM
    for hd in range(MLA_HEADS):
        sl = slice(hd * HEAD_PAD, (hd + 1) * HEAD_PAD)
        qh = qa[:, sl] * ca + qb[:, sl] * sb
        ss = jnp.sum(qh * qh, axis=-1, keepdims=True)
        qm_ref[0, :, sl] = (qh * lax.rsqrt(ss * inv_d + EPS) * gqm_ref[...]).astype(BF16)
        kh = kc[:, sl] + kpe
        ss = jnp.sum(kh * kh, axis=-1, keepdims=True)
        km_ref[0, :, sl] = (kh * lax.rsqrt(ss * inv_d + EPS) * gkm_ref[...]).astype(BF16)

    fqn = _fox_qk_norm(proj[:, C_FQ:C_FQ + FOX_WIDTH], bd_ref[...], gqf_ref[...])
    fkn = _fox_qk_norm(proj[:, C_FK:C_FK + FOX_WIDTH], bd_ref[...], gkf_ref[...])
    _store_v_with_ones(vf_ref, proj[:, C_FV:C_FV + FOX_WIDTH])

    z = proj[:, C_FF:C_FF + LANE] + bf_ref[...]
    log_f = jnp.minimum(z, 0.0) - jnp.log1p(jnp.exp(-jnp.abs(z)))
    sums = _dot(tri_ref[...], jnp.concatenate(_split3(log_f), axis=1))
    cum = sums[:, :LANE] + sums[:, LANE:2 * LANE] + sums[:, 2 * LANE:] + carry_ref[...]
    carry_ref[...] = cum[cum.shape[0] - 1:, :]
    hi, mid, lo = _split3(cum * LOG2E)
    lane = lax.broadcasted_iota(jnp.int32, hi.shape, 1)
    pieces = jnp.where(lane < FOX_HEADS, hi, jnp.where(lane < 2 * FOX_HEADS, mid, lo))
    auxq = (_dot(pieces, eq_ref[...]) + oq_ref[...]).astype(BF16)
    auxk = (_dot(pieces, ek_ref[...]) + ok_ref[...]).astype(BF16)
    for p in range(N_PAIRS):
        src = slice(p * LANE, (p + 1) * LANE)
        qf_ref[0, :, p * PAIR:p * PAIR + LANE] = fqn[:, src]
        qf_ref[0, :, p * PAIR + LANE:(p + 1) * PAIR] = auxq[:, src]
        kf_ref[0, :, p * PAIR:p * PAIR + LANE] = fkn[:, src]
        kf_ref[0, :, p * PAIR + LANE:(p + 1) * PAIR] = auxk[:, src]

    u_ref[0] = proj[:, C_S5:C_S5 + S5_WIDTH]


def _layer(arr, l, single_buffer=False):
    n = arr.ndim - 1
    mode = dict(pipeline_mode=pl.Buffered(1)) if single_buffer else {}
    return pl.BlockSpec((None,) + arr.shape[1:], lambda *_: (l,) + (0,) * n, **mode)


def _pack_kernel(w_ref, wcat_ref, wgm_ref):
    w = w_ref[...]
    rows = w.shape[0]
    half = MLA_ROPE // 2
    zeros = lambda n: jnp.zeros((rows, n), F32)
    kpe_lo = w[:, O_KPE:O_KPE + half]
    kpe_hi = w[:, O_KPE + half:O_KPE + MLA_ROPE]
    tail = zeros(LANE - MLA_NOPE - MLA_ROPE)
    ff = w[:, O_FF:O_FF + FOX_HEADS]
    pieces = [w[:, O_CQ:O_CQ + MLA_Q_RANK + MLA_KV_RANK],
              zeros(MLA_NOPE), kpe_lo, kpe_hi, tail,
              zeros(MLA_NOPE), kpe_hi, kpe_lo, tail,
              w[:, O_FQ:O_FQ + 3 * FOX_WIDTH],
              ff, ff, ff, zeros(LANE - N_SPLIT * FOX_HEADS),
              w[:, O_S5U:O_S5U + S5_WIDTH]]
    wcat_ref[...] = jnp.concatenate(pieces, axis=1).astype(wcat_ref.dtype)
    wgm_ref[...] = w[:, O_GATE:O_END].astype(wgm_ref.dtype)


def _pack_w_in(w_in):
    L = w_in.shape[0]
    tr = 128
    blk = lambda width: pl.BlockSpec((None, tr, width), lambda l, r: (l, r, 0))
    return pl.pallas_call(
        _pack_kernel,
        grid=(L, D_MODEL // tr),
        in_specs=[blk(O_END)],
        out_specs=[blk(C_END), blk(O_END - O_GATE)],
        out_shape=[jax.ShapeDtypeStruct((L, D_MODEL, C_END), BF16),
                   jax.ShapeDtypeStruct((L, D_MODEL, O_END - O_GATE), BF16)],
        compiler_params=_cparams(("parallel", "parallel")),
        name="pack_w_in",
    )(w_in)


def _prep_weights(pr, wcat):
    L = wcat.shape[0]
    row = lambda a: a[:, None, :]
    half = MLA_ROPE // 2

    wq = pr["mla_w_q_up"].reshape(L, MLA_Q_RANK, MLA_HEADS, MLA_QK_DIM)
    nope, pe = wq[..., :MLA_NOPE], wq[..., MLA_NOPE:]
    pe_sw = jnp.concatenate([pe[..., half:], pe[..., :half]], axis=3)
    z32 = jnp.zeros((L, MLA_Q_RANK, MLA_HEADS, HEAD_PAD - MLA_QK_DIM), F32)
    wqa = jnp.concatenate([nope, pe, z32], axis=3).reshape(L, MLA_Q_RANK, MLA_HEADS * HEAD_PAD).astype(BF16)
    wqb = jnp.concatenate([jnp.zeros_like(nope), pe_sw, z32], axis=3).reshape(L, MLA_Q_RANK, -1).astype(BF16)

    wkv = pr["mla_w_kv_up"].reshape(L, MLA_KV_RANK, MLA_HEADS, MLA_NOPE + MLA_V)
    wk = jnp.concatenate([wkv[..., :MLA_NOPE], jnp.zeros((L, MLA_KV_RANK, MLA_HEADS, HEAD_PAD - MLA_NOPE), F32)],
                         axis=3).reshape(L, MLA_KV_RANK, -1).astype(BF16)
    wv = wkv[..., MLA_NOPE:].reshape(L, MLA_KV_RANK, MLA_WIDTH).astype(BF16)

    pad = jnp.zeros((L, HEAD_PAD - MLA_QK_DIM), F32)
    gqm = row(jnp.concatenate([pr["mla_q_norm"], pad], axis=1) * (LOG2E / math.sqrt(MLA_QK_DIM)))
    gkm = row(jnp.concatenate([pr["mla_k_norm"], pad], axis=1))
    bf = row(jnp.concatenate([pr["fox_b_f"]] * N_SPLIT + [jnp.zeros((L, LANE - N_SPLIT * FOX_HEADS), F32)], axis=1))
    gqf = row(jnp.tile(pr["fox_q_norm"], (1, FOX_HEADS)) * (LOG2E / math.sqrt(FOX_HEAD_DIM)))
    gkf = row(jnp.tile(pr["fox_k_norm"], (1, FOX_HEADS)))
    return dict(ng=row(pr["norm_g"]), wcat=wcat, qan=row(pr["mla_q_a_norm"]), wqa=wqa, wqb=wqb,
                kvn=row(pr["mla_kv_a_norm"]), wk=wk, wv=wv, gqm=gqm, gkm=gkm, bf=bf, gqf=gqf, gkf=gkf)


def _prep_consts(tm):
    bd = np.kron(np.eye(PAIR // FOX_HEAD_DIM, dtype=np.float32), np.ones((FOX_HEAD_DIM, FOX_HEAD_DIM), np.float32))
    tri = np.tril(np.ones((tm, tm), np.float32))
    eq = np.zeros((LANE, N_PAIRS * LANE), np.float32)
    ek = np.zeros_like(eq)
    oq = np.zeros((1, N_PAIRS * LANE), np.float32)
    ok = np.zeros_like(oq)
    for hd in range(FOX_HEADS):
        p, a = divmod(hd, 2)
        base = p * LANE + 2 * N_SPLIT * a
        for i in range(N_SPLIT):
            ek[i * FOX_HEADS + hd, base + i] = -1.0
            oq[0, base + i] = 1.0
            eq[i * FOX_HEADS + hd, base + N_SPLIT + i] = 1.0
            ok[0, base + N_SPLIT + i] = 1.0
    return dict(bd=jnp.asarray(bd, BF16), tri=jnp.asarray(tri, BF16),
                eq=jnp.asarray(eq, BF16), ek=jnp.asarray(ek, BF16), oq=jnp.asarray(oq), ok=jnp.asarray(ok))


def _prep_call(x, ca, sb, w, c, l, tm):
    B, S, _ = x.shape
    tok = lambda width: pl.BlockSpec((1, tm, width), lambda b, s: (b, s, 0))
    stacked = [w["ng"], w["wcat"], w["qan"], w["wqa"], w["wqb"], w["kvn"], w["wk"], w["wv"],
               w["gqm"], w["gkm"], w["bf"], w["gqf"], w["gkf"]]
    consts = [c["bd"], c["tri"], c["eq"], c["ek"], c["oq"], c["ok"]]
    args = [x, ca, sb] + stacked + consts
    in_specs = ([tok(D_MODEL), tok(LANE), tok(LANE)] + [_layer(a, l) for a in stacked]
                + [_full(a.shape) for a in consts])
    widths = [MLA_HEADS * HEAD_PAD, MLA_HEADS * HEAD_PAD, N_PAIRS * PAIR, N_PAIRS * PAIR, N_PAIRS * PAIR, N_PAIRS * PAIR]
    out_shape = [jax.ShapeDtypeStruct((B, S, wd), BF16) for wd in widths]
    out_shape.append(jax.ShapeDtypeStruct((B, S, S5_WIDTH), F32))
    return pl.pallas_call(
        _prep_kernel,
        grid=(B, S // tm),
        in_specs=in_specs,
        out_specs=[tok(wd) for wd in widths] + [tok(S5_WIDTH)],
        out_shape=out_shape,
        scratch_shapes=[pltpu.VMEM((1, LANE), F32)],
        compiler_params=_cparams(("parallel", "arbitrary")),
        name="prep",
    )(*args)


def _diag_tile(tq):
    return tq // 2 if tq >= 2 * PAIR else tq


def _attn_kernel(q_ref, k_ref, v_ref, hm_ref, o_ref, qm_ref, sa_ref, sb_ref, bias_ref, m_ref, acc_ref, *,
                 tq, chunk_causal):
    S = q_ref.shape[1]
    nq = S // tq
    td = _diag_tile(tq)
    steps = [(qi, j) for qi in range(nq) for j in range(qi + 1)]
    row = lax.broadcasted_iota(jnp.int32, (td, td), 0)
    col = lax.broadcasted_iota(jnp.int32, (td, td), 1)
    shift = int(math.log2(CHUNK)) if chunk_causal else 0
    bias_ref[...] = jnp.where((col >> shift) <= (row >> shift), 0.0, -jnp.inf)
    for a in range(2):
        qm_ref[a] = q_ref[0] * hm_ref[a:a + 1, :]

    def row_blocks(qi, j):
        if j < qi:
            return [(0, tq, tq)]
        return [(r * td, td, (r + 1) * td) for r in range(tq // td)]

    def qk(qi, j, s_ref):
        for r0, nr, nk in row_blocks(qi, j):
            k = k_ref[0, j * tq:j * tq + nk, :]
            for a in range(2):
                q = qm_ref[a, qi * tq + r0:qi * tq + r0 + nr, :]
                s_ref[a, r0:r0 + nr, :nk] = lax.dot_general(q, k, (((1,), (1,)), ((), ())),
                                                            preferred_element_type=F32)

    qk(0, 0, sa_ref)
    bufs = (sa_ref, sb_ref)
    for t, (qi, j) in enumerate(steps):
        s_cur, s_next = bufs[t % 2], bufs[(t + 1) % 2]
        if t + 1 < len(steps):
            qk(*steps[t + 1], s_next)
        for r0, nr, nk in row_blocks(qi, j):
            rows = slice(r0, r0 + nr)
            lane = lax.broadcasted_iota(jnp.int32, (nr, LANE), 1)
            outs = []
            for a in range(2):
                s = s_cur[a, rows, :nk]
                if j == qi:
                    tail = s[:, nk - td:] + bias_ref[...]
                    s = tail if nk == td else jnp.concatenate([s[:, :nk - td], tail], axis=1)
                v = v_ref[0, j * tq:j * tq + nk, a * LANE:(a + 1) * LANE]
                m_cur = jnp.max(s, axis=-1, keepdims=True)
                if j == 0:
                    m_new = jnp.broadcast_to(m_cur, (nr, LANE))
                    p = jnp.exp2(s - m_cur)
                    acc = _dot(p.astype(BF16), v)
                else:
                    m_prev = m_ref[a, rows]
                    m_new = jnp.maximum(m_prev, m_cur)
                    alpha = jnp.exp2(m_prev - m_new)
                    p = jnp.exp2(s - jnp.concatenate([m_new] * (nk // LANE), axis=1))
                    acc = alpha * acc_ref[a, rows] + _dot(p.astype(BF16), v)
                if j == qi:
                    outs.append(acc / pltpu.roll(acc, LANE // 2, axis=1))
                else:
                    acc_ref[a, rows] = acc
                    m_ref[a, rows] = m_new
            if j == qi:
                out = jnp.where(lane < LANE // 2, outs[0], outs[1]).astype(o_ref.dtype)
                o_ref[0, qi * tq + r0:qi * tq + r0 + nr, :] = out


def _attn_call(q, k, v, head_mask, chunk_causal, tq, name):
    B, S, _ = q.shape
    kern = functools.partial(_attn_kernel, tq=tq, chunk_causal=chunk_causal)
    td = _diag_tile(tq)
    return pl.pallas_call(
        kern,
        grid=(B, N_PAIRS),
        in_specs=[pl.BlockSpec((1, S, PAIR), lambda b, p: (b, 0, p)),
                  pl.BlockSpec((1, S, PAIR), lambda b, p: (b, 0, p)),
                  pl.BlockSpec((1, S, PAIR), lambda b, p: (b, 0, p)),
                  _full((2, PAIR))],
        out_specs=pl.BlockSpec((1, S, LANE), lambda b, p: (b, 0, p)),
        out_shape=jax.ShapeDtypeStruct((B, S, N_PAIRS * LANE), BF16),
        scratch_shapes=[pltpu.VMEM((2, S, PAIR), BF16), pltpu.VMEM((2, tq, tq), F32), pltpu.VMEM((2, tq, tq), F32),
                        pltpu.VMEM((td, td), F32), pltpu.VMEM((2, tq, LANE), F32), pltpu.VMEM((2, tq, LANE), F32)],
        compiler_params=_cparams(("parallel", "parallel")),
        name=name,
    )(q, k, v, head_mask)


def _head_masks():
    mla = np.zeros((2, PAIR), np.float32)
    fox = np.zeros((2, PAIR), np.float32)
    for a in range(2):
        mla[a, a * HEAD_PAD:(a + 1) * HEAD_PAD] = 1.0
        fox[a, a * FOX_HEAD_DIM:(a + 1) * FOX_HEAD_DIM] = 1.0
        fox[a, LANE + 2 * N_SPLIT * a:LANE + 2 * N_SPLIT * (a + 1)] = 1.0
    return jnp.asarray(mla, BF16), jnp.asarray(fox, BF16)


def _s5_param_kernel(lr_ref, li_ref, ldt_ref, br_ref, bi_ref, are_ref, aim_ref, bbr_ref, bbi_ref):
    lr = lr_ref[...]
    li = li_ref[...]
    dt = jnp.exp(ldt_ref[...])
    mag = jnp.exp(lr * dt)
    a_re = mag * jnp.cos(li * dt)
    a_im = mag * jnp.sin(li * dt)
    den = lr * lr + li * li
    f_re = ((a_re - 1.0) * lr + a_im * li) / den
    f_im = (a_im * lr - (a_re - 1.0) * li) / den
    br = br_ref[...]
    bi = bi_ref[...]
    are_ref[...] = a_re
    aim_ref[...] = a_im
    bbr_ref[...] = f_re * br - f_im * bi
    bbi_ref[...] = f_re * bi + f_im * br


def _s5_params(pr, n_batch):
    G, N, C = S5_GROUPS, S5_STATE, S5_GROUP
    L = pr["s5_lambda_re"].shape[0]
    rep = lambda a: jnp.repeat(a.reshape(L * G, N), C, axis=0)
    ldt = jnp.broadcast_to(pr["s5_log_dt"][:, :, None], (L, G, N))
    tr = lambda b: jnp.transpose(b, (0, 1, 3, 2)).reshape(L * G * C, N)
    shp = jax.ShapeDtypeStruct((L * G * C, N), F32)
    a_re, a_im, bb_re, bb_im = pl.pallas_call(
        _s5_param_kernel, out_shape=[shp] * 4, name="s5_params",
    )(rep(pr["s5_lambda_re"]), rep(pr["s5_lambda_im"]), rep(ldt), tr(pr["s5_b_re"]), tr(pr["s5_b_im"]))
    in_mask = jnp.asarray(np.arange(G * C)[:, None] // C == np.arange(G * N)[None, :] // N)
    out_mask = jnp.asarray(np.arange(G * N)[:, None] // N == np.arange(G * C)[None, :] // C)

    def blockdiag_in(bb):
        return jnp.where(in_mask, jnp.tile(bb.reshape(L, G * C, N), (1, 1, G)), 0.0)

    def blockdiag_out(cc):
        rows = jnp.transpose(cc, (0, 1, 3, 2)).reshape(L, G * N, C)
        return jnp.where(out_mask, jnp.tile(rows, (1, 1, G)), 0.0).astype(BF16)

    w_b = jnp.concatenate([blockdiag_in(bb_re), blockdiag_in(bb_im)], axis=2).astype(BF16)
    w_cre = blockdiag_out(pr["s5_c_re"])
    w_cim = -blockdiag_out(pr["s5_c_im"])
    wb_t, wc_t = [], []
    for g in range(S5_SCAN_GROUPS):
        for k in range(2 * S5_GROUP_TILES):
            n = S5_GROUP_TILES * g + k % S5_GROUP_TILES
            c0 = n // 2 * LANE
            col0 = (k // S5_GROUP_TILES) * G * N + n * PAIR
            wb_t.append(w_b[:, c0:c0 + LANE, col0:col0 + PAIR])
            wc = w_cre if k < S5_GROUP_TILES else w_cim
            wc_t.append(wc[:, n * PAIR:(n + 1) * PAIR, g * PAIR:(g + 1) * PAIR])
    first = lambda a: a.reshape(L, G, C, N)[:, :, 0, :].reshape(L, S5_SCAN_GROUPS, S5_GROUP_TILES, 1, PAIR)
    a_rep = jnp.broadcast_to(jnp.concatenate([first(a_re), first(a_im)], axis=2),
                             (L, S5_SCAN_GROUPS, 2 * S5_GROUP_TILES, n_batch, PAIR))
    return dict(a=a_rep, w_b=jnp.stack(wb_t, axis=1), w_c=jnp.stack(wc_t, axis=1),
                d=pr["s5_d"][:, None, :], w_glu=pr["s5_w_glu"].astype(BF16), b_glu=pr["s5_b_glu"][:, None, :])


def _s5_kernel(u_ref, a_ref, wb_ref, wc_ref, d_ref, wg_ref, bg_ref, o_ref, st_ref, us_ref, zs_ref, ub_ref, uf_ref,
               y_ref, bu0_ref, bu1_ref, x0_ref, x1_ref, *, nb, ts, n_sub):
    @pl.when(pl.program_id(0) == 0)
    def _():
        st_ref[...] = jnp.zeros_like(st_ref)

    pitch = ts + S5_ROW_PAD
    nk = S5_WIDTH // LANE
    gt = S5_GROUP_TILES
    n_iter = ts // S5_SCAN_UNROLL
    tiles_per_iter = 2 * gt // n_iter
    steps_per_tile = S5_SCAN_UNROLL // tiles_per_iter
    bu = (bu0_ref, bu1_ref)
    xs = (x0_ref, x1_ref)
    everything = slice(None)

    def stage_u(c):
        for b in range(nb):
            for k in range(nk):
                us_ref[k, b * pitch:b * pitch + ts, :] = u_ref[b, c * ts:(c + 1) * ts, k * LANE:(k + 1) * LANE]
        u = jnp.concatenate(
            [jnp.concatenate([us_ref[k, pl.ds(t, nb, stride=pitch), :] for k in range(nk)], axis=1) for t in range(ts)],
            axis=0)
        uf_ref[c % 2] = u
        for k in range(nk):
            ub_ref[k] = u[:, k * LANE:(k + 1) * LANE].astype(BF16)

    def b_item(g, k, rows):
        cb = (gt * g) // 2 + (k % gt) // 2
        bu[g][k, rows, :] = _dot(ub_ref[cb, rows, :], wb_ref[2 * gt * g + k])

    def c_item(g, k, rows):
        y_ref[g, rows, :] += _dot(xs[g][k, rows, :].astype(BF16), wc_ref[2 * gt * g + k])

    def scan_phase(g, b_target, c_source):
        ar = a_ref[g, :gt]
        ai = a_ref[g, gt:]
        if c_source is not None:
            y_ref[c_source] = jnp.zeros(y_ref.shape[1:], F32)

        def body(i, carry):
            xr, xi = carry
            for j in range(tiles_per_iter):
                k = i * tiles_per_iter + j
                if b_target is not None:
                    b_item(b_target, k, everything)
                if c_source is not None:
                    c_item(c_source, k, everything)
                for tt in range(steps_per_tile):
                    t = i * S5_SCAN_UNROLL + j * steps_per_tile + tt
                    r = pl.ds(t * nb, nb) if n_iter == 1 else pl.ds(pl.multiple_of(t * nb, nb), nb)
                    nr = ar * xr - ai * xi + bu[g][:gt, r, :]
                    ni = ar * xi + ai * xr + bu[g][gt:, r, :]
                    xs[g][:gt, r, :] = nr
                    xs[g][gt:, r, :] = ni
                    xr, xi = nr, ni
            return xr, xi

        carry = (st_ref[g, :gt], st_ref[g, gt:])
        xr, xi = body(0, carry) if n_iter == 1 else lax.fori_loop(0, n_iter, body, carry)
        st_ref[g, :gt] = xr
        st_ref[g, gt:] = xi

    def epilogue(c):
        y = jnp.concatenate([y_ref[g] for g in range(S5_SCAN_GROUPS)], axis=1) + d_ref[...] * uf_ref[c % 2]
        z = jax.nn.gelu(y, approximate=True)
        z = z * jax.nn.sigmoid(_dot(z.astype(BF16), wg_ref[...]) + bg_ref[...])
        for t in range(ts):
            for k in range(nk):
                zs_ref[k, pl.ds(t, nb, stride=pitch), :] = z[t * nb:(t + 1) * nb, k * LANE:(k + 1) * LANE]
        for b in range(nb):
            rows = jnp.concatenate([zs_ref[k, b * pitch:b * pitch + ts, :] for k in range(nk)], axis=1)
            o_ref[b, c * ts:(c + 1) * ts, :] = rows.astype(o_ref.dtype)

    stage_u(0)
    for k in range(2 * gt):
        b_item(0, k, everything)
    for c in range(n_sub):
        scan_phase(0, 1, 1 if c > 0 else None)
        if c > 0:
            epilogue(c - 1)
        if c + 1 < n_sub:
            stage_u(c + 1)
        scan_phase(1, 0 if c + 1 < n_sub else None, 0)
    y_ref[1] = jnp.zeros(y_ref.shape[1:], F32)
    for k in range(2 * gt):
        c_item(1, k, everything)
    epilogue(n_sub - 1)


def _s5_call(u, p, l, ts, n_sub):
    B, S, _ = u.shape
    blk = ts * n_sub
    rows = ts * B
    args = [u, p["a"], p["w_b"], p["w_c"], p["d"], p["w_glu"], p["b_glu"]]
    kern = functools.partial(_s5_kernel, nb=B, ts=ts, n_sub=n_sub)
    once = lambda a: pl.BlockSpec((None,) + a.shape[1:], lambda i: (l,) + (0,) * (a.ndim - 1),
                                  pipeline_mode=pl.Buffered(1))
    slab_buf = pltpu.VMEM((S5_WIDTH // LANE, B * (ts + S5_ROW_PAD), LANE), F32)
    tile_buf = pltpu.VMEM((2 * S5_GROUP_TILES, rows, PAIR), F32)
    return pl.pallas_call(
        kern,
        grid=(S // blk,),
        in_specs=[pl.BlockSpec((B, blk, S5_WIDTH), lambda i: (0, i, 0))] + [once(a) for a in args[1:]],
        out_specs=pl.BlockSpec((B, blk, S5_WIDTH), lambda i: (0, i, 0)),
        out_shape=jax.ShapeDtypeStruct((B, S, S5_WIDTH), BF16),
        scratch_shapes=[pltpu.VMEM((S5_SCAN_GROUPS, 2 * S5_GROUP_TILES, B, PAIR), F32), slab_buf, slab_buf,
                        pltpu.VMEM((S5_WIDTH // LANE, rows, LANE), BF16), pltpu.VMEM((2, rows, S5_WIDTH), F32),
                        pltpu.VMEM((S5_SCAN_GROUPS, rows, PAIR), F32), tile_buf, tile_buf, tile_buf, tile_buf],
        compiler_params=_cparams(("arbitrary",)),
        name="s5",
    )(*args)


def _merge_kernel(x_ref, ym_ref, yf_ref, ys_ref, ng_ref, wgm_ref, wo_ref, wout_ref, o_ref):
    x = x_ref[0]
    h = _rms(x, ng_ref[...]).astype(BF16)
    merged = None
    for b, y_ref in enumerate((ym_ref, yf_ref, ys_ref)):
        g = _dot(h, wgm_ref[:, b * MLA_WIDTH:(b + 1) * MLA_WIDTH])
        gated = (y_ref[0].astype(F32) * (g * jax.nn.sigmoid(g))).astype(BF16)
        o = _dot(gated, wo_ref[b * MLA_WIDTH:(b + 1) * MLA_WIDTH, :])
        m = _dot(h, wgm_ref[:, BRANCH_WIDTH + b * D_MODEL:BRANCH_WIDTH + (b + 1) * D_MODEL])
        term = jax.nn.sigmoid(m) * o
        merged = term if merged is None else merged + term
    o_ref[0] = x + _dot(merged.astype(BF16), wout_ref[...])


def _merge_weights(pr, wgm):
    return [pr["norm_g"][:, None, :], wgm, pr["w_branch_out"].astype(BF16), pr["w_out"].astype(BF16)]


def _merge_call(x, y_mla, y_fox, y_s5, w, l, tm):
    B, S, _ = x.shape
    tok = lambda width: pl.BlockSpec((1, tm, width), lambda b, s: (b, s, 0))
    args = [x, y_mla, y_fox, y_s5] + w
    return pl.pallas_call(
        _merge_kernel,
        grid=(B, S // tm),
        in_specs=([tok(D_MODEL), tok(MLA_WIDTH), tok(FOX_WIDTH), tok(S5_WIDTH)]
                  + [_layer(a, l, single_buffer=True) for a in w]),
        out_specs=tok(D_MODEL),
        out_shape=jax.ShapeDtypeStruct(x.shape, x.dtype),
        compiler_params=_cparams(("parallel", "parallel")),
        name="merge",
    )(*args)


_LAYER_KEYS = ("norm_g", "w_in", "mla_q_a_norm", "mla_w_q_up", "mla_kv_a_norm", "mla_w_kv_up", "mla_q_norm",
               "mla_k_norm", "fox_b_f", "fox_q_norm", "fox_k_norm", "s5_lambda_re", "s5_lambda_im", "s5_log_dt",
               "s5_b_re", "s5_b_im", "s5_c_re", "s5_c_im", "s5_d", "s5_w_glu", "s5_b_glu", "w_branch_out", "w_out")


def _forward(x, positions, params):
    B, S, _ = x.shape
    tm = min(S, 512)
    tq = min(S, 512)
    ts = min(S, 64)
    depth = params["w_in"].shape[0]
    ca, sb = _rope_tables(positions)
    consts = _prep_consts(tm)
    mask_mla, mask_fox = _head_masks()
    wcat, wgm = _pack_w_in(params["w_in"])
    w_prep = _prep_weights(params, wcat)
    w_s5 = _s5_params(params, B)
    w_merge = _merge_weights(params, wgm)
    h = x
    for l in range(depth):
        qm, km, vm, qf, kf, vf, u = _prep_call(h, ca, sb, w_prep, consts, l, tm)
        y_mla = _attn_call(qm, km, vm, mask_mla, True, tq, "attn_mla")
        y_fox = _attn_call(qf, kf, vf, mask_fox, False, tq, "attn_fox")
        y_s5 = _s5_call(u, w_s5, l, ts, min(S // ts, 4))
        h = _merge_call(h, y_mla, y_fox, y_s5, w_merge, l, min(S, 2 * tm))
    return h


def kernel(x, positions, norm_g, w_in, mla_q_a_norm, mla_w_q_up, mla_kv_a_norm, mla_w_kv_up, mla_q_norm, mla_k_norm, fox_b_f, fox_q_norm, fox_k_norm, s5_lambda_re, s5_lambda_im, s5_log_dt, s5_b_re, s5_b_im, s5_c_re, s5_c_im, s5_d, s5_w_glu, s5_b_glu, w_branch_out, w_out):
    params = dict(zip(_LAYER_KEYS, (norm_g, w_in, mla_q_a_norm, mla_w_q_up, mla_kv_a_norm, mla_w_kv_up, mla_q_norm,
                                    mla_k_norm, fox_b_f, fox_q_norm, fox_k_norm, s5_lambda_re, s5_lambda_im,
                                    s5_log_dt, s5_b_re, s5_b_im, s5_c_re, s5_c_im, s5_d, s5_w_glu, s5_b_glu,
                                    w_branch_out, w_out)))
    return _forward(x, positions, params)
```

```python
import functools
import math

import numpy as np
import jax
import jax.numpy as jnp
from jax import lax
from jax.experimental import pallas as pl
from jax.experimental.pallas import tpu as pltpu

F32 = jnp.float32
BF16 = jnp.bfloat16

D_MODEL = 1024
CHUNK = 64
EPS = 1e-6

MLA_HEADS = 8
MLA_NOPE = 64
MLA_ROPE = 32
MLA_V = 64
MLA_Q_RANK = 256
MLA_KV_RANK = 128
MLA_WIDTH = MLA_HEADS * MLA_V
MLA_QK_DIM = MLA_NOPE + MLA_ROPE
ROPE_THETA = 10000.0

FOX_HEADS = 8
FOX_HEAD_DIM = 64
FOX_WIDTH = FOX_HEADS * FOX_HEAD_DIM

S5_WIDTH = 512
S5_GROUP = 16
S5_GROUPS = S5_WIDTH // S5_GROUP
S5_STATE = 64
S5_STATES = S5_GROUPS * S5_STATE
S5_ROW_PAD = 8
S5_SCAN_GROUPS = 2
S5_GROUP_TILES = 4
S5_SCAN_UNROLL = 64

BRANCH_WIDTH = MLA_WIDTH + FOX_WIDTH + S5_WIDTH

LANE = 128
HEAD_PAD = 128
PAIR = 2 * HEAD_PAD
N_PAIRS = 4
N_SPLIT = 3
LOG2E = math.log2(math.e)

_OFF = np.cumsum([0, MLA_Q_RANK, MLA_KV_RANK, MLA_ROPE, FOX_WIDTH, FOX_WIDTH, FOX_WIDTH, FOX_HEADS,
                  S5_WIDTH, MLA_WIDTH, FOX_WIDTH, S5_WIDTH, D_MODEL, D_MODEL, D_MODEL]).tolist()
(O_CQ, O_CKV, O_KPE, O_FQ, O_FK, O_FV, O_FF, O_S5U, O_GATE, _o1, _o2, O_MERGE, _o3, _o4, O_END) = _OFF

C_CQ = 0
C_CKV = C_CQ + MLA_Q_RANK
C_KPA = C_CKV + MLA_KV_RANK
C_KPB = C_KPA + LANE
C_FQ = C_KPB + LANE
C_FK = C_FQ + FOX_WIDTH
C_FV = C_FK + FOX_WIDTH
C_FF = C_FV + FOX_WIDTH
C_S5 = C_FF + LANE
C_END = C_S5 + S5_WIDTH

VMEM_LIMIT = 56 * 1024 * 1024


def _cparams(sem):
    return pltpu.CompilerParams(dimension_semantics=sem, vmem_limit_bytes=VMEM_LIMIT)


def _full(shape):
    n = len(shape)
    return pl.BlockSpec(shape, lambda *_: (0,) * n)


def _rms(x, g):
    return x * lax.rsqrt(jnp.mean(x * x, axis=-1, keepdims=True) + EPS) * g


def _dot(a, b):
    return jnp.dot(a, b, preferred_element_type=F32)


def _dot_nt(a, b):
    return lax.dot_general(a, b, (((1,), (1,)), ((), ())), preferred_element_type=F32)


def _store_v_with_ones(v_ref, v):
    lane = lax.broadcasted_iota(jnp.int32, (v.shape[0], LANE), 1)
    for p in range(N_PAIRS):
        vp = v[:, p * LANE:(p + 1) * LANE]
        v_ref[0, :, p * PAIR:p * PAIR + LANE] = jnp.where(lane < LANE // 2, vp, 1.0).astype(v_ref.dtype)
        v_ref[0, :, p * PAIR + LANE:(p + 1) * PAIR] = jnp.where(lane < LANE // 2, 1.0, vp).astype(v_ref.dtype)


def _rope_kernel(pos_ref, inv_ref, sign_ref, ca_ref, sb_ref):
    ang = pos_ref[0].astype(F32) * inv_ref[...]
    ca_ref[0] = jnp.cos(ang)
    sb_ref[0] = jnp.sin(ang) * sign_ref[...]


def _rope_tables(positions):
    B, S = positions.shape
    ts = min(S, 512)
    inv = 1.0 / (ROPE_THETA ** (jnp.arange(0, MLA_ROPE, 2, dtype=F32) / MLA_ROPE))
    half = MLA_ROPE // 2
    inv_row = jnp.zeros((1, LANE), F32).at[0, MLA_NOPE:MLA_NOPE + half].set(inv)
    inv_row = inv_row.at[0, MLA_NOPE + half:MLA_NOPE + MLA_ROPE].set(inv)
    sign = np.zeros((1, LANE), np.float32)
    sign[0, MLA_NOPE:MLA_NOPE + half] = -1.0
    sign[0, MLA_NOPE + half:MLA_NOPE + MLA_ROPE] = 1.0
    blk = pl.BlockSpec((1, ts, LANE), lambda b, s: (b, s, 0))
    return pl.pallas_call(
        _rope_kernel,
        grid=(B, S // ts),
        in_specs=[pl.BlockSpec((1, ts, 1), lambda b, s: (b, s, 0)), _full((1, LANE)), _full((1, LANE))],
        out_specs=[blk, blk],
        out_shape=[jax.ShapeDtypeStruct((B, S, LANE), F32)] * 2,
        compiler_params=_cparams(("parallel", "parallel")),
        name="rope_tables",
    )(positions.reshape(B, S, 1), inv_row, jnp.asarray(sign))


def _split3(x):
    hi = x.astype(BF16)
    r1 = x - hi.astype(F32)
    mid = r1.astype(BF16)
    return hi, mid, (r1 - mid.astype(F32)).astype(BF16)


def _fox_qk_norm(x, bd, gain):
    sq = (x * x).astype(BF16)
    ss = jnp.concatenate([_dot(sq[:, c:c + PAIR], bd) for c in range(0, FOX_WIDTH, PAIR)], axis=1)
    return (x * lax.rsqrt(ss * (1.0 / FOX_HEAD_DIM) + EPS) * gain).astype(BF16)


def _prep_kernel(x_ref, ca_ref, sb_ref, ng_ref, wcat_ref, qan_ref, wqa_ref, wqb_ref, kvn_ref, wk_ref, wv_ref,
                 gqm_ref, gkm_ref, bf_ref, gqf_ref, gkf_ref, bd_ref, tri_ref,
                 eq_ref, ek_ref, oq_ref, ok_ref,
                 qm_ref, km_ref, vm_ref, qf_ref, kf_ref, vf_ref, u_ref, carry_ref):
    @pl.when(pl.program_id(1) == 0)
    def _():
        carry_ref[...] = jnp.zeros_like(carry_ref)

    x = x_ref[0]
    h = _rms(x, ng_ref[...]).astype(BF16)
    proj = _dot_nt(h, wcat_ref[...])
    ca = ca_ref[0]
    sb = sb_ref[0]

    cqn = _rms(proj[:, C_CQ:C_CQ + MLA_Q_RANK], qan_ref[...]).astype(BF16)
    qa = _dot(cqn, wqa_ref[...])
    qb = _dot(cqn, wqb_ref[...])
    ckvn = _rms(proj[:, C_CKV:C_CKV + MLA_KV_RANK], kvn_ref[...]).astype(BF16)
    kc = _dot(ckvn, wk_ref[...])
    _store_v_with_ones(vm_ref, _dot(ckvn, wv_ref[...]))
    kpe = proj[:, C_KPA:C_KPA + LANE] * ca + proj[:, C_KPB:C_KPB + LANE] * sb
    inv_d = 1.0 / MLA_QK_DIM
    for hd in range(MLA_HEADS):
        sl = slice(hd * HEAD_PAD, (hd + 1) * HEAD_PAD)
        qh = qa[:, sl] * ca + qb[:, sl] * sb
        ss = jnp.sum(qh * qh, axis=-1, keepdims=True)
        qm_ref[0, :, sl] = (qh * lax.rsqrt(ss * inv_d + EPS) * gqm_ref[...]).astype(BF16)
        kh = kc[:, sl] + kpe
        ss = jnp.sum(kh * kh, axis=-1, keepdims=True)
        km_ref[0, :, sl] = (kh * lax.rsqrt(ss * inv_d + EPS) * gkm_ref[...]).astype(BF16)

    fqn = _fox_qk_norm(proj[:, C_FQ:C_FQ + FOX_WIDTH], bd_ref[...], gqf_ref[...])
    fkn = _fox_qk_norm(proj[:, C_FK:C_FK + FOX_WIDTH], bd_ref[...], gkf_ref[...])
    _store_v_with_ones(vf_ref, proj[:, C_FV:C_FV + FOX_WIDTH])

    z = proj[:, C_FF:C_FF + LANE] + bf_ref[...]
    log_f = jnp.minimum(z, 0.0) - jnp.log1p(jnp.exp(-jnp.abs(z)))
    sums = _dot(tri_ref[...], jnp.concatenate(_split3(log_f), axis=1))
    cum = sums[:, :LANE] + sums[:, LANE:2 * LANE] + sums[:, 2 * LANE:] + carry_ref[...]
    carry_ref[...] = cum[cum.shape[0] - 1:, :]
    hi, mid, lo = _split3(cum * LOG2E)
    lane = lax.broadcasted_iota(jnp.int32, hi.shape, 1)
    pieces = jnp.where(lane < FOX_HEADS, hi, jnp.where(lane < 2 * FOX_HEADS, mid, lo))
    auxq = (_dot(pieces, eq_ref[...]) + oq_ref[...]).astype(BF16)
    auxk = (_dot(pieces, ek_ref[...]) + ok_ref[...]).astype(BF16)
    for p in range(N_PAIRS):
        src = slice(p * LANE, (p + 1) * LANE)
        qf_ref[0, :, p * PAIR:p * PAIR + LANE] = fqn[:, src]
        qf_ref[0, :, p * PAIR + LANE:(p + 1) * PAIR] = auxq[:, src]
        kf_ref[0, :, p * PAIR:p * PAIR + LANE] = fkn[:, src]
        kf_ref[0, :, p * PAIR + LANE:(p + 1) * PAIR] = auxk[:, src]

    u_ref[0] = proj[:, C_S5:C_S5 + S5_WIDTH]


def _layer(arr, l, single_buffer=False):
    n = arr.ndim - 1
    mode = dict(pipeline_mode=pl.Buffered(1)) if single_buffer else {}
    return pl.BlockSpec((None,) + arr.shape[1:], lambda *_: (l,) + (0,) * n, **mode)


def _pack_w_in(w_in):
    wt = jnp.transpose(w_in, (0, 2, 1))
    L = wt.shape[0]
    half = MLA_ROPE // 2
    zeros = lambda n: jnp.zeros((L, n, D_MODEL), F32)
    kpe_lo = wt[:, O_KPE:O_KPE + half]
    kpe_hi = wt[:, O_KPE + half:O_KPE + MLA_ROPE]
    tail = zeros(LANE - MLA_NOPE - MLA_ROPE)
    ff = wt[:, O_FF:O_FF + FOX_HEADS]
    pieces = [wt[:, O_CQ:O_CQ + MLA_Q_RANK + MLA_KV_RANK],
              zeros(MLA_NOPE), kpe_lo, kpe_hi, tail,
              zeros(MLA_NOPE), kpe_hi, kpe_lo, tail,
              wt[:, O_FQ:O_FQ + 3 * FOX_WIDTH],
              ff, ff, ff, zeros(LANE - N_SPLIT * FOX_HEADS),
              wt[:, O_S5U:O_S5U + S5_WIDTH]]
    return jnp.concatenate(pieces, axis=1).astype(BF16), wt[:, O_GATE:O_END].astype(BF16)


def _prep_weights(pr, wcat):
    L = wcat.shape[0]
    row = lambda a: a[:, None, :]
    half = MLA_ROPE // 2

    wq = pr["mla_w_q_up"].reshape(L, MLA_Q_RANK, MLA_HEADS, MLA_QK_DIM)
    nope, pe = wq[..., :MLA_NOPE], wq[..., MLA_NOPE:]
    pe_sw = jnp.concatenate([pe[..., half:], pe[..., :half]], axis=3)
    z32 = jnp.zeros((L, MLA_Q_RANK, MLA_HEADS, HEAD_PAD - MLA_QK_DIM), F32)
    wqa = jnp.concatenate([nope, pe, z32], axis=3).reshape(L, MLA_Q_RANK, MLA_HEADS * HEAD_PAD).astype(BF16)
    wqb = jnp.concatenate([jnp.zeros_like(nope), pe_sw, z32], axis=3).reshape(L, MLA_Q_RANK, -1).astype(BF16)

    wkv = pr["mla_w_kv_up"].reshape(L, MLA_KV_RANK, MLA_HEADS, MLA_NOPE + MLA_V)
    wk = jnp.concatenate([wkv[..., :MLA_NOPE], jnp.zeros((L, MLA_KV_RANK, MLA_HEADS, HEAD_PAD - MLA_NOPE), F32)],
                         axis=3).reshape(L, MLA_KV_RANK, -1).astype(BF16)
    wv = wkv[..., MLA_NOPE:].reshape(L, MLA_KV_RANK, MLA_WIDTH).astype(BF16)

    pad = jnp.zeros((L, HEAD_PAD - MLA_QK_DIM), F32)
    gqm = row(jnp.concatenate([pr["mla_q_norm"], pad], axis=1) * (LOG2E / math.sqrt(MLA_QK_DIM)))
    gkm = row(jnp.concatenate([pr["mla_k_norm"], pad], axis=1))
    bf = row(jnp.concatenate([pr["fox_b_f"]] * N_SPLIT + [jnp.zeros((L, LANE - N_SPLIT * FOX_HEADS), F32)], axis=1))
    gqf = row(jnp.tile(pr["fox_q_norm"], (1, FOX_HEADS)) * (LOG2E / math.sqrt(FOX_HEAD_DIM)))
    gkf = row(jnp.tile(pr["fox_k_norm"], (1, FOX_HEADS)))
    return dict(ng=row(pr["norm_g"]), wcat=wcat, qan=row(pr["mla_q_a_norm"]), wqa=wqa, wqb=wqb,
                kvn=row(pr["mla_kv_a_norm"]), wk=wk, wv=wv, gqm=gqm, gkm=gkm, bf=bf, gqf=gqf, gkf=gkf)


def _prep_consts(tm):
    bd = np.kron(np.eye(PAIR // FOX_HEAD_DIM, dtype=np.float32), np.ones((FOX_HEAD_DIM, FOX_HEAD_DIM), np.float32))
    tri = np.tril(np.ones((tm, tm), np.float32))
    eq = np.zeros((LANE, N_PAIRS * LANE), np.float32)
    ek = np.zeros_like(eq)
    oq = np.zeros((1, N_PAIRS * LANE), np.float32)
    ok = np.zeros_like(oq)
    for hd in range(FOX_HEADS):
        p, a = divmod(hd, 2)
        base = p * LANE + 2 * N_SPLIT * a
        for i in range(N_SPLIT):
            ek[i * FOX_HEADS + hd, base + i] = -1.0
            oq[0, base + i] = 1.0
            eq[i * FOX_HEADS + hd, base + N_SPLIT + i] = 1.0
            ok[0, base + N_SPLIT + i] = 1.0
    return dict(bd=jnp.asarray(bd, BF16), tri=jnp.asarray(tri, BF16),
                eq=jnp.asarray(eq, BF16), ek=jnp.asarray(ek, BF16), oq=jnp.asarray(oq), ok=jnp.asarray(ok))


def _prep_call(x, ca, sb, w, c, l, tm):
    B, S, _ = x.shape
    tok = lambda width: pl.BlockSpec((1, tm, width), lambda b, s: (b, s, 0))
    stacked = [w["ng"], w["wcat"], w["qan"], w["wqa"], w["wqb"], w["kvn"], w["wk"], w["wv"],
               w["gqm"], w["gkm"], w["bf"], w["gqf"], w["gkf"]]
    consts = [c["bd"], c["tri"], c["eq"], c["ek"], c["oq"], c["ok"]]
    args = [x, ca, sb] + stacked + consts
    in_specs = ([tok(D_MODEL), tok(LANE), tok(LANE)] + [_layer(a, l) for a in stacked]
                + [_full(a.shape) for a in consts])
    widths = [MLA_HEADS * HEAD_PAD, MLA_HEADS * HEAD_PAD, N_PAIRS * PAIR, N_PAIRS * PAIR, N_PAIRS * PAIR, N_PAIRS * PAIR]
    out_shape = [jax.ShapeDtypeStruct((B, S, wd), BF16) for wd in widths]
    out_shape.append(jax.ShapeDtypeStruct((B, S, S5_WIDTH), F32))
    return pl.pallas_call(
        _prep_kernel,
        grid=(B, S // tm),
        in_specs=in_specs,
        out_specs=[tok(wd) for wd in widths] + [tok(S5_WIDTH)],
        out_shape=out_shape,
        scratch_shapes=[pltpu.VMEM((1, LANE), F32)],
        compiler_params=_cparams(("parallel", "arbitrary")),
        name="prep",
    )(*args)


def _diag_tile(tq):
    return tq // 2 if tq >= 2 * PAIR else tq


def _attn_kernel(q_ref, k_ref, v_ref, hm_ref, o_ref, qm_ref, sa_ref, sb_ref, bias_ref, m_ref, acc_ref, *,
                 tq, chunk_causal):
    S = q_ref.shape[1]
    nq = S // tq
    td = _diag_tile(tq)
    steps = [(qi, j) for qi in range(nq) for j in range(qi + 1)]
    row = lax.broadcasted_iota(jnp.int32, (td, td), 0)
    col = lax.broadcasted_iota(jnp.int32, (td, td), 1)
    shift = int(math.log2(CHUNK)) if chunk_causal else 0
    bias_ref[...] = jnp.where((col >> shift) <= (row >> shift), 0.0, -jnp.inf)
    for a in range(2):
        qm_ref[a] = q_ref[0] * hm_ref[a:a + 1, :]

    def row_blocks(qi, j):
        if j < qi:
            return [(0, tq, tq)]
        return [(r * td, td, (r + 1) * td) for r in range(tq // td)]

    def qk(qi, j, s_ref):
        for r0, nr, nk in row_blocks(qi, j):
            k = k_ref[0, j * tq:j * tq + nk, :]
            for a in range(2):
                q = qm_ref[a, qi * tq + r0:qi * tq + r0 + nr, :]
                s_ref[a, r0:r0 + nr, :nk] = lax.dot_general(q, k, (((1,), (1,)), ((), ())),
                                                            preferred_element_type=F32)

    qk(0, 0, sa_ref)
    bufs = (sa_ref, sb_ref)
    for t, (qi, j) in enumerate(steps):
        s_cur, s_next = bufs[t % 2], bufs[(t + 1) % 2]
        if t + 1 < len(steps):
            qk(*steps[t + 1], s_next)
        for r0, nr, nk in row_blocks(qi, j):
            rows = slice(r0, r0 + nr)
            lane = lax.broadcasted_iota(jnp.int32, (nr, LANE), 1)
            outs = []
            for a in range(2):
                s = s_cur[a, rows, :nk]
                if j == qi:
                    tail = s[:, nk - td:] + bias_ref[...]
                    s = tail if nk == td else jnp.concatenate([s[:, :nk - td], tail], axis=1)
                v = v_ref[0, j * tq:j * tq + nk, a * LANE:(a + 1) * LANE]
                m_cur = jnp.max(s, axis=-1, keepdims=True)
                if j == 0:
                    m_new = jnp.broadcast_to(m_cur, (nr, LANE))
                    p = jnp.exp2(s - m_cur)
                    acc = _dot(p.astype(BF16), v)
                else:
                    m_prev = m_ref[a, rows]
                    m_new = jnp.maximum(m_prev, m_cur)
                    alpha = jnp.exp2(m_prev - m_new)
                    p = jnp.exp2(s - jnp.concatenate([m_new] * (nk // LANE), axis=1))
                    acc = alpha * acc_ref[a, rows] + _dot(p.astype(BF16), v)
                if j == qi:
                    outs.append(acc / pltpu.roll(acc, LANE // 2, axis=1))
                else:
                    acc_ref[a, rows] = acc
                    m_ref[a, rows] = m_new
            if j == qi:
                out = jnp.where(lane < LANE // 2, outs[0], outs[1]).astype(o_ref.dtype)
                o_ref[0, qi * tq + r0:qi * tq + r0 + nr, :] = out


def _attn_call(q, k, v, head_mask, chunk_causal, tq, name):
    B, S, _ = q.shape
    kern = functools.partial(_attn_kernel, tq=tq, chunk_causal=chunk_causal)
    td = _diag_tile(tq)
    return pl.pallas_call(
        kern,
        grid=(B, N_PAIRS),
        in_specs=[pl.BlockSpec((1, S, PAIR), lambda b, p: (b, 0, p)),
                  pl.BlockSpec((1, S, PAIR), lambda b, p: (b, 0, p)),
                  pl.BlockSpec((1, S, PAIR), lambda b, p: (b, 0, p)),
                  _full((2, PAIR))],
        out_specs=pl.BlockSpec((1, S, LANE), lambda b, p: (b, 0, p)),
        out_shape=jax.ShapeDtypeStruct((B, S, N_PAIRS * LANE), BF16),
        scratch_shapes=[pltpu.VMEM((2, S, PAIR), BF16), pltpu.VMEM((2, tq, tq), F32), pltpu.VMEM((2, tq, tq), F32),
                        pltpu.VMEM((td, td), F32), pltpu.VMEM((2, tq, LANE), F32), pltpu.VMEM((2, tq, LANE), F32)],
        compiler_params=_cparams(("parallel", "parallel")),
        name=name,
    )(q, k, v, head_mask)


def _head_masks():
    mla = np.zeros((2, PAIR), np.float32)
    fox = np.zeros((2, PAIR), np.float32)
    for a in range(2):
        mla[a, a * HEAD_PAD:(a + 1) * HEAD_PAD] = 1.0
        fox[a, a * FOX_HEAD_DIM:(a + 1) * FOX_HEAD_DIM] = 1.0
        fox[a, LANE + 2 * N_SPLIT * a:LANE + 2 * N_SPLIT * (a + 1)] = 1.0
    return jnp.asarray(mla, BF16), jnp.asarray(fox, BF16)


def _s5_param_kernel(lr_ref, li_ref, ldt_ref, br_ref, bi_ref, are_ref, aim_ref, bbr_ref, bbi_ref):
    lr = lr_ref[...]
    li = li_ref[...]
    dt = jnp.exp(ldt_ref[...])
    mag = jnp.exp(lr * dt)
    a_re = mag * jnp.cos(li * dt)
    a_im = mag * jnp.sin(li * dt)
    den = lr * lr + li * li
    f_re = ((a_re - 1.0) * lr + a_im * li) / den
    f_im = (a_im * lr - (a_re - 1.0) * li) / den
    br = br_ref[...]
    bi = bi_ref[...]
    are_ref[...] = a_re
    aim_ref[...] = a_im
    bbr_ref[...] = f_re * br - f_im * bi
    bbi_ref[...] = f_re * bi + f_im * br


def _s5_params(pr, n_batch):
    G, N, C = S5_GROUPS, S5_STATE, S5_GROUP
    L = pr["s5_lambda_re"].shape[0]
    rep = lambda a: jnp.repeat(a.reshape(L * G, N), C, axis=0)
    ldt = jnp.broadcast_to(pr["s5_log_dt"][:, :, None], (L, G, N))
    tr = lambda b: jnp.transpose(b, (0, 1, 3, 2)).reshape(L * G * C, N)
    shp = jax.ShapeDtypeStruct((L * G * C, N), F32)
    a_re, a_im, bb_re, bb_im = pl.pallas_call(
        _s5_param_kernel, out_shape=[shp] * 4, name="s5_params",
    )(rep(pr["s5_lambda_re"]), rep(pr["s5_lambda_im"]), rep(ldt), tr(pr["s5_b_re"]), tr(pr["s5_b_im"]))
    in_mask = jnp.asarray(np.arange(G * C)[:, None] // C == np.arange(G * N)[None, :] // N)
    out_mask = jnp.asarray(np.arange(G * N)[:, None] // N == np.arange(G * C)[None, :] // C)

    def blockdiag_in(bb):
        return jnp.where(in_mask, jnp.tile(bb.reshape(L, G * C, N), (1, 1, G)), 0.0)

    def blockdiag_out(cc):
        rows = jnp.transpose(cc, (0, 1, 3, 2)).reshape(L, G * N, C)
        return jnp.where(out_mask, jnp.tile(rows, (1, 1, G)), 0.0).astype(BF16)

    w_b = jnp.concatenate([blockdiag_in(bb_re), blockdiag_in(bb_im)], axis=2).astype(BF16)
    w_cre = blockdiag_out(pr["s5_c_re"])
    w_cim = -blockdiag_out(pr["s5_c_im"])
    wb_t, wc_t = [], []
    for g in range(S5_SCAN_GROUPS):
        for k in range(2 * S5_GROUP_TILES):
            n = S5_GROUP_TILES * g + k % S5_GROUP_TILES
            c0 = n // 2 * LANE
            col0 = (k // S5_GROUP_TILES) * G * N + n * PAIR
            wb_t.append(w_b[:, c0:c0 + LANE, col0:col0 + PAIR])
            wc = w_cre if k < S5_GROUP_TILES else w_cim
            wc_t.append(wc[:, n * PAIR:(n + 1) * PAIR, g * PAIR:(g + 1) * PAIR])
    first = lambda a: a.reshape(L, G, C, N)[:, :, 0, :].reshape(L, S5_SCAN_GROUPS, S5_GROUP_TILES, 1, PAIR)
    a_rep = jnp.broadcast_to(jnp.concatenate([first(a_re), first(a_im)], axis=2),
                             (L, S5_SCAN_GROUPS, 2 * S5_GROUP_TILES, n_batch, PAIR))
    return dict(a=a_rep, w_b=jnp.stack(wb_t, axis=1), w_c=jnp.stack(wc_t, axis=1),
                d=pr["s5_d"][:, None, :], w_glu=pr["s5_w_glu"].astype(BF16), b_glu=pr["s5_b_glu"][:, None, :])


def _s5_kernel(u_ref, a_ref, wb_ref, wc_ref, d_ref, wg_ref, bg_ref, o_ref, st_ref, us_ref, zs_ref, ub_ref, uf_ref,
               y_ref, bu0_ref, bu1_ref, x0_ref, x1_ref, *, nb, ts, n_sub):
    @pl.when(pl.program_id(0) == 0)
    def _():
        st_ref[...] = jnp.zeros_like(st_ref)

    pitch = ts + S5_ROW_PAD
    nk = S5_WIDTH // LANE
    gt = S5_GROUP_TILES
    n_iter = ts // S5_SCAN_UNROLL
    tiles_per_iter = 2 * gt // n_iter
    steps_per_tile = S5_SCAN_UNROLL // tiles_per_iter
    bu = (bu0_ref, bu1_ref)
    xs = (x0_ref, x1_ref)
    everything = slice(None)

    def stage_u(c):
        for b in range(nb):
            for k in range(nk):
                us_ref[k, b * pitch:b * pitch + ts, :] = u_ref[b, c * ts:(c + 1) * ts, k * LANE:(k + 1) * LANE]
        u = jnp.concatenate(
            [jnp.concatenate([us_ref[k, pl.ds(t, nb, stride=pitch), :] for k in range(nk)], axis=1) for t in range(ts)],
            axis=0)
        uf_ref[c % 2] = u
        for k in range(nk):
            ub_ref[k] = u[:, k * LANE:(k + 1) * LANE].astype(BF16)

    def b_item(g, k, rows):
        cb = (gt * g) // 2 + (k % gt) // 2
        bu[g][k, rows, :] = _dot(ub_ref[cb, rows, :], wb_ref[2 * gt * g + k])

    def c_item(g, k, rows):
        y_ref[g, rows, :] += _dot(xs[g][k, rows, :].astype(BF16), wc_ref[2 * gt * g + k])

    def scan_phase(g, b_target, c_source):
        ar = a_ref[g, :gt]
        ai = a_ref[g, gt:]
        if c_source is not None:
            y_ref[c_source] = jnp.zeros(y_ref.shape[1:], F32)

        def body(i, carry):
            xr, xi = carry
            for j in range(tiles_per_iter):
                k = i * tiles_per_iter + j
                if b_target is not None:
                    b_item(b_target, k, everything)
                if c_source is not None:
                    c_item(c_source, k, everything)
                for tt in range(steps_per_tile):
                    t = i * S5_SCAN_UNROLL + j * steps_per_tile + tt
                    r = pl.ds(t * nb, nb) if n_iter == 1 else pl.ds(pl.multiple_of(t * nb, nb), nb)
                    nr = ar * xr - ai * xi + bu[g][:gt, r, :]
                    ni = ar * xi + ai * xr + bu[g][gt:, r, :]
                    xs[g][:gt, r, :] = nr
                    xs[g][gt:, r, :] = ni
                    xr, xi = nr, ni
            return xr, xi

        carry = (st_ref[g, :gt], st_ref[g, gt:])
        xr, xi = body(0, carry) if n_iter == 1 else lax.fori_loop(0, n_iter, body, carry)
        st_ref[g, :gt] = xr
        st_ref[g, gt:] = xi

    def epilogue(c):
        y = jnp.concatenate([y_ref[g] for g in range(S5_SCAN_GROUPS)], axis=1) + d_ref[...] * uf_ref[c % 2]
        z = jax.nn.gelu(y, approximate=True)
        z = z * jax.nn.sigmoid(_dot(z.astype(BF16), wg_ref[...]) + bg_ref[...])
        for t in range(ts):
            for k in range(nk):
                zs_ref[k, pl.ds(t, nb, stride=pitch), :] = z[t * nb:(t + 1) * nb, k * LANE:(k + 1) * LANE]
        for b in range(nb):
            rows = jnp.concatenate([zs_ref[k, b * pitch:b * pitch + ts, :] for k in range(nk)], axis=1)
            o_ref[b, c * ts:(c + 1) * ts, :] = rows.astype(o_ref.dtype)

    stage_u(0)
    for k in range(2 * gt):
        b_item(0, k, everything)
    for c in range(n_sub):
        scan_phase(0, 1, 1 if c > 0 else None)
        if c > 0:
            epilogue(c - 1)
        if c + 1 < n_sub:
            stage_u(c + 1)
        scan_phase(1, 0 if c + 1 < n_sub else None, 0)
    y_ref[1] = jnp.zeros(y_ref.shape[1:], F32)
    for k in range(2 * gt):
        c_item(1, k, everything)
    epilogue(n_sub - 1)


def _s5_call(u, p, l, ts, n_sub):
    B, S, _ = u.shape
    blk = ts * n_sub
    rows = ts * B
    args = [u, p["a"], p["w_b"], p["w_c"], p["d"], p["w_glu"], p["b_glu"]]
    kern = functools.partial(_s5_kernel, nb=B, ts=ts, n_sub=n_sub)
    once = lambda a: pl.BlockSpec((None,) + a.shape[1:], lambda i: (l,) + (0,) * (a.ndim - 1),
                                  pipeline_mode=pl.Buffered(1))
    slab_buf = pltpu.VMEM((S5_WIDTH // LANE, B * (ts + S5_ROW_PAD), LANE), F32)
    tile_buf = pltpu.VMEM((2 * S5_GROUP_TILES, rows, PAIR), F32)
    return pl.pallas_call(
        kern,
        grid=(S // blk,),
        in_specs=[pl.BlockSpec((B, blk, S5_WIDTH), lambda i: (0, i, 0))] + [once(a) for a in args[1:]],
        out_specs=pl.BlockSpec((B, blk, S5_WIDTH), lambda i: (0, i, 0)),
        out_shape=jax.ShapeDtypeStruct((B, S, S5_WIDTH), BF16),
        scratch_shapes=[pltpu.VMEM((S5_SCAN_GROUPS, 2 * S5_GROUP_TILES, B, PAIR), F32), slab_buf, slab_buf,
                        pltpu.VMEM((S5_WIDTH // LANE, rows, LANE), BF16), pltpu.VMEM((2, rows, S5_WIDTH), F32),
                        pltpu.VMEM((S5_SCAN_GROUPS, rows, PAIR), F32), tile_buf, tile_buf, tile_buf, tile_buf],
        compiler_params=_cparams(("arbitrary",)),
        name="s5",
    )(*args)


def _merge_kernel(x_ref, ym_ref, yf_ref, ys_ref, ng_ref, wgm_ref, wo_ref, wout_ref, o_ref):
    x = x_ref[0]
    h = _rms(x, ng_ref[...]).astype(BF16)
    merged = None
    for b, y_ref in enumerate((ym_ref, yf_ref, ys_ref)):
        g = _dot_nt(h, wgm_ref[b * MLA_WIDTH:(b + 1) * MLA_WIDTH, :])
        gated = (y_ref[0].astype(F32) * (g * jax.nn.sigmoid(g))).astype(BF16)
        o = _dot(gated, wo_ref[b * MLA_WIDTH:(b + 1) * MLA_WIDTH, :])
        m = _dot_nt(h, wgm_ref[BRANCH_WIDTH + b * D_MODEL:BRANCH_WIDTH + (b + 1) * D_MODEL, :])
        term = jax.nn.sigmoid(m) * o
        merged = term if merged is None else merged + term
    o_ref[0] = x + _dot(merged.astype(BF16), wout_ref[...])


def _merge_weights(pr, wgm):
    return [pr["norm_g"][:, None, :], wgm, pr["w_branch_out"].astype(BF16), pr["w_out"].astype(BF16)]


def _merge_call(x, y_mla, y_fox, y_s5, w, l, tm):
    B, S, _ = x.shape
    tok = lambda width: pl.BlockSpec((1, tm, width), lambda b, s: (b, s, 0))
    args = [x, y_mla, y_fox, y_s5] + w
    return pl.pallas_call(
        _merge_kernel,
        grid=(B, S // tm),
        in_specs=([tok(D_MODEL), tok(MLA_WIDTH), tok(FOX_WIDTH), tok(S5_WIDTH)]
                  + [_layer(a, l, single_buffer=True) for a in w]),
        out_specs=tok(D_MODEL),
        out_shape=jax.ShapeDtypeStruct(x.shape, x.dtype),
        compiler_params=_cparams(("parallel", "parallel")),
        name="merge",
    )(*args)


_LAYER_KEYS = ("norm_g", "w_in", "mla_q_a_norm", "mla_w_q_up", "mla_kv_a_norm", "mla_w_kv_up", "mla_q_norm",
               "mla_k_norm", "fox_b_f", "fox_q_norm", "fox_k_norm", "s5_lambda_re", "s5_lambda_im", "s5_log_dt",
               "s5_b_re", "s5_b_im", "s5_c_re", "s5_c_im", "s5_d", "s5_w_glu", "s5_b_glu", "w_branch_out", "w_out")


def _forward(x, positions, params):
    B, S, _ = x.shape
    tm = min(S, 512)
    tq = min(S, 512)
    ts = min(S, 64)
    depth = params["w_in"].shape[0]
    ca, sb = _rope_tables(positions)
    consts = _prep_consts(tm)
    mask_mla, mask_fox = _head_masks()
    wcat, wgm = _pack_w_in(params["w_in"])
    w_prep = _prep_weights(params, wcat)
    w_s5 = _s5_params(params, B)
    w_merge = _merge_weights(params, wgm)
    h = x
    for l in range(depth):
        qm, km, vm, qf, kf, vf, u = _prep_call(h, ca, sb, w_prep, consts, l, tm)
        y_mla = _attn_call(qm, km, vm, mask_mla, True, tq, "attn_mla")
        y_fox = _attn_call(qf, kf, vf, mask_fox, False, tq, "attn_fox")
        y_s5 = _s5_call(u, w_s5, l, ts, min(S // ts, 4))
        h = _merge_call(h, y_mla, y_fox, y_s5, w_merge, l, min(S, 2 * tm))
    return h


def kernel(x, positions, norm_g, w_in, mla_q_a_norm, mla_w_q_up, mla_kv_a_norm, mla_w_kv_up, mla_q_norm, mla_k_norm, fox_b_f, fox_q_norm, fox_k_norm, s5_lambda_re, s5_lambda_im, s5_log_dt, s5_b_re, s5_b_im, s5_c_re, s5_c_im, s5_d, s5_w_glu, s5_b_glu, w_branch_out, w_out):
    params = dict(zip(_LAYER_KEYS, (norm_g, w_in, mla_q_a_norm, mla_w_q_up, mla_kv_a_norm, mla_w_kv_up, mla_q_norm,
                                    mla_k_norm, fox_b_f, fox_q_norm, fox_k_norm, s5_lambda_re, s5_lambda_im,
                                    s5_log_dt, s5_b_re, s5_b_im, s5_c_re, s5_c_im, s5_d, s5_w_glu, s5_b_glu,
                                    w_branch_out, w_out)))
    return _forward(x, positions, params)
```

```python
import functools
import math

import numpy as np
import jax
import jax.numpy as jnp
from jax import lax
from jax.experimental import pallas as pl
from jax.experimental.pallas import tpu as pltpu

F32 = jnp.float32
BF16 = jnp.bfloat16

D_MODEL = 1024
CHUNK = 64
EPS = 1e-6

MLA_HEADS = 8
MLA_NOPE = 64
MLA_ROPE = 32
MLA_V = 64
MLA_Q_RANK = 256
MLA_KV_RANK = 128
MLA_WIDTH = MLA_HEADS * MLA_V
MLA_QK_DIM = MLA_NOPE + MLA_ROPE
ROPE_THETA = 10000.0

FOX_HEADS = 8
FOX_HEAD_DIM = 64
FOX_WIDTH = FOX_HEADS * FOX_HEAD_DIM

S5_WIDTH = 512
S5_GROUP = 16
S5_GROUPS = S5_WIDTH // S5_GROUP
S5_STATE = 64
S5_STATES = S5_GROUPS * S5_STATE
S5_ROW_PAD = 8
S5_SCAN_GROUPS = 2
S5_GROUP_TILES = 4
S5_SCAN_UNROLL = 64

BRANCH_WIDTH = MLA_WIDTH + FOX_WIDTH + S5_WIDTH

LANE = 128
HEAD_PAD = 128
PAIR = 2 * HEAD_PAD
N_PAIRS = 4
N_SPLIT = 3
LOG2E = math.log2(math.e)

_OFF = np.cumsum([0, MLA_Q_RANK, MLA_KV_RANK, MLA_ROPE, FOX_WIDTH, FOX_WIDTH, FOX_WIDTH, FOX_HEADS,
                  S5_WIDTH, MLA_WIDTH, FOX_WIDTH, S5_WIDTH, D_MODEL, D_MODEL, D_MODEL]).tolist()
(O_CQ, O_CKV, O_KPE, O_FQ, O_FK, O_FV, O_FF, O_S5U, O_GATE, _o1, _o2, O_MERGE, _o3, _o4, O_END) = _OFF

C_CQ = 0
C_CKV = C_CQ + MLA_Q_RANK
C_KPA = C_CKV + MLA_KV_RANK
C_KPB = C_KPA + LANE
C_FQ = C_KPB + LANE
C_FK = C_FQ + FOX_WIDTH
C_FV = C_FK + FOX_WIDTH
C_FF = C_FV + FOX_WIDTH
C_S5 = C_FF + LANE
C_END = C_S5 + S5_WIDTH

VMEM_LIMIT = 56 * 1024 * 1024


def _cparams(sem):
    return pltpu.CompilerParams(dimension_semantics=sem, vmem_limit_bytes=VMEM_LIMIT)


def _full(shape):
    n = len(shape)
    return pl.BlockSpec(shape, lambda *_: (0,) * n)


def _rms(x, g):
    return x * lax.rsqrt(jnp.mean(x * x, axis=-1, keepdims=True) + EPS) * g


def _dot(a, b):
    return jnp.dot(a, b, preferred_element_type=F32)


def _dot_nt(a, b):
    return lax.dot_general(a, b, (((1,), (1,)), ((), ())), preferred_element_type=F32)


def _store_v_with_ones(v_ref, v):
    lane = lax.broadcasted_iota(jnp.int32, (v.shape[0], LANE), 1)
    for p in range(N_PAIRS):
        vp = v[:, p * LANE:(p + 1) * LANE]
        v_ref[0, :, p * PAIR:p * PAIR + LANE] = jnp.where(lane < LANE // 2, vp, 1.0).astype(v_ref.dtype)
        v_ref[0, :, p * PAIR + LANE:(p + 1) * PAIR] = jnp.where(lane < LANE // 2, 1.0, vp).astype(v_ref.dtype)


def _rope_kernel(pos_ref, inv_ref, sign_ref, ca_ref, sb_ref):
    ang = pos_ref[0].astype(F32) * inv_ref[...]
    ca_ref[0] = jnp.cos(ang)
    sb_ref[0] = jnp.sin(ang) * sign_ref[...]


def _rope_tables(positions):
    B, S = positions.shape
    ts = min(S, 512)
    inv = 1.0 / (ROPE_THETA ** (jnp.arange(0, MLA_ROPE, 2, dtype=F32) / MLA_ROPE))
    half = MLA_ROPE // 2
    inv_row = jnp.concatenate([jnp.zeros((MLA_NOPE,), F32), inv, inv,
                               jnp.zeros((LANE - MLA_NOPE - MLA_ROPE,), F32)])[None]
    sign = np.zeros((1, LANE), np.float32)
    sign[0, MLA_NOPE:MLA_NOPE + half] = -1.0
    sign[0, MLA_NOPE + half:MLA_NOPE + MLA_ROPE] = 1.0
    blk = pl.BlockSpec((1, ts, LANE), lambda b, s: (b, s, 0))
    return pl.pallas_call(
        _rope_kernel,
        grid=(B, S // ts),
        in_specs=[pl.BlockSpec((1, ts, 1), lambda b, s: (b, s, 0)), _full((1, LANE)), _full((1, LANE))],
        out_specs=[blk, blk],
        out_shape=[jax.ShapeDtypeStruct((B, S, LANE), F32)] * 2,
        compiler_params=_cparams(("parallel", "parallel")),
        name="rope_tables",
    )(positions.reshape(B, S, 1), inv_row, jnp.asarray(sign))


def _split3(x):
    hi = x.astype(BF16)
    r1 = x - hi.astype(F32)
    mid = r1.astype(BF16)
    return hi, mid, (r1 - mid.astype(F32)).astype(BF16)


def _fox_qk_norm(x, bd, gain):
    sq = (x * x).astype(BF16)
    ss = jnp.concatenate([_dot(sq[:, c:c + PAIR], bd) for c in range(0, FOX_WIDTH, PAIR)], axis=1)
    return (x * lax.rsqrt(ss * (1.0 / FOX_HEAD_DIM) + EPS) * gain).astype(BF16)


def _prep_kernel(x_ref, ca_ref, sb_ref, ng_ref, wcat_ref, qan_ref, wqa_ref, wqb_ref, kvn_ref, wk_ref, wv_ref,
                 gqm_ref, gkm_ref, bf_ref, gqf_ref, gkf_ref, bd_ref, tri_ref,
                 eq_ref, ek_ref, oq_ref, ok_ref,
                 qm_ref, km_ref, vm_ref, qf_ref, kf_ref, vf_ref, u_ref, carry_ref):
    @pl.when(pl.program_id(1) == 0)
    def _():
        carry_ref[...] = jnp.zeros_like(carry_ref)

    x = x_ref[0]
    h = _rms(x, ng_ref[...]).astype(BF16)
    proj = _dot_nt(h, wcat_ref[...])
    ca = ca_ref[0]
    sb = sb_ref[0]

    cqn = _rms(proj[:, C_CQ:C_CQ + MLA_Q_RANK], qan_ref[...]).astype(BF16)
    qa = _dot(cqn, wqa_ref[...])
    qb = _dot(cqn, wqb_ref[...])
    ckvn = _rms(proj[:, C_CKV:C_CKV + MLA_KV_RANK], kvn_ref[...]).astype(BF16)
    kc = _dot(ckvn, wk_ref[...])
    _store_v_with_ones(vm_ref, _dot(ckvn, wv_ref[...]))
    kpe = proj[:, C_KPA:C_KPA + LANE] * ca + proj[:, C_KPB:C_KPB + LANE] * sb
    inv_d = 1.0 / MLA_QK_DIM
    for hd in range(MLA_HEADS):
        sl = slice(hd * HEAD_PAD, (hd + 1) * HEAD_PAD)
        qh = qa[:, sl] * ca + qb[:, sl] * sb
        ss = jnp.sum(qh * qh, axis=-1, keepdims=True)
        qm_ref[0, :, sl] = (qh * lax.rsqrt(ss * inv_d + EPS) * gqm_ref[...]).astype(BF16)
        kh = kc[:, sl] + kpe
        ss = jnp.sum(kh * kh, axis=-1, keepdims=True)
        km_ref[0, :, sl] = (kh * lax.rsqrt(ss * inv_d + EPS) * gkm_ref[...]).astype(BF16)

    fqn = _fox_qk_norm(proj[:, C_FQ:C_FQ + FOX_WIDTH], bd_ref[...], gqf_ref[...])
    fkn = _fox_qk_norm(proj[:, C_FK:C_FK + FOX_WIDTH], bd_ref[...], gkf_ref[...])
    _store_v_with_ones(vf_ref, proj[:, C_FV:C_FV + FOX_WIDTH])

    z = proj[:, C_FF:C_FF + LANE] + bf_ref[...]
    log_f = jnp.minimum(z, 0.0) - jnp.log1p(jnp.exp(-jnp.abs(z)))
    sums = _dot(tri_ref[...], jnp.concatenate(_split3(log_f), axis=1))
    cum = sums[:, :LANE] + sums[:, LANE:2 * LANE] + sums[:, 2 * LANE:] + carry_ref[...]
    carry_ref[...] = cum[cum.shape[0] - 1:, :]
    hi, mid, lo = _split3(cum * LOG2E)
    lane = lax.broadcasted_iota(jnp.int32, hi.shape, 1)
    pieces = jnp.where(lane < FOX_HEADS, hi, jnp.where(lane < 2 * FOX_HEADS, mid, lo))
    auxq = (_dot(pieces, eq_ref[...]) + oq_ref[...]).astype(BF16)
    auxk = (_dot(pieces, ek_ref[...]) + ok_ref[...]).astype(BF16)
    for p in range(N_PAIRS):
        src = slice(p * LANE, (p + 1) * LANE)
        qf_ref[0, :, p * PAIR:p * PAIR + LANE] = fqn[:, src]
        qf_ref[0, :, p * PAIR + LANE:(p + 1) * PAIR] = auxq[:, src]
        kf_ref[0, :, p * PAIR:p * PAIR + LANE] = fkn[:, src]
        kf_ref[0, :, p * PAIR + LANE:(p + 1) * PAIR] = auxk[:, src]

    u_ref[0] = proj[:, C_S5:C_S5 + S5_WIDTH]


def _layer(arr, l, single_buffer=False):
    n = arr.ndim - 1
    mode = dict(pipeline_mode=pl.Buffered(1)) if single_buffer else {}
    return pl.BlockSpec((None,) + arr.shape[1:], lambda *_: (l,) + (0,) * n, **mode)


def _pack_w_in(w_in):
    wt = jnp.transpose(w_in, (0, 2, 1))
    L = wt.shape[0]
    half = MLA_ROPE // 2
    zeros = lambda n: jnp.zeros((L, n, D_MODEL), F32)
    kpe_lo = wt[:, O_KPE:O_KPE + half]
    kpe_hi = wt[:, O_KPE + half:O_KPE + MLA_ROPE]
    tail = zeros(LANE - MLA_NOPE - MLA_ROPE)
    ff = wt[:, O_FF:O_FF + FOX_HEADS]
    pieces = [wt[:, O_CQ:O_CQ + MLA_Q_RANK + MLA_KV_RANK],
              zeros(MLA_NOPE), kpe_lo, kpe_hi, tail,
              zeros(MLA_NOPE), kpe_hi, kpe_lo, tail,
              wt[:, O_FQ:O_FQ + 3 * FOX_WIDTH],
              ff, ff, ff, zeros(LANE - N_SPLIT * FOX_HEADS),
              wt[:, O_S5U:O_S5U + S5_WIDTH]]
    return jnp.concatenate(pieces, axis=1).astype(BF16), wt[:, O_GATE:O_END].astype(BF16)


def _prep_weights(pr, wcat):
    L = wcat.shape[0]
    row = lambda a: a[:, None, :]
    half = MLA_ROPE // 2

    wq = pr["mla_w_q_up"].reshape(L, MLA_Q_RANK, MLA_HEADS, MLA_QK_DIM)
    nope, pe = wq[..., :MLA_NOPE], wq[..., MLA_NOPE:]
    pe_sw = jnp.concatenate([pe[..., half:], pe[..., :half]], axis=3)
    z32 = jnp.zeros((L, MLA_Q_RANK, MLA_HEADS, HEAD_PAD - MLA_QK_DIM), F32)
    wqa = jnp.concatenate([nope, pe, z32], axis=3).reshape(L, MLA_Q_RANK, MLA_HEADS * HEAD_PAD).astype(BF16)
    wqb = jnp.concatenate([jnp.zeros_like(nope), pe_sw, z32], axis=3).reshape(L, MLA_Q_RANK, -1).astype(BF16)

    wkv = pr["mla_w_kv_up"].reshape(L, MLA_KV_RANK, MLA_HEADS, MLA_NOPE + MLA_V)
    wk = jnp.concatenate([wkv[..., :MLA_NOPE], jnp.zeros((L, MLA_KV_RANK, MLA_HEADS, HEAD_PAD - MLA_NOPE), F32)],
                         axis=3).reshape(L, MLA_KV_RANK, -1).astype(BF16)
    wv = wkv[..., MLA_NOPE:].reshape(L, MLA_KV_RANK, MLA_WIDTH).astype(BF16)

    pad = jnp.zeros((L, HEAD_PAD - MLA_QK_DIM), F32)
    gqm = row(jnp.concatenate([pr["mla_q_norm"], pad], axis=1) * (LOG2E / math.sqrt(MLA_QK_DIM)))
    gkm = row(jnp.concatenate([pr["mla_k_norm"], pad], axis=1))
    bf = row(jnp.concatenate([pr["fox_b_f"]] * N_SPLIT + [jnp.zeros((L, LANE - N_SPLIT * FOX_HEADS), F32)], axis=1))
    gqf = row(jnp.tile(pr["fox_q_norm"], (1, FOX_HEADS)) * (LOG2E / math.sqrt(FOX_HEAD_DIM)))
    gkf = row(jnp.tile(pr["fox_k_norm"], (1, FOX_HEADS)))
    return dict(ng=row(pr["norm_g"]), wcat=wcat, qan=row(pr["mla_q_a_norm"]), wqa=wqa, wqb=wqb,
                kvn=row(pr["mla_kv_a_norm"]), wk=wk, wv=wv, gqm=gqm, gkm=gkm, bf=bf, gqf=gqf, gkf=gkf)


def _prep_consts(tm):
    bd = np.kron(np.eye(PAIR // FOX_HEAD_DIM, dtype=np.float32), np.ones((FOX_HEAD_DIM, FOX_HEAD_DIM), np.float32))
    tri = np.tril(np.ones((tm, tm), np.float32))
    eq = np.zeros((LANE, N_PAIRS * LANE), np.float32)
    ek = np.zeros_like(eq)
    oq = np.zeros((1, N_PAIRS * LANE), np.float32)
    ok = np.zeros_like(oq)
    for hd in range(FOX_HEADS):
        p, a = divmod(hd, 2)
        base = p * LANE + 2 * N_SPLIT * a
        for i in range(N_SPLIT):
            ek[i * FOX_HEADS + hd, base + i] = -1.0
            oq[0, base + i] = 1.0
            eq[i * FOX_HEADS + hd, base + N_SPLIT + i] = 1.0
            ok[0, base + N_SPLIT + i] = 1.0
    return dict(bd=jnp.asarray(bd, BF16), tri=jnp.asarray(tri, BF16),
                eq=jnp.asarray(eq, BF16), ek=jnp.asarray(ek, BF16), oq=jnp.asarray(oq), ok=jnp.asarray(ok))


def _prep_call(x, ca, sb, w, c, l, tm):
    B, S, _ = x.shape
    tok = lambda width: pl.BlockSpec((1, tm, width), lambda b, s: (b, s, 0))
    stacked = [w["ng"], w["wcat"], w["qan"], w["wqa"], w["wqb"], w["kvn"], w["wk"], w["wv"],
               w["gqm"], w["gkm"], w["bf"], w["gqf"], w["gkf"]]
    consts = [c["bd"], c["tri"], c["eq"], c["ek"], c["oq"], c["ok"]]
    args = [x, ca, sb] + stacked + consts
    in_specs = ([tok(D_MODEL), tok(LANE), tok(LANE)] + [_layer(a, l) for a in stacked]
                + [_full(a.shape) for a in consts])
    widths = [MLA_HEADS * HEAD_PAD, MLA_HEADS * HEAD_PAD, N_PAIRS * PAIR, N_PAIRS * PAIR, N_PAIRS * PAIR, N_PAIRS * PAIR]
    out_shape = [jax.ShapeDtypeStruct((B, S, wd), BF16) for wd in widths]
    out_shape.append(jax.ShapeDtypeStruct((B, S, S5_WIDTH), F32))
    return pl.pallas_call(
        _prep_kernel,
        grid=(B, S // tm),
        in_specs=in_specs,
        out_specs=[tok(wd) for wd in widths] + [tok(S5_WIDTH)],
        out_shape=out_shape,
        scratch_shapes=[pltpu.VMEM((1, LANE), F32)],
        compiler_params=_cparams(("parallel", "arbitrary")),
        name="prep",
    )(*args)


def _diag_tile(tq):
    return tq // 2 if tq >= 2 * PAIR else tq


def _attn_kernel(q_ref, k_ref, v_ref, hm_ref, o_ref, qm_ref, sa_ref, sb_ref, bias_ref, m_ref, acc_ref, *,
                 tq, chunk_causal):
    S = q_ref.shape[1]
    nq = S // tq
    td = _diag_tile(tq)
    steps = [(qi, j) for qi in range(nq) for j in range(qi + 1)]
    row = lax.broadcasted_iota(jnp.int32, (td, td), 0)
    col = lax.broadcasted_iota(jnp.int32, (td, td), 1)
    shift = int(math.log2(CHUNK)) if chunk_causal else 0
    bias_ref[...] = jnp.where((col >> shift) <= (row >> shift), 0.0, -jnp.inf)
    for a in range(2):
        qm_ref[a] = q_ref[0] * hm_ref[a:a + 1, :]

    def row_blocks(qi, j):
        if j < qi:
            return [(0, tq, tq)]
        return [(r * td, td, (r + 1) * td) for r in range(tq // td)]

    def qk(qi, j, s_ref):
        for r0, nr, nk in row_blocks(qi, j):
            k = k_ref[0, j * tq:j * tq + nk, :]
            for a in range(2):
                q = qm_ref[a, qi * tq + r0:qi * tq + r0 + nr, :]
                s_ref[a, r0:r0 + nr, :nk] = lax.dot_general(q, k, (((1,), (1,)), ((), ())),
                                                            preferred_element_type=F32)

    qk(0, 0, sa_ref)
    bufs = (sa_ref, sb_ref)
    for t, (qi, j) in enumerate(steps):
        s_cur, s_next = bufs[t % 2], bufs[(t + 1) % 2]
        if t + 1 < len(steps):
            qk(*steps[t + 1], s_next)
        for r0, nr, nk in row_blocks(qi, j):
            rows = slice(r0, r0 + nr)
            lane = lax.broadcasted_iota(jnp.int32, (nr, LANE), 1)
            outs = []
            for a in range(2):
                s = s_cur[a, rows, :nk]
                if j == qi:
                    tail = s[:, nk - td:] + bias_ref[...]
                    s = tail if nk == td else jnp.concatenate([s[:, :nk - td], tail], axis=1)
                v = v_ref[0, j * tq:j * tq + nk, a * LANE:(a + 1) * LANE]
                m_cur = jnp.max(s, axis=-1, keepdims=True)
                if j == 0:
                    m_new = jnp.broadcast_to(m_cur, (nr, LANE))
                    p = jnp.exp2(s - m_cur)
                    acc = _dot(p.astype(BF16), v)
                else:
                    m_prev = m_ref[a, rows]
                    m_new = jnp.maximum(m_prev, m_cur)
                    alpha = jnp.exp2(m_prev - m_new)
                    p = jnp.exp2(s - jnp.concatenate([m_new] * (nk // LANE), axis=1))
                    acc = alpha * acc_ref[a, rows] + _dot(p.astype(BF16), v)
                if j == qi:
                    outs.append(acc / pltpu.roll(acc, LANE // 2, axis=1))
                else:
                    acc_ref[a, rows] = acc
                    m_ref[a, rows] = m_new
            if j == qi:
                out = jnp.where(lane < LANE // 2, outs[0], outs[1]).astype(o_ref.dtype)
                o_ref[0, qi * tq + r0:qi * tq + r0 + nr, :] = out


def _attn_call(q, k, v, head_mask, chunk_causal, tq, name):
    B, S, _ = q.shape
    kern = functools.partial(_attn_kernel, tq=tq, chunk_causal=chunk_causal)
    td = _diag_tile(tq)
    return pl.pallas_call(
        kern,
        grid=(B, N_PAIRS),
        in_specs=[pl.BlockSpec((1, S, PAIR), lambda b, p: (b, 0, p)),
                  pl.BlockSpec((1, S, PAIR), lambda b, p: (b, 0, p)),
                  pl.BlockSpec((1, S, PAIR), lambda b, p: (b, 0, p)),
                  _full((2, PAIR))],
        out_specs=pl.BlockSpec((1, S, LANE), lambda b, p: (b, 0, p)),
        out_shape=jax.ShapeDtypeStruct((B, S, N_PAIRS * LANE), BF16),
        scratch_shapes=[pltpu.VMEM((2, S, PAIR), BF16), pltpu.VMEM((2, tq, tq), F32), pltpu.VMEM((2, tq, tq), F32),
                        pltpu.VMEM((td, td), F32), pltpu.VMEM((2, tq, LANE), F32), pltpu.VMEM((2, tq, LANE), F32)],
        compiler_params=_cparams(("parallel", "parallel")),
        name=name,
    )(q, k, v, head_mask)


def _head_masks():
    mla = np.zeros((2, PAIR), np.float32)
    fox = np.zeros((2, PAIR), np.float32)
    for a in range(2):
        mla[a, a * HEAD_PAD:(a + 1) * HEAD_PAD] = 1.0
        fox[a, a * FOX_HEAD_DIM:(a + 1) * FOX_HEAD_DIM] = 1.0
        fox[a, LANE + 2 * N_SPLIT * a:LANE + 2 * N_SPLIT * (a + 1)] = 1.0
    return jnp.asarray(mla, BF16), jnp.asarray(fox, BF16)


def _s5_param_kernel(lr_ref, li_ref, ldt_ref, br_ref, bi_ref, are_ref, aim_ref, bbr_ref, bbi_ref):
    lr = lr_ref[...]
    li = li_ref[...]
    dt = jnp.exp(ldt_ref[...])
    mag = jnp.exp(lr * dt)
    a_re = mag * jnp.cos(li * dt)
    a_im = mag * jnp.sin(li * dt)
    den = lr * lr + li * li
    f_re = ((a_re - 1.0) * lr + a_im * li) / den
    f_im = (a_im * lr - (a_re - 1.0) * li) / den
    br = br_ref[...]
    bi = bi_ref[...]
    are_ref[...] = a_re
    aim_ref[...] = a_im
    bbr_ref[...] = f_re * br - f_im * bi
    bbi_ref[...] = f_re * bi + f_im * br


def _s5_params(pr, n_batch):
    G, N, C = S5_GROUPS, S5_STATE, S5_GROUP
    L = pr["s5_lambda_re"].shape[0]
    rep = lambda a: jnp.repeat(a.reshape(L * G, N), C, axis=0)
    ldt = jnp.broadcast_to(pr["s5_log_dt"][:, :, None], (L, G, N))
    tr = lambda b: jnp.transpose(b, (0, 1, 3, 2)).reshape(L * G * C, N)
    shp = jax.ShapeDtypeStruct((L * G * C, N), F32)
    a_re, a_im, bb_re, bb_im = pl.pallas_call(
        _s5_param_kernel, out_shape=[shp] * 4, name="s5_params",
    )(rep(pr["s5_lambda_re"]), rep(pr["s5_lambda_im"]), rep(ldt), tr(pr["s5_b_re"]), tr(pr["s5_b_im"]))
    n_tiles = G * N // PAIR
    gpt = PAIR // N
    t_idx = np.arange(n_tiles)[:, None, None]
    mask_b = jnp.asarray(np.arange(LANE)[None, :, None] // C == gpt * (t_idx % 2) + np.arange(PAIR)[None, None, :] // N)
    mask_c = jnp.asarray(np.arange(PAIR)[None, None, :] // C
                         == gpt * (t_idx % S5_GROUP_TILES) + np.arange(PAIR)[None, :, None] // N)

    def group_major(re, im):
        split = lambda a: a.reshape(L, S5_SCAN_GROUPS, S5_GROUP_TILES, *a.shape[2:])
        both = jnp.concatenate([split(re), split(im)], axis=2)
        return both.reshape(L, 2 * n_tiles, *re.shape[2:]).astype(BF16)

    def b_tiles(bb):
        blocks = jnp.repeat(bb.reshape(L, G * C // LANE, LANE, N), 2, axis=1)
        return jnp.where(mask_b, jnp.tile(blocks, (1, 1, 1, gpt)), 0.0)

    def c_tiles(cc):
        rows = jnp.transpose(cc, (0, 1, 3, 2)).reshape(L, n_tiles, PAIR, C)
        return jnp.where(mask_c, jnp.tile(rows, (1, 1, 1, PAIR // C)), 0.0)

    w_b = group_major(b_tiles(bb_re), b_tiles(bb_im))
    w_c = group_major(c_tiles(pr["s5_c_re"]), -c_tiles(pr["s5_c_im"]))
    first = lambda a: a.reshape(L, G, C, N)[:, :, 0, :].reshape(L, S5_SCAN_GROUPS, S5_GROUP_TILES, 1, PAIR)
    a_rep = jnp.broadcast_to(jnp.concatenate([first(a_re), first(a_im)], axis=2),
                             (L, S5_SCAN_GROUPS, 2 * S5_GROUP_TILES, n_batch, PAIR))
    return dict(a=a_rep, w_b=w_b, w_c=w_c,
                d=pr["s5_d"][:, None, :], w_glu=pr["s5_w_glu"].astype(BF16), b_glu=pr["s5_b_glu"][:, None, :])


def _s5_kernel(u_ref, a_ref, wb_ref, wc_ref, d_ref, wg_ref, bg_ref, o_ref, st_ref, us_ref, zs_ref, ub_ref, uf_ref,
               y_ref, bu0_ref, bu1_ref, x0_ref, x1_ref, *, nb, ts, n_sub):
    @pl.when(pl.program_id(0) == 0)
    def _():
        st_ref[...] = jnp.zeros_like(st_ref)

    pitch = ts + S5_ROW_PAD
    nk = S5_WIDTH // LANE
    gt = S5_GROUP_TILES
    n_iter = ts // S5_SCAN_UNROLL
    tiles_per_iter = 2 * gt // n_iter
    steps_per_tile = S5_SCAN_UNROLL // tiles_per_iter
    bu = (bu0_ref, bu1_ref)
    xs = (x0_ref, x1_ref)
    everything = slice(None)

    def stage_u(c):
        for b in range(nb):
            for k in range(nk):
                us_ref[k, b * pitch:b * pitch + ts, :] = u_ref[b, c * ts:(c + 1) * ts, k * LANE:(k + 1) * LANE]
        u = jnp.concatenate(
            [jnp.concatenate([us_ref[k, pl.ds(t, nb, stride=pitch), :] for k in range(nk)], axis=1) for t in range(ts)],
            axis=0)
        uf_ref[c % 2] = u
        for k in range(nk):
            ub_ref[k] = u[:, k * LANE:(k + 1) * LANE].astype(BF16)

    def b_item(g, k, rows):
        cb = (gt * g) // 2 + (k % gt) // 2
        bu[g][k, rows, :] = _dot(ub_ref[cb, rows, :], wb_ref[2 * gt * g + k])

    def c_item(g, k, rows):
        y_ref[g, rows, :] += _dot(xs[g][k, rows, :].astype(BF16), wc_ref[2 * gt * g + k])

    def scan_phase(g, b_target, c_source):
        ar = a_ref[g, :gt]
        ai = a_ref[g, gt:]
        if c_source is not None:
            y_ref[c_source] = jnp.zeros(y_ref.shape[1:], F32)

        def body(i, carry):
            xr, xi = carry
            for j in range(tiles_per_iter):
                k = i * tiles_per_iter + j
                if b_target is not None:
                    b_item(b_target, k, everything)
                if c_source is not None:
                    c_item(c_source, k, everything)
                for tt in range(steps_per_tile):
                    t = i * S5_SCAN_UNROLL + j * steps_per_tile + tt
                    r = pl.ds(t * nb, nb) if n_iter == 1 else pl.ds(pl.multiple_of(t * nb, nb), nb)
                    nr = ar * xr - ai * xi + bu[g][:gt, r, :]
                    ni = ar * xi + ai * xr + bu[g][gt:, r, :]
                    xs[g][:gt, r, :] = nr
                    xs[g][gt:, r, :] = ni
                    xr, xi = nr, ni
            return xr, xi

        carry = (st_ref[g, :gt], st_ref[g, gt:])
        xr, xi = body(0, carry) if n_iter == 1 else lax.fori_loop(0, n_iter, body, carry)
        st_ref[g, :gt] = xr
        st_ref[g, gt:] = xi

    def epilogue(c):
        y = jnp.concatenate([y_ref[g] for g in range(S5_SCAN_GROUPS)], axis=1) + d_ref[...] * uf_ref[c % 2]
        z = jax.nn.gelu(y, approximate=True)
        z = z * jax.nn.sigmoid(_dot(z.astype(BF16), wg_ref[...]) + bg_ref[...])
        for t in range(ts):
            for k in range(nk):
                zs_ref[k, pl.ds(t, nb, stride=pitch), :] = z[t * nb:(t + 1) * nb, k * LANE:(k + 1) * LANE]
        for b in range(nb):
            rows = jnp.concatenate([zs_ref[k, b * pitch:b * pitch + ts, :] for k in range(nk)], axis=1)
            o_ref[b, c * ts:(c + 1) * ts, :] = rows.astype(o_ref.dtype)

    stage_u(0)
    for k in range(2 * gt):
        b_item(0, k, everything)
    for c in range(n_sub):
        scan_phase(0, 1, 1 if c > 0 else None)
        if c > 0:
            epilogue(c - 1)
        if c + 1 < n_sub:
            stage_u(c + 1)
        scan_phase(1, 0 if c + 1 < n_sub else None, 0)
    y_ref[1] = jnp.zeros(y_ref.shape[1:], F32)
    for k in range(2 * gt):
        c_item(1, k, everything)
    epilogue(n_sub - 1)


def _s5_call(u, p, l, ts, n_sub):
    B, S, _ = u.shape
    blk = ts * n_sub
    rows = ts * B
    args = [u, p["a"], p["w_b"], p["w_c"], p["d"], p["w_glu"], p["b_glu"]]
    kern = functools.partial(_s5_kernel, nb=B, ts=ts, n_sub=n_sub)
    once = lambda a: pl.BlockSpec((None,) + a.shape[1:], lambda i: (l,) + (0,) * (a.ndim - 1),
                                  pipeline_mode=pl.Buffered(1))
    slab_buf = pltpu.VMEM((S5_WIDTH // LANE, B * (ts + S5_ROW_PAD), LANE), F32)
    tile_buf = pltpu.VMEM((2 * S5_GROUP_TILES, rows, PAIR), F32)
    return pl.pallas_call(
        kern,
        grid=(S // blk,),
        in_specs=[pl.BlockSpec((B, blk, S5_WIDTH), lambda i: (0, i, 0))] + [once(a) for a in args[1:]],
        out_specs=pl.BlockSpec((B, blk, S5_WIDTH), lambda i: (0, i, 0)),
        out_shape=jax.ShapeDtypeStruct((B, S, S5_WIDTH), BF16),
        scratch_shapes=[pltpu.VMEM((S5_SCAN_GROUPS, 2 * S5_GROUP_TILES, B, PAIR), F32), slab_buf, slab_buf,
                        pltpu.VMEM((S5_WIDTH // LANE, rows, LANE), BF16), pltpu.VMEM((2, rows, S5_WIDTH), F32),
                        pltpu.VMEM((S5_SCAN_GROUPS, rows, PAIR), F32), tile_buf, tile_buf, tile_buf, tile_buf],
        compiler_params=_cparams(("arbitrary",)),
        name="s5",
    )(*args)


def _merge_kernel(x_ref, ym_ref, yf_ref, ys_ref, ng_ref, wgm_ref, wo_ref, wout_ref, o_ref):
    x = x_ref[0]
    h = _rms(x, ng_ref[...]).astype(BF16)
    merged = None
    for b, y_ref in enumerate((ym_ref, yf_ref, ys_ref)):
        g = _dot_nt(h, wgm_ref[b * MLA_WIDTH:(b + 1) * MLA_WIDTH, :])
        gated = (y_ref[0].astype(F32) * (g * jax.nn.sigmoid(g))).astype(BF16)
        o = _dot(gated, wo_ref[b * MLA_WIDTH:(b + 1) * MLA_WIDTH, :])
        m = _dot_nt(h, wgm_ref[BRANCH_WIDTH + b * D_MODEL:BRANCH_WIDTH + (b + 1) * D_MODEL, :])
        term = jax.nn.sigmoid(m) * o
        merged = term if merged is None else merged + term
    o_ref[0] = x + _dot(merged.astype(BF16), wout_ref[...])


def _merge_weights(pr, wgm):
    return [pr["norm_g"][:, None, :], wgm, pr["w_branch_out"].astype(BF16), pr["w_out"].astype(BF16)]


def _merge_call(x, y_mla, y_fox, y_s5, w, l, tm):
    B, S, _ = x.shape
    tok = lambda width: pl.BlockSpec((1, tm, width), lambda b, s: (b, s, 0))
    args = [x, y_mla, y_fox, y_s5] + w
    return pl.pallas_call(
        _merge_kernel,
        grid=(B, S // tm),
        in_specs=([tok(D_MODEL), tok(MLA_WIDTH), tok(FOX_WIDTH), tok(S5_WIDTH)]
                  + [_layer(a, l, single_buffer=True) for a in w]),
        out_specs=tok(D_MODEL),
        out_shape=jax.ShapeDtypeStruct(x.shape, x.dtype),
        compiler_params=_cparams(("parallel", "parallel")),
        name="merge",
    )(*args)


_LAYER_KEYS = ("norm_g", "w_in", "mla_q_a_norm", "mla_w_q_up", "mla_kv_a_norm", "mla_w_kv_up", "mla_q_norm",
               "mla_k_norm", "fox_b_f", "fox_q_norm", "fox_k_norm", "s5_lambda_re", "s5_lambda_im", "s5_log_dt",
               "s5_b_re", "s5_b_im", "s5_c_re", "s5_c_im", "s5_d", "s5_w_glu", "s5_b_glu", "w_branch_out", "w_out")


def _forward(x, positions, params):
    B, S, _ = x.shape
    tm = min(S, 512)
    tq = min(S, 512)
    ts = min(S, 64)
    depth = params["w_in"].shape[0]
    ca, sb = _rope_tables(positions)
    consts = _prep_consts(tm)
    mask_mla, mask_fox = _head_masks()
    wcat, wgm = _pack_w_in(params["w_in"])
    w_prep = _prep_weights(params, wcat)
    w_s5 = _s5_params(params, B)
    w_merge = _merge_weights(params, wgm)
    h = x
    for l in range(depth):
        qm, km, vm, qf, kf, vf, u = _prep_call(h, ca, sb, w_prep, consts, l, tm)
        y_mla = _attn_call(qm, km, vm, mask_mla, True, tq, "attn_mla")
        y_fox = _attn_call(qf, kf, vf, mask_fox, False, tq, "attn_fox")
        y_s5 = _s5_call(u, w_s5, l, ts, min(S // ts, 4))
        h = _merge_call(h, y_mla, y_fox, y_s5, w_merge, l, min(S, 2 * tm))
    return h


def kernel(x, positions, norm_g, w_in, mla_q_a_norm, mla_w_q_up, mla_kv_a_norm, mla_w_kv_up, mla_q_norm, mla_k_norm, fox_b_f, fox_q_norm, fox_k_norm, s5_lambda_re, s5_lambda_im, s5_log_dt, s5_b_re, s5_b_im, s5_c_re, s5_c_im, s5_d, s5_w_glu, s5_b_glu, w_branch_out, w_out):
    params = dict(zip(_LAYER_KEYS, (norm_g, w_in, mla_q_a_norm, mla_w_q_up, mla_kv_a_norm, mla_w_kv_up, mla_q_norm,
                                    mla_k_norm, fox_b_f, fox_q_norm, fox_k_norm, s5_lambda_re, s5_lambda_im,
                                    s5_log_dt, s5_b_re, s5_b_im, s5_c_re, s5_c_im, s5_d, s5_w_glu, s5_b_glu,
                                    w_branch_out, w_out)))
    return _forward(x, positions, params)
```

```python
import functools
import math

import numpy as np
import jax
import jax.numpy as jnp
from jax import lax
from jax.experimental import pallas as pl
from jax.experimental.pallas import tpu as pltpu

F32 = jnp.float32
BF16 = jnp.bfloat16

D_MODEL = 1024
CHUNK = 64
EPS = 1e-6

MLA_HEADS = 8
MLA_NOPE = 64
MLA_ROPE = 32
MLA_V = 64
MLA_Q_RANK = 256
MLA_KV_RANK = 128
MLA_WIDTH = MLA_HEADS * MLA_V
MLA_QK_DIM = MLA_NOPE + MLA_ROPE
ROPE_THETA = 10000.0

FOX_HEADS = 8
FOX_HEAD_DIM = 64
FOX_WIDTH = FOX_HEADS * FOX_HEAD_DIM

S5_WIDTH = 512
S5_GROUP = 16
S5_GROUPS = S5_WIDTH // S5_GROUP
S5_STATE = 64
S5_ROW_PAD = 8
S5_SCAN_GROUPS = 2
S5_GROUP_TILES = 4
S5_SCAN_UNROLL = 64

BRANCH_WIDTH = MLA_WIDTH + FOX_WIDTH + S5_WIDTH

LANE = 128
HEAD_PAD = 128
PAIR = 2 * HEAD_PAD
N_PAIRS = 4
N_SPLIT = 3
LOG2E = math.log2(math.e)

_OFF = np.cumsum([0, MLA_Q_RANK, MLA_KV_RANK, MLA_ROPE, FOX_WIDTH, FOX_WIDTH, FOX_WIDTH, FOX_HEADS,
                  S5_WIDTH, MLA_WIDTH, FOX_WIDTH, S5_WIDTH, D_MODEL, D_MODEL, D_MODEL]).tolist()
(O_CQ, O_CKV, O_KPE, O_FQ, O_FK, O_FV, O_FF, O_S5U, O_GATE, _o1, _o2, O_MERGE, _o3, _o4, O_END) = _OFF

C_CQ = 0
C_CKV = C_CQ + MLA_Q_RANK
C_KPA = C_CKV + MLA_KV_RANK
C_KPB = C_KPA + LANE
C_FQ = C_KPB + LANE
C_FK = C_FQ + FOX_WIDTH
C_FV = C_FK + FOX_WIDTH
C_FF = C_FV + FOX_WIDTH
C_S5 = C_FF + LANE
C_END = C_S5 + S5_WIDTH

VMEM_LIMIT = 56 * 1024 * 1024


def _cparams(sem):
    return pltpu.CompilerParams(dimension_semantics=sem, vmem_limit_bytes=VMEM_LIMIT)


def _full(shape):
    n = len(shape)
    return pl.BlockSpec(shape, lambda *_: (0,) * n)


def _rms(x, g):
    return x * lax.rsqrt(jnp.mean(x * x, axis=-1, keepdims=True) + EPS) * g


def _dot(a, b):
    return jnp.dot(a, b, preferred_element_type=F32)


def _dot_nt(a, b):
    return lax.dot_general(a, b, (((1,), (1,)), ((), ())), preferred_element_type=F32)


def _store_v_with_ones(v_ref, v):
    lane = lax.broadcasted_iota(jnp.int32, (v.shape[0], LANE), 1)
    for p in range(N_PAIRS):
        vp = v[:, p * LANE:(p + 1) * LANE]
        v_ref[0, :, p * PAIR:p * PAIR + LANE] = jnp.where(lane < LANE // 2, vp, 1.0).astype(v_ref.dtype)
        v_ref[0, :, p * PAIR + LANE:(p + 1) * PAIR] = jnp.where(lane < LANE // 2, 1.0, vp).astype(v_ref.dtype)


def _rope_kernel(pos_ref, inv_ref, cos_ref, sin_ref):
    ang = pos_ref[...].astype(F32) * inv_ref[...]
    cos_ref[...] = jnp.cos(ang)
    sin_ref[...] = jnp.sin(ang)


def _rope_tables(positions):
    B, S = positions.shape
    half = MLA_ROPE // 2
    per_row = LANE // half
    inv = 1.0 / (ROPE_THETA ** (jnp.arange(0, MLA_ROPE, 2, dtype=F32) / MLA_ROPE))
    pos = jnp.repeat(positions.reshape(B * S // per_row, per_row), half, axis=1)
    rows = pos.shape[0]
    tr = min(rows, 1024)
    blk = pl.BlockSpec((tr, LANE), lambda r: (r, 0))
    cos, sin = pl.pallas_call(
        _rope_kernel,
        grid=(rows // tr,),
        in_specs=[blk, _full((1, LANE))],
        out_specs=[blk, blk],
        out_shape=[jax.ShapeDtypeStruct((rows, LANE), F32)] * 2,
        compiler_params=_cparams(("parallel",)),
        name="rope_tables",
    )(pos, jnp.tile(inv, per_row)[None])
    cos = cos.reshape(B, S, half)
    sin = sin.reshape(B, S, half)
    lead = jnp.ones((B, S, MLA_NOPE), F32)
    tail = jnp.zeros((B, S, LANE - MLA_NOPE - MLA_ROPE), F32)
    ca = jnp.concatenate([lead, cos, cos, tail], axis=-1)
    sb = jnp.concatenate([jnp.zeros_like(lead), -sin, sin, tail], axis=-1)
    return ca, sb


def _split3(x):
    hi = x.astype(BF16)
    r1 = x - hi.astype(F32)
    mid = r1.astype(BF16)
    return hi, mid, (r1 - mid.astype(F32)).astype(BF16)


def _fox_qk_norm(x, bd, gain):
    sq = (x * x).astype(BF16)
    ss = jnp.concatenate([_dot(sq[:, c:c + PAIR], bd) for c in range(0, FOX_WIDTH, PAIR)], axis=1)
    return (x * lax.rsqrt(ss * (1.0 / FOX_HEAD_DIM) + EPS) * gain).astype(BF16)


def _prep_kernel(x_ref, ca_ref, sb_ref, ng_ref, wcat_ref, qan_ref, wqa_ref, wqb_ref, kvn_ref, wk_ref, wv_ref,
                 gqm_ref, gkm_ref, bf_ref, gqf_ref, gkf_ref, bd_ref, tri_ref,
                 eq_ref, ek_ref, oq_ref, ok_ref,
                 qm_ref, km_ref, vm_ref, qf_ref, kf_ref, vf_ref, u_ref, carry_ref):
    @pl.when(pl.program_id(1) == 0)
    def _():
        carry_ref[...] = jnp.zeros_like(carry_ref)

    x = x_ref[0]
    h = _rms(x, ng_ref[...]).astype(BF16)
    proj = _dot_nt(h, wcat_ref[...])
    ca = ca_ref[0]
    sb = sb_ref[0]

    cqn = _rms(proj[:, C_CQ:C_CQ + MLA_Q_RANK], qan_ref[...]).astype(BF16)
    qa = _dot(cqn, wqa_ref[...])
    qb = _dot(cqn, wqb_ref[...])
    ckvn = _rms(proj[:, C_CKV:C_CKV + MLA_KV_RANK], kvn_ref[...]).astype(BF16)
    kc = _dot(ckvn, wk_ref[...])
    _store_v_with_ones(vm_ref, _dot(ckvn, wv_ref[...]))
    kpe = proj[:, C_KPA:C_KPA + LANE] * ca + proj[:, C_KPB:C_KPB + LANE] * sb
    inv_d = 1.0 / MLA_QK_DIM
    for hd in range(MLA_HEADS):
        sl = slice(hd * HEAD_PAD, (hd + 1) * HEAD_PAD)
        qh = qa[:, sl] * ca + qb[:, sl] * sb
        ss = jnp.sum(qh * qh, axis=-1, keepdims=True)
        qm_ref[0, :, sl] = (qh * lax.rsqrt(ss * inv_d + EPS) * gqm_ref[...]).astype(BF16)
        kh = kc[:, sl] + kpe
        ss = jnp.sum(kh * kh, axis=-1, keepdims=True)
        km_ref[0, :, sl] = (kh * lax.rsqrt(ss * inv_d + EPS) * gkm_ref[...]).astype(BF16)

    fqn = _fox_qk_norm(proj[:, C_FQ:C_FQ + FOX_WIDTH], bd_ref[...], gqf_ref[...])
    fkn = _fox_qk_norm(proj[:, C_FK:C_FK + FOX_WIDTH], bd_ref[...], gkf_ref[...])
    _store_v_with_ones(vf_ref, proj[:, C_FV:C_FV + FOX_WIDTH])

    z = proj[:, C_FF:C_FF + LANE] + bf_ref[...]
    log_f = jnp.minimum(z, 0.0) - jnp.log1p(jnp.exp(-jnp.abs(z)))
    sums = _dot(tri_ref[...], jnp.concatenate(_split3(log_f), axis=1))
    cum = sums[:, :LANE] + sums[:, LANE:2 * LANE] + sums[:, 2 * LANE:] + carry_ref[...]
    carry_ref[...] = cum[cum.shape[0] - 1:, :]
    hi, mid, lo = _split3(cum * LOG2E)
    lane = lax.broadcasted_iota(jnp.int32, hi.shape, 1)
    pieces = jnp.where(lane < FOX_HEADS, hi, jnp.where(lane < 2 * FOX_HEADS, mid, lo))
    auxq = (_dot(pieces, eq_ref[...]) + oq_ref[...]).astype(BF16)
    auxk = (_dot(pieces, ek_ref[...]) + ok_ref[...]).astype(BF16)
    for p in range(N_PAIRS):
        src = slice(p * LANE, (p + 1) * LANE)
        qf_ref[0, :, p * PAIR:p * PAIR + LANE] = fqn[:, src]
        qf_ref[0, :, p * PAIR + LANE:(p + 1) * PAIR] = auxq[:, src]
        kf_ref[0, :, p * PAIR:p * PAIR + LANE] = fkn[:, src]
        kf_ref[0, :, p * PAIR + LANE:(p + 1) * PAIR] = auxk[:, src]

    u_ref[0] = proj[:, C_S5:C_S5 + S5_WIDTH]


def _layer(arr, l, single_buffer=False):
    n = arr.ndim - 1
    mode = dict(pipeline_mode=pl.Buffered(1)) if single_buffer else {}
    return pl.BlockSpec((None,) + arr.shape[1:], lambda *_: (l,) + (0,) * n, **mode)


def _pack_w_in(w_in):
    wt = jnp.transpose(w_in, (0, 2, 1))
    L = wt.shape[0]
    half = MLA_ROPE // 2
    zeros = lambda n: jnp.zeros((L, n, D_MODEL), F32)
    kpe_lo = wt[:, O_KPE:O_KPE + half]
    kpe_hi = wt[:, O_KPE + half:O_KPE + MLA_ROPE]
    tail = zeros(LANE - MLA_NOPE - MLA_ROPE)
    ff = wt[:, O_FF:O_FF + FOX_HEADS]
    pieces = [wt[:, O_CQ:O_CQ + MLA_Q_RANK + MLA_KV_RANK],
              zeros(MLA_NOPE), kpe_lo, kpe_hi, tail,
              zeros(MLA_NOPE), kpe_hi, kpe_lo, tail,
              wt[:, O_FQ:O_FQ + 3 * FOX_WIDTH],
              ff, ff, ff, zeros(LANE - N_SPLIT * FOX_HEADS),
              wt[:, O_S5U:O_S5U + S5_WIDTH]]
    return jnp.concatenate(pieces, axis=1).astype(BF16), wt[:, O_GATE:O_END].astype(BF16)


def _prep_weights(pr, wcat):
    L = wcat.shape[0]
    row = lambda a: a[:, None, :]
    half = MLA_ROPE // 2

    wq = pr["mla_w_q_up"].reshape(L, MLA_Q_RANK, MLA_HEADS, MLA_QK_DIM)
    nope, pe = wq[..., :MLA_NOPE], wq[..., MLA_NOPE:]
    pe_sw = jnp.concatenate([pe[..., half:], pe[..., :half]], axis=3)
    z32 = jnp.zeros((L, MLA_Q_RANK, MLA_HEADS, HEAD_PAD - MLA_QK_DIM), F32)
    wqa = jnp.concatenate([nope, pe, z32], axis=3).reshape(L, MLA_Q_RANK, MLA_HEADS * HEAD_PAD).astype(BF16)
    wqb = jnp.concatenate([jnp.zeros_like(nope), pe_sw, z32], axis=3).reshape(L, MLA_Q_RANK, -1).astype(BF16)

    wkv = pr["mla_w_kv_up"].reshape(L, MLA_KV_RANK, MLA_HEADS, MLA_NOPE + MLA_V)
    wk = jnp.concatenate([wkv[..., :MLA_NOPE], jnp.zeros((L, MLA_KV_RANK, MLA_HEADS, HEAD_PAD - MLA_NOPE), F32)],
                         axis=3).reshape(L, MLA_KV_RANK, -1).astype(BF16)
    wv = wkv[..., MLA_NOPE:].reshape(L, MLA_KV_RANK, MLA_WIDTH).astype(BF16)

    pad = jnp.zeros((L, HEAD_PAD - MLA_QK_DIM), F32)
    gqm = row(jnp.concatenate([pr["mla_q_norm"], pad], axis=1) * (LOG2E / math.sqrt(MLA_QK_DIM)))
    gkm = row(jnp.concatenate([pr["mla_k_norm"], pad], axis=1))
    bf = row(jnp.concatenate([pr["fox_b_f"]] * N_SPLIT + [jnp.zeros((L, LANE - N_SPLIT * FOX_HEADS), F32)], axis=1))
    gqf = row(jnp.tile(pr["fox_q_norm"], (1, FOX_HEADS)) * (LOG2E / math.sqrt(FOX_HEAD_DIM)))
    gkf = row(jnp.tile(pr["fox_k_norm"], (1, FOX_HEADS)))
    return dict(ng=row(pr["norm_g"]), wcat=wcat, qan=row(pr["mla_q_a_norm"]), wqa=wqa, wqb=wqb,
                kvn=row(pr["mla_kv_a_norm"]), wk=wk, wv=wv, gqm=gqm, gkm=gkm, bf=bf, gqf=gqf, gkf=gkf)


def _prep_consts(tm):
    bd = np.kron(np.eye(PAIR // FOX_HEAD_DIM, dtype=np.float32), np.ones((FOX_HEAD_DIM, FOX_HEAD_DIM), np.float32))
    tri = np.tril(np.ones((tm, tm), np.float32))
    eq = np.zeros((LANE, N_PAIRS * LANE), np.float32)
    ek = np.zeros_like(eq)
    oq = np.zeros((1, N_PAIRS * LANE), np.float32)
    ok = np.zeros_like(oq)
    for hd in range(FOX_HEADS):
        p, a = divmod(hd, 2)
        base = p * LANE + 2 * N_SPLIT * a
        for i in range(N_SPLIT):
            ek[i * FOX_HEADS + hd, base + i] = -1.0
            oq[0, base + i] = 1.0
            eq[i * FOX_HEADS + hd, base + N_SPLIT + i] = 1.0
            ok[0, base + N_SPLIT + i] = 1.0
    return dict(bd=jnp.asarray(bd, BF16), tri=jnp.asarray(tri, BF16),
                eq=jnp.asarray(eq, BF16), ek=jnp.asarray(ek, BF16), oq=jnp.asarray(oq), ok=jnp.asarray(ok))


def _prep_call(x, ca, sb, w, c, l, tm):
    B, S, _ = x.shape
    tok = lambda width: pl.BlockSpec((1, tm, width), lambda b, s: (b, s, 0))
    stacked = [w["ng"], w["wcat"], w["qan"], w["wqa"], w["wqb"], w["kvn"], w["wk"], w["wv"],
               w["gqm"], w["gkm"], w["bf"], w["gqf"], w["gkf"]]
    consts = [c["bd"], c["tri"], c["eq"], c["ek"], c["oq"], c["ok"]]
    args = [x, ca, sb] + stacked + consts
    in_specs = ([tok(D_MODEL), tok(LANE), tok(LANE)] + [_layer(a, l) for a in stacked]
                + [_full(a.shape) for a in consts])
    widths = [MLA_HEADS * HEAD_PAD, MLA_HEADS * HEAD_PAD, N_PAIRS * PAIR, N_PAIRS * PAIR, N_PAIRS * PAIR, N_PAIRS * PAIR]
    out_shape = [jax.ShapeDtypeStruct((B, S, wd), BF16) for wd in widths]
    out_shape.append(jax.ShapeDtypeStruct((B, S, S5_WIDTH), F32))
    return pl.pallas_call(
        _prep_kernel,
        grid=(B, S // tm),
        in_specs=in_specs,
        out_specs=[tok(wd) for wd in widths] + [tok(S5_WIDTH)],
        out_shape=out_shape,
        scratch_shapes=[pltpu.VMEM((1, LANE), F32)],
        compiler_params=_cparams(("parallel", "arbitrary")),
        name="prep",
    )(*args)


def _diag_tile(tq):
    return tq // 2 if tq >= 2 * PAIR else tq


def _attn_kernel(q_ref, k_ref, v_ref, hm_ref, o_ref, qm_ref, sa_ref, sb_ref, bias_ref, m_ref, acc_ref, *,
                 tq, chunk_causal):
    S = q_ref.shape[1]
    nq = S // tq
    td = _diag_tile(tq)
    steps = [(qi, j) for qi in range(nq) for j in range(qi + 1)]
    row = lax.broadcasted_iota(jnp.int32, (td, td), 0)
    col = lax.broadcasted_iota(jnp.int32, (td, td), 1)
    shift = int(math.log2(CHUNK)) if chunk_causal else 0
    bias_ref[...] = jnp.where((col >> shift) <= (row >> shift), 0.0, -jnp.inf)
    for a in range(2):
        qm_ref[a] = q_ref[0] * hm_ref[a:a + 1, :]

    def row_blocks(qi, j):
        if j < qi:
            return [(0, tq, tq)]
        return [(r * td, td, (r + 1) * td) for r in range(tq // td)]

    def qk(qi, j, s_ref):
        for r0, nr, nk in row_blocks(qi, j):
            k = k_ref[0, j * tq:j * tq + nk, :]
            for a in range(2):
                q = qm_ref[a, qi * tq + r0:qi * tq + r0 + nr, :]
                s_ref[a, r0:r0 + nr, :nk] = lax.dot_general(q, k, (((1,), (1,)), ((), ())),
                                                            preferred_element_type=F32)

    qk(0, 0, sa_ref)
    bufs = (sa_ref, sb_ref)
    for t, (qi, j) in enumerate(steps):
        s_cur, s_next = bufs[t % 2], bufs[(t + 1) % 2]
        if t + 1 < len(steps):
            qk(*steps[t + 1], s_next)
        for r0, nr, nk in row_blocks(qi, j):
            rows = slice(r0, r0 + nr)
            lane = lax.broadcasted_iota(jnp.int32, (nr, LANE), 1)
            outs = []
            for a in range(2):
                s = s_cur[a, rows, :nk]
                if j == qi:
                    tail = s[:, nk - td:] + bias_ref[...]
                    s = tail if nk == td else jnp.concatenate([s[:, :nk - td], tail], axis=1)
                v = v_ref[0, j * tq:j * tq + nk, a * LANE:(a + 1) * LANE]
                m_cur = jnp.max(s, axis=-1, keepdims=True)
                if j == 0:
                    m_new = jnp.broadcast_to(m_cur, (nr, LANE))
                    p = jnp.exp2(s - m_cur)
                    acc = _dot(p.astype(BF16), v)
                else:
                    m_prev = m_ref[a, rows]
                    m_new = jnp.maximum(m_prev, m_cur)
                    alpha = jnp.exp2(m_prev - m_new)
                    p = jnp.exp2(s - jnp.concatenate([m_new] * (nk // LANE), axis=1))
                    acc = alpha * acc_ref[a, rows] + _dot(p.astype(BF16), v)
                if j == qi:
                    outs.append(acc / pltpu.roll(acc, LANE // 2, axis=1))
                else:
                    acc_ref[a, rows] = acc
                    m_ref[a, rows] = m_new
            if j == qi:
                out = jnp.where(lane < LANE // 2, outs[0], outs[1]).astype(o_ref.dtype)
                o_ref[0, qi * tq + r0:qi * tq + r0 + nr, :] = out


def _attn_call(q, k, v, head_mask, chunk_causal, tq, name):
    B, S, _ = q.shape
    kern = functools.partial(_attn_kernel, tq=tq, chunk_causal=chunk_causal)
    td = _diag_tile(tq)
    return pl.pallas_call(
        kern,
        grid=(B, N_PAIRS),
        in_specs=[pl.BlockSpec((1, S, PAIR), lambda b, p: (b, 0, p)),
                  pl.BlockSpec((1, S, PAIR), lambda b, p: (b, 0, p)),
                  pl.BlockSpec((1, S, PAIR), lambda b, p: (b, 0, p)),
                  _full((2, PAIR))],
        out_specs=pl.BlockSpec((1, S, LANE), lambda b, p: (b, 0, p)),
        out_shape=jax.ShapeDtypeStruct((B, S, N_PAIRS * LANE), BF16),
        scratch_shapes=[pltpu.VMEM((2, S, PAIR), BF16), pltpu.VMEM((2, tq, tq), F32), pltpu.VMEM((2, tq, tq), F32),
                        pltpu.VMEM((td, td), F32), pltpu.VMEM((2, tq, LANE), F32), pltpu.VMEM((2, tq, LANE), F32)],
        compiler_params=_cparams(("parallel", "parallel")),
        name=name,
    )(q, k, v, head_mask)


def _head_masks():
    mla = np.zeros((2, PAIR), np.float32)
    fox = np.zeros((2, PAIR), np.float32)
    for a in range(2):
        mla[a, a * HEAD_PAD:(a + 1) * HEAD_PAD] = 1.0
        fox[a, a * FOX_HEAD_DIM:(a + 1) * FOX_HEAD_DIM] = 1.0
        fox[a, LANE + 2 * N_SPLIT * a:LANE + 2 * N_SPLIT * (a + 1)] = 1.0
    return jnp.asarray(mla, BF16), jnp.asarray(fox, BF16)


def _s5_param_kernel(lr_ref, li_ref, ldt_ref, br_ref, bi_ref, are_ref, aim_ref, bbr_ref, bbi_ref):
    lr = lr_ref[...]
    li = li_ref[...]
    dt = jnp.exp(ldt_ref[...])
    mag = jnp.exp(lr * dt)
    a_re = mag * jnp.cos(li * dt)
    a_im = mag * jnp.sin(li * dt)
    den = lr * lr + li * li
    f_re = ((a_re - 1.0) * lr + a_im * li) / den
    f_im = (a_im * lr - (a_re - 1.0) * li) / den
    br = br_ref[...]
    bi = bi_ref[...]
    are_ref[...] = a_re
    aim_ref[...] = a_im
    bbr_ref[...] = f_re * br - f_im * bi
    bbi_ref[...] = f_re * bi + f_im * br


def _s5_params(pr, n_batch):
    G, N, C = S5_GROUPS, S5_STATE, S5_GROUP
    L = pr["s5_lambda_re"].shape[0]
    rep = lambda a: jnp.repeat(a.reshape(L * G, N), C, axis=0)
    ldt = jnp.broadcast_to(pr["s5_log_dt"][:, :, None], (L, G, N))
    tr = lambda b: jnp.transpose(b, (0, 1, 3, 2)).reshape(L * G * C, N)
    shp = jax.ShapeDtypeStruct((L * G * C, N), F32)
    a_re, a_im, bb_re, bb_im = pl.pallas_call(
        _s5_param_kernel, out_shape=[shp] * 4, name="s5_params",
    )(rep(pr["s5_lambda_re"]), rep(pr["s5_lambda_im"]), rep(ldt), tr(pr["s5_b_re"]), tr(pr["s5_b_im"]))
    n_tiles = G * N // PAIR
    gpt = PAIR // N
    t_idx = np.arange(n_tiles)[:, None, None]
    mask_b = jnp.asarray(np.arange(LANE)[None, :, None] // C == gpt * (t_idx % 2) + np.arange(PAIR)[None, None, :] // N)
    mask_c = jnp.asarray(np.arange(PAIR)[None, None, :] // C
                         == gpt * (t_idx % S5_GROUP_TILES) + np.arange(PAIR)[None, :, None] // N)

    def group_major(re, im):
        split = lambda a: a.reshape(L, S5_SCAN_GROUPS, S5_GROUP_TILES, *a.shape[2:])
        both = jnp.concatenate([split(re), split(im)], axis=2)
        return both.reshape(L, 2 * n_tiles, *re.shape[2:]).astype(BF16)

    def b_tiles(bb):
        blocks = jnp.repeat(bb.reshape(L, G * C // LANE, LANE, N), 2, axis=1)
        return jnp.where(mask_b, jnp.tile(blocks, (1, 1, 1, gpt)), 0.0)

    def c_tiles(cc):
        rows = jnp.transpose(cc, (0, 1, 3, 2)).reshape(L, n_tiles, PAIR, C)
        return jnp.where(mask_c, jnp.tile(rows, (1, 1, 1, PAIR // C)), 0.0)

    w_b = group_major(b_tiles(bb_re), b_tiles(bb_im))
    w_c = group_major(c_tiles(pr["s5_c_re"]), -c_tiles(pr["s5_c_im"]))
    first = lambda a: a.reshape(L, G, C, N)[:, :, 0, :].reshape(L, S5_SCAN_GROUPS, S5_GROUP_TILES, 1, PAIR)
    a_rep = jnp.broadcast_to(jnp.concatenate([first(a_re), first(a_im)], axis=2),
                             (L, S5_SCAN_GROUPS, 2 * S5_GROUP_TILES, n_batch, PAIR))
    return dict(a=a_rep, w_b=w_b, w_c=w_c,
                d=pr["s5_d"][:, None, :], w_glu=pr["s5_w_glu"].astype(BF16), b_glu=pr["s5_b_glu"][:, None, :])


def _s5_kernel(u_ref, a_ref, wb_ref, wc_ref, d_ref, wg_ref, bg_ref, o_ref, st_ref, us_ref, zs_ref, ub_ref, uf_ref,
               y_ref, bu0_ref, bu1_ref, x0_ref, x1_ref, *, nb, ts, n_sub):
    @pl.when(pl.program_id(0) == 0)
    def _():
        st_ref[...] = jnp.zeros_like(st_ref)

    pitch = ts + S5_ROW_PAD
    nk = S5_WIDTH // LANE
    gt = S5_GROUP_TILES
    n_iter = ts // S5_SCAN_UNROLL
    tiles_per_iter = 2 * gt // n_iter
    steps_per_tile = S5_SCAN_UNROLL // tiles_per_iter
    bu = (bu0_ref, bu1_ref)
    xs = (x0_ref, x1_ref)
    everything = slice(None)

    def stage_u(c):
        for b in range(nb):
            for k in range(nk):
                us_ref[k, b * pitch:b * pitch + ts, :] = u_ref[b, c * ts:(c + 1) * ts, k * LANE:(k + 1) * LANE]
        u = jnp.concatenate(
            [jnp.concatenate([us_ref[k, pl.ds(t, nb, stride=pitch), :] for k in range(nk)], axis=1) for t in range(ts)],
            axis=0)
        uf_ref[c % 2] = u
        for k in range(nk):
            ub_ref[k] = u[:, k * LANE:(k + 1) * LANE].astype(BF16)

    def b_item(g, k, rows):
        cb = (gt * g) // 2 + (k % gt) // 2
        bu[g][k, rows, :] = _dot(ub_ref[cb, rows, :], wb_ref[2 * gt * g + k])

    def c_item(g, k, rows):
        y_ref[g, rows, :] += _dot(xs[g][k, rows, :].astype(BF16), wc_ref[2 * gt * g + k])

    def scan_phase(g, b_target, c_source):
        ar = a_ref[g, :gt]
        ai = a_ref[g, gt:]
        if c_source is not None:
            y_ref[c_source] = jnp.zeros(y_ref.shape[1:], F32)

        def body(i, carry):
            xr, xi = carry
            for j in range(tiles_per_iter):
                k = i * tiles_per_iter + j
                if b_target is not None:
                    b_item(b_target, k, everything)
                if c_source is not None:
                    c_item(c_source, k, everything)
                for tt in range(steps_per_tile):
                    t = i * S5_SCAN_UNROLL + j * steps_per_tile + tt
                    r = pl.ds(t * nb, nb) if n_iter == 1 else pl.ds(pl.multiple_of(t * nb, nb), nb)
                    nr = ar * xr - ai * xi + bu[g][:gt, r, :]
                    ni = ar * xi + ai * xr + bu[g][gt:, r, :]
                    xs[g][:gt, r, :] = nr
                    xs[g][gt:, r, :] = ni
                    xr, xi = nr, ni
            return xr, xi

        carry = (st_ref[g, :gt], st_ref[g, gt:])
        xr, xi = body(0, carry) if n_iter == 1 else lax.fori_loop(0, n_iter, body, carry)
        st_ref[g, :gt] = xr
        st_ref[g, gt:] = xi

    def epilogue(c):
        y = jnp.concatenate([y_ref[g] for g in range(S5_SCAN_GROUPS)], axis=1) + d_ref[...] * uf_ref[c % 2]
        z = jax.nn.gelu(y, approximate=True)
        z = z * jax.nn.sigmoid(_dot(z.astype(BF16), wg_ref[...]) + bg_ref[...])
        for t in range(ts):
            for k in range(nk):
                zs_ref[k, pl.ds(t, nb, stride=pitch), :] = z[t * nb:(t + 1) * nb, k * LANE:(k + 1) * LANE]
        for b in range(nb):
            rows = jnp.concatenate([zs_ref[k, b * pitch:b * pitch + ts, :] for k in range(nk)], axis=1)
            o_ref[b, c * ts:(c + 1) * ts, :] = rows.astype(o_ref.dtype)

    stage_u(0)
    for k in range(2 * gt):
        b_item(0, k, everything)
    for c in range(n_sub):
        scan_phase(0, 1, 1 if c > 0 else None)
        if c > 0:
            epilogue(c - 1)
        if c + 1 < n_sub:
            stage_u(c + 1)
        scan_phase(1, 0 if c + 1 < n_sub else None, 0)
    y_ref[1] = jnp.zeros(y_ref.shape[1:], F32)
    for k in range(2 * gt):
        c_item(1, k, everything)
    epilogue(n_sub - 1)


def _s5_call(u, p, l, ts, n_sub):
    B, S, _ = u.shape
    blk = ts * n_sub
    rows = ts * B
    args = [u, p["a"], p["w_b"], p["w_c"], p["d"], p["w_glu"], p["b_glu"]]
    kern = functools.partial(_s5_kernel, nb=B, ts=ts, n_sub=n_sub)
    once = lambda a: pl.BlockSpec((None,) + a.shape[1:], lambda i: (l,) + (0,) * (a.ndim - 1),
                                  pipeline_mode=pl.Buffered(1))
    slab_buf = pltpu.VMEM((S5_WIDTH // LANE, B * (ts + S5_ROW_PAD), LANE), F32)
    tile_buf = pltpu.VMEM((2 * S5_GROUP_TILES, rows, PAIR), F32)
    return pl.pallas_call(
        kern,
        grid=(S // blk,),
        in_specs=[pl.BlockSpec((B, blk, S5_WIDTH), lambda i: (0, i, 0))] + [once(a) for a in args[1:]],
        out_specs=pl.BlockSpec((B, blk, S5_WIDTH), lambda i: (0, i, 0)),
        out_shape=jax.ShapeDtypeStruct((B, S, S5_WIDTH), BF16),
        scratch_shapes=[pltpu.VMEM((S5_SCAN_GROUPS, 2 * S5_GROUP_TILES, B, PAIR), F32), slab_buf, slab_buf,
                        pltpu.VMEM((S5_WIDTH // LANE, rows, LANE), BF16), pltpu.VMEM((2, rows, S5_WIDTH), F32),
                        pltpu.VMEM((S5_SCAN_GROUPS, rows, PAIR), F32), tile_buf, tile_buf, tile_buf, tile_buf],
        compiler_params=_cparams(("arbitrary",)),
        name="s5",
    )(*args)


def _merge_kernel(x_ref, ym_ref, yf_ref, ys_ref, ng_ref, wgm_ref, wo_ref, wout_ref, o_ref):
    x = x_ref[0]
    h = _rms(x, ng_ref[...]).astype(BF16)
    merged = None
    for b, y_ref in enumerate((ym_ref, yf_ref, ys_ref)):
        g = _dot_nt(h, wgm_ref[b * MLA_WIDTH:(b + 1) * MLA_WIDTH, :])
        gated = (y_ref[0].astype(F32) * (g * jax.nn.sigmoid(g))).astype(BF16)
        o = _dot(gated, wo_ref[b * MLA_WIDTH:(b + 1) * MLA_WIDTH, :])
        m = _dot_nt(h, wgm_ref[BRANCH_WIDTH + b * D_MODEL:BRANCH_WIDTH + (b + 1) * D_MODEL, :])
        term = jax.nn.sigmoid(m) * o
        merged = term if merged is None else merged + term
    o_ref[0] = x + _dot(merged.astype(BF16), wout_ref[...])


def _merge_weights(pr, wgm):
    return [pr["norm_g"][:, None, :], wgm, pr["w_branch_out"].astype(BF16), pr["w_out"].astype(BF16)]


def _merge_call(x, y_mla, y_fox, y_s5, w, l, tm):
    B, S, _ = x.shape
    tok = lambda width: pl.BlockSpec((1, tm, width), lambda b, s: (b, s, 0))
    args = [x, y_mla, y_fox, y_s5] + w
    return pl.pallas_call(
        _merge_kernel,
        grid=(B, S // tm),
        in_specs=([tok(D_MODEL), tok(MLA_WIDTH), tok(FOX_WIDTH), tok(S5_WIDTH)]
                  + [_layer(a, l, single_buffer=True) for a in w]),
        out_specs=tok(D_MODEL),
        out_shape=jax.ShapeDtypeStruct(x.shape, x.dtype),
        compiler_params=_cparams(("parallel", "parallel")),
        name="merge",
    )(*args)


_LAYER_KEYS = ("norm_g", "w_in", "mla_q_a_norm", "mla_w_q_up", "mla_kv_a_norm", "mla_w_kv_up", "mla_q_norm",
               "mla_k_norm", "fox_b_f", "fox_q_norm", "fox_k_norm", "s5_lambda_re", "s5_lambda_im", "s5_log_dt",
               "s5_b_re", "s5_b_im", "s5_c_re", "s5_c_im", "s5_d", "s5_w_glu", "s5_b_glu", "w_branch_out", "w_out")


def _forward(x, positions, params):
    B, S, _ = x.shape
    tm = min(S, 512)
    tq = min(S, 512)
    ts = min(S, 64)
    depth = params["w_in"].shape[0]
    ca, sb = _rope_tables(positions)
    consts = _prep_consts(tm)
    mask_mla, mask_fox = _head_masks()
    wcat, wgm = _pack_w_in(params["w_in"])
    w_prep = _prep_weights(params, wcat)
    w_s5 = _s5_params(params, B)
    w_merge = _merge_weights(params, wgm)
    h = x
    for l in range(depth):
        qm, km, vm, qf, kf, vf, u = _prep_call(h, ca, sb, w_prep, consts, l, tm)
        y_mla = _attn_call(qm, km, vm, mask_mla, True, tq, "attn_mla")
        y_fox = _attn_call(qf, kf, vf, mask_fox, False, tq, "attn_fox")
        y_s5 = _s5_call(u, w_s5, l, ts, min(S // ts, 4))
        h = _merge_call(h, y_mla, y_fox, y_s5, w_merge, l, min(S, 2 * tm))
    return h


def kernel(x, positions, norm_g, w_in, mla_q_a_norm, mla_w_q_up, mla_kv_a_norm, mla_w_kv_up, mla_q_norm, mla_k_norm, fox_b_f, fox_q_norm, fox_k_norm, s5_lambda_re, s5_lambda_im, s5_log_dt, s5_b_re, s5_b_im, s5_c_re, s5_c_im, s5_d, s5_w_glu, s5_b_glu, w_branch_out, w_out):
    params = dict(zip(_LAYER_KEYS, (norm_g, w_in, mla_q_a_norm, mla_w_q_up, mla_kv_a_norm, mla_w_kv_up, mla_q_norm,
                                    mla_k_norm, fox_b_f, fox_q_norm, fox_k_norm, s5_lambda_re, s5_lambda_im,
                                    s5_log_dt, s5_b_re, s5_b_im, s5_c_re, s5_c_im, s5_d, s5_w_glu, s5_b_glu,
                                    w_branch_out, w_out)))
    return _forward(x, positions, params)
```

```python
import functools
import math

import numpy as np
import jax
import jax.numpy as jnp
from jax import lax
from jax.experimental import pallas as pl
from jax.experimental.pallas import tpu as pltpu

F32 = jnp.float32
BF16 = jnp.bfloat16

D_MODEL = 1024
CHUNK = 64
EPS = 1e-6

MLA_HEADS = 8
MLA_NOPE = 64
MLA_ROPE = 32
MLA_V = 64
MLA_Q_RANK = 256
MLA_KV_RANK = 128
MLA_WIDTH = MLA_HEADS * MLA_V
MLA_QK_DIM = MLA_NOPE + MLA_ROPE
ROPE_THETA = 10000.0

FOX_HEADS = 8
FOX_HEAD_DIM = 64
FOX_WIDTH = FOX_HEADS * FOX_HEAD_DIM

S5_WIDTH = 512
S5_GROUP = 16
S5_GROUPS = S5_WIDTH // S5_GROUP
S5_STATE = 64
S5_ROW_PAD = 8
S5_SCAN_GROUPS = 2
S5_GROUP_TILES = 4
S5_SCAN_UNROLL = 64

BRANCH_WIDTH = MLA_WIDTH + FOX_WIDTH + S5_WIDTH

LANE = 128
HEAD_PAD = 128
PAIR = 2 * HEAD_PAD
N_PAIRS = 4
N_SPLIT = 3
LOG2E = math.log2(math.e)

_OFF = np.cumsum([0, MLA_Q_RANK, MLA_KV_RANK, MLA_ROPE, FOX_WIDTH, FOX_WIDTH, FOX_WIDTH, FOX_HEADS,
                  S5_WIDTH, MLA_WIDTH, FOX_WIDTH, S5_WIDTH, D_MODEL, D_MODEL, D_MODEL]).tolist()
(O_CQ, O_CKV, O_KPE, O_FQ, O_FK, O_FV, O_FF, O_S5U, O_GATE, _o1, _o2, O_MERGE, _o3, _o4, O_END) = _OFF

C_CQ = 0
C_CKV = C_CQ + MLA_Q_RANK
C_KPA = C_CKV + MLA_KV_RANK
C_KPB = C_KPA + LANE
C_FQ = C_KPB + LANE
C_FK = C_FQ + FOX_WIDTH
C_FV = C_FK + FOX_WIDTH
C_FF = C_FV + FOX_WIDTH
C_S5 = C_FF + LANE
C_END = C_S5 + S5_WIDTH

VMEM_LIMIT = 56 * 1024 * 1024


def _cparams(sem):
    return pltpu.CompilerParams(dimension_semantics=sem, vmem_limit_bytes=VMEM_LIMIT)


def _full(shape):
    n = len(shape)
    return pl.BlockSpec(shape, lambda *_: (0,) * n)


def _rms(x, g):
    return x * lax.rsqrt(jnp.mean(x * x, axis=-1, keepdims=True) + EPS) * g


def _dot(a, b):
    return jnp.dot(a, b, preferred_element_type=F32)


def _dot_nt(a, b):
    return lax.dot_general(a, b, (((1,), (1,)), ((), ())), preferred_element_type=F32)


def _store_v_with_ones(v_ref, v):
    lane = lax.broadcasted_iota(jnp.int32, (v.shape[0], LANE), 1)
    for p in range(N_PAIRS):
        vp = v[:, p * LANE:(p + 1) * LANE]
        v_ref[0, :, p * PAIR:p * PAIR + LANE] = jnp.where(lane < LANE // 2, vp, 1.0).astype(v_ref.dtype)
        v_ref[0, :, p * PAIR + LANE:(p + 1) * PAIR] = jnp.where(lane < LANE // 2, 1.0, vp).astype(v_ref.dtype)


def _rope_kernel(pos_ref, inv_ref, cos_ref, sin_ref, nsin_ref):
    ang = pos_ref[...].astype(F32) * inv_ref[...]
    sin = jnp.sin(ang)
    cos_ref[...] = jnp.cos(ang)
    sin_ref[...] = sin
    nsin_ref[...] = -sin


def _rope_tables(positions):
    B, S = positions.shape
    half = MLA_ROPE // 2
    per_row = LANE // half
    inv = 1.0 / (ROPE_THETA ** (jnp.arange(0, MLA_ROPE, 2, dtype=F32) / MLA_ROPE))
    rows = B * S // per_row
    pos = jnp.broadcast_to(positions.reshape(rows, per_row, 1), (rows, per_row, half)).reshape(rows, LANE)
    tr = min(rows, 1024)
    blk = pl.BlockSpec((tr, LANE), lambda r: (r, 0))
    cos, sin, nsin = (t.reshape(B, S, half) for t in pl.pallas_call(
        _rope_kernel,
        grid=(rows // tr,),
        in_specs=[blk, _full((1, LANE))],
        out_specs=[blk, blk, blk],
        out_shape=[jax.ShapeDtypeStruct((rows, LANE), F32)] * 3,
        compiler_params=_cparams(("parallel",)),
        name="rope_tables",
    )(pos, jnp.tile(inv, per_row)[None]))
    lead = jnp.ones((B, S, MLA_NOPE), F32)
    tail = jnp.zeros((B, S, LANE - MLA_NOPE - MLA_ROPE), F32)
    ca = jnp.concatenate([lead, cos, cos, tail], axis=-1)
    sb = jnp.concatenate([jnp.zeros_like(lead), nsin, sin, tail], axis=-1)
    return ca, sb


def _split3(x):
    hi = x.astype(BF16)
    r1 = x - hi.astype(F32)
    mid = r1.astype(BF16)
    return hi, mid, (r1 - mid.astype(F32)).astype(BF16)


def _fox_qk_norm(x, bd, gain):
    sq = (x * x).astype(BF16)
    ss = jnp.concatenate([_dot(sq[:, c:c + PAIR], bd) for c in range(0, FOX_WIDTH, PAIR)], axis=1)
    return (x * lax.rsqrt(ss * (1.0 / FOX_HEAD_DIM) + EPS) * gain).astype(BF16)


def _prep_kernel(x_ref, ca_ref, sb_ref, ng_ref, wcat_ref, qan_ref, wqa_ref, wqb_ref, kvn_ref, wk_ref, wv_ref,
                 gqm_ref, gkm_ref, bf_ref, gqf_ref, gkf_ref, bd_ref, tri_ref,
                 eq_ref, ek_ref, oq_ref, ok_ref,
                 qm_ref, km_ref, vm_ref, qf_ref, kf_ref, vf_ref, u_ref, carry_ref):
    @pl.when(pl.program_id(1) == 0)
    def _():
        carry_ref[...] = jnp.zeros_like(carry_ref)

    x = x_ref[0]
    h = _rms(x, ng_ref[...]).astype(BF16)
    proj = _dot_nt(h, wcat_ref[...])
    ca = ca_ref[0]
    sb = sb_ref[0]

    cqn = _rms(proj[:, C_CQ:C_CQ + MLA_Q_RANK], qan_ref[...]).astype(BF16)
    qa = _dot(cqn, wqa_ref[...])
    qb = _dot(cqn, wqb_ref[...])
    ckvn = _rms(proj[:, C_CKV:C_CKV + MLA_KV_RANK], kvn_ref[...]).astype(BF16)
    kc = _dot(ckvn, wk_ref[...])
    _store_v_with_ones(vm_ref, _dot(ckvn, wv_ref[...]))
    kpe = proj[:, C_KPA:C_KPA + LANE] * ca + proj[:, C_KPB:C_KPB + LANE] * sb
    inv_d = 1.0 / MLA_QK_DIM
    for hd in range(MLA_HEADS):
        sl = slice(hd * HEAD_PAD, (hd + 1) * HEAD_PAD)
        qh = qa[:, sl] * ca + qb[:, sl] * sb
        ss = jnp.sum(qh * qh, axis=-1, keepdims=True)
        qm_ref[0, :, sl] = (qh * lax.rsqrt(ss * inv_d + EPS) * gqm_ref[...]).astype(BF16)
        kh = kc[:, sl] + kpe
        ss = jnp.sum(kh * kh, axis=-1, keepdims=True)
        km_ref[0, :, sl] = (kh * lax.rsqrt(ss * inv_d + EPS) * gkm_ref[...]).astype(BF16)

    fqn = _fox_qk_norm(proj[:, C_FQ:C_FQ + FOX_WIDTH], bd_ref[...], gqf_ref[...])
    fkn = _fox_qk_norm(proj[:, C_FK:C_FK + FOX_WIDTH], bd_ref[...], gkf_ref[...])
    _store_v_with_ones(vf_ref, proj[:, C_FV:C_FV + FOX_WIDTH])

    z = proj[:, C_FF:C_FF + LANE] + bf_ref[...]
    log_f = jnp.minimum(z, 0.0) - jnp.log1p(jnp.exp(-jnp.abs(z)))
    sums = _dot(tri_ref[...], jnp.concatenate(_split3(log_f), axis=1))
    cum = sums[:, :LANE] + sums[:, LANE:2 * LANE] + sums[:, 2 * LANE:] + carry_ref[...]
    carry_ref[...] = cum[cum.shape[0] - 1:, :]
    hi, mid, lo = _split3(cum * LOG2E)
    lane = lax.broadcasted_iota(jnp.int32, hi.shape, 1)
    pieces = jnp.where(lane < FOX_HEADS, hi, jnp.where(lane < 2 * FOX_HEADS, mid, lo))
    auxq = (_dot(pieces, eq_ref[...]) + oq_ref[...]).astype(BF16)
    auxk = (_dot(pieces, ek_ref[...]) + ok_ref[...]).astype(BF16)
    for p in range(N_PAIRS):
        src = slice(p * LANE, (p + 1) * LANE)
        qf_ref[0, :, p * PAIR:p * PAIR + LANE] = fqn[:, src]
        qf_ref[0, :, p * PAIR + LANE:(p + 1) * PAIR] = auxq[:, src]
        kf_ref[0, :, p * PAIR:p * PAIR + LANE] = fkn[:, src]
        kf_ref[0, :, p * PAIR + LANE:(p + 1) * PAIR] = auxk[:, src]

    u_ref[0] = proj[:, C_S5:C_S5 + S5_WIDTH]


def _layer(arr, l, single_buffer=False):
    n = arr.ndim - 1
    mode = dict(pipeline_mode=pl.Buffered(1)) if single_buffer else {}
    return pl.BlockSpec((None,) + arr.shape[1:], lambda *_: (l,) + (0,) * n, **mode)


def _pack_w_in(w_in):
    wt = jnp.transpose(w_in, (0, 2, 1))
    L = wt.shape[0]
    half = MLA_ROPE // 2
    zeros = lambda n: jnp.zeros((L, n, D_MODEL), F32)
    kpe_lo = wt[:, O_KPE:O_KPE + half]
    kpe_hi = wt[:, O_KPE + half:O_KPE + MLA_ROPE]
    tail = zeros(LANE - MLA_NOPE - MLA_ROPE)
    ff = wt[:, O_FF:O_FF + FOX_HEADS]
    pieces = [wt[:, O_CQ:O_CQ + MLA_Q_RANK + MLA_KV_RANK],
              zeros(MLA_NOPE), kpe_lo, kpe_hi, tail,
              zeros(MLA_NOPE), kpe_hi, kpe_lo, tail,
              wt[:, O_FQ:O_FQ + 3 * FOX_WIDTH],
              ff, ff, ff, zeros(LANE - N_SPLIT * FOX_HEADS),
              wt[:, O_S5U:O_S5U + S5_WIDTH]]
    return jnp.concatenate(pieces, axis=1).astype(BF16), wt[:, O_GATE:O_END].astype(BF16)


def _prep_weights(pr, wcat):
    L = wcat.shape[0]
    row = lambda a: a[:, None, :]
    half = MLA_ROPE // 2

    wq = pr["mla_w_q_up"].reshape(L, MLA_Q_RANK, MLA_HEADS, MLA_QK_DIM)
    nope, pe = wq[..., :MLA_NOPE], wq[..., MLA_NOPE:]
    pe_sw = jnp.concatenate([pe[..., half:], pe[..., :half]], axis=3)
    z32 = jnp.zeros((L, MLA_Q_RANK, MLA_HEADS, HEAD_PAD - MLA_QK_DIM), F32)
    wqa = jnp.concatenate([nope, pe, z32], axis=3).reshape(L, MLA_Q_RANK, MLA_HEADS * HEAD_PAD).astype(BF16)
    wqb = jnp.concatenate([jnp.zeros_like(nope), pe_sw, z32], axis=3).reshape(L, MLA_Q_RANK, -1).astype(BF16)

    wkv = pr["mla_w_kv_up"].reshape(L, MLA_KV_RANK, MLA_HEADS, MLA_NOPE + MLA_V)
    wk = jnp.concatenate([wkv[..., :MLA_NOPE], jnp.zeros((L, MLA_KV_RANK, MLA_HEADS, HEAD_PAD - MLA_NOPE), F32)],
                         axis=3).reshape(L, MLA_KV_RANK, -1).astype(BF16)
    wv = wkv[..., MLA_NOPE:].reshape(L, MLA_KV_RANK, MLA_WIDTH).astype(BF16)

    pad = jnp.zeros((L, HEAD_PAD - MLA_QK_DIM), F32)
    gqm = row(jnp.concatenate([pr["mla_q_norm"], pad], axis=1) * (LOG2E / math.sqrt(MLA_QK_DIM)))
    gkm = row(jnp.concatenate([pr["mla_k_norm"], pad], axis=1))
    bf = row(jnp.concatenate([pr["fox_b_f"]] * N_SPLIT + [jnp.zeros((L, LANE - N_SPLIT * FOX_HEADS), F32)], axis=1))
    gqf = row(jnp.tile(pr["fox_q_norm"], (1, FOX_HEADS)) * (LOG2E / math.sqrt(FOX_HEAD_DIM)))
    gkf = row(jnp.tile(pr["fox_k_norm"], (1, FOX_HEADS)))
    return dict(ng=row(pr["norm_g"]), wcat=wcat, qan=row(pr["mla_q_a_norm"]), wqa=wqa, wqb=wqb,
                kvn=row(pr["mla_kv_a_norm"]), wk=wk, wv=wv, gqm=gqm, gkm=gkm, bf=bf, gqf=gqf, gkf=gkf)


def _prep_consts(tm):
    bd = np.kron(np.eye(PAIR // FOX_HEAD_DIM, dtype=np.float32), np.ones((FOX_HEAD_DIM, FOX_HEAD_DIM), np.float32))
    tri = np.tril(np.ones((tm, tm), np.float32))
    eq = np.zeros((LANE, N_PAIRS * LANE), np.float32)
    ek = np.zeros_like(eq)
    oq = np.zeros((1, N_PAIRS * LANE), np.float32)
    ok = np.zeros_like(oq)
    for hd in range(FOX_HEADS):
        p, a = divmod(hd, 2)
        base = p * LANE + 2 * N_SPLIT * a
        for i in range(N_SPLIT):
            ek[i * FOX_HEADS + hd, base + i] = -1.0
            oq[0, base + i] = 1.0
            eq[i * FOX_HEADS + hd, base + N_SPLIT + i] = 1.0
            ok[0, base + N_SPLIT + i] = 1.0
    return dict(bd=jnp.asarray(bd, BF16), tri=jnp.asarray(tri, BF16),
                eq=jnp.asarray(eq, BF16), ek=jnp.asarray(ek, BF16), oq=jnp.asarray(oq), ok=jnp.asarray(ok))


def _prep_call(x, ca, sb, w, c, l, tm):
    B, S, _ = x.shape
    tok = lambda width: pl.BlockSpec((1, tm, width), lambda b, s: (b, s, 0))
    stacked = [w["ng"], w["wcat"], w["qan"], w["wqa"], w["wqb"], w["kvn"], w["wk"], w["wv"],
               w["gqm"], w["gkm"], w["bf"], w["gqf"], w["gkf"]]
    consts = [c["bd"], c["tri"], c["eq"], c["ek"], c["oq"], c["ok"]]
    args = [x, ca, sb] + stacked + consts
    in_specs = ([tok(D_MODEL), tok(LANE), tok(LANE)] + [_layer(a, l) for a in stacked]
                + [_full(a.shape) for a in consts])
    widths = [MLA_HEADS * HEAD_PAD, MLA_HEADS * HEAD_PAD, N_PAIRS * PAIR, N_PAIRS * PAIR, N_PAIRS * PAIR, N_PAIRS * PAIR]
    out_shape = [jax.ShapeDtypeStruct((B, S, wd), BF16) for wd in widths]
    out_shape.append(jax.ShapeDtypeStruct((B, S, S5_WIDTH), F32))
    return pl.pallas_call(
        _prep_kernel,
        grid=(B, S // tm),
        in_specs=in_specs,
        out_specs=[tok(wd) for wd in widths] + [tok(S5_WIDTH)],
        out_shape=out_shape,
        scratch_shapes=[pltpu.VMEM((1, LANE), F32)],
        compiler_params=_cparams(("parallel", "arbitrary")),
        name="prep",
    )(*args)


def _diag_tile(tq):
    return tq // 2 if tq >= 2 * PAIR else tq


def _attn_kernel(q_ref, k_ref, v_ref, hm_ref, o_ref, qm_ref, sa_ref, sb_ref, bias_ref, m_ref, acc_ref, *,
                 tq, chunk_causal):
    S = q_ref.shape[1]
    nq = S // tq
    td = _diag_tile(tq)
    steps = [(qi, j) for qi in range(nq) for j in range(qi + 1)]
    row = lax.broadcasted_iota(jnp.int32, (td, td), 0)
    col = lax.broadcasted_iota(jnp.int32, (td, td), 1)
    shift = int(math.log2(CHUNK)) if chunk_causal else 0
    bias_ref[...] = jnp.where((col >> shift) <= (row >> shift), 0.0, -jnp.inf)
    for a in range(2):
        qm_ref[a] = q_ref[0] * hm_ref[a:a + 1, :]

    def row_blocks(qi, j):
        if j < qi:
            return [(0, tq, tq)]
        return [(r * td, td, (r + 1) * td) for r in range(tq // td)]

    def qk(qi, j, s_ref):
        for r0, nr, nk in row_blocks(qi, j):
            k = k_ref[0, j * tq:j * tq + nk, :]
            for a in range(2):
                q = qm_ref[a, qi * tq + r0:qi * tq + r0 + nr, :]
                s_ref[a, r0:r0 + nr, :nk] = lax.dot_general(q, k, (((1,), (1,)), ((), ())),
                                                            preferred_element_type=F32)

    qk(0, 0, sa_ref)
    bufs = (sa_ref, sb_ref)
    for t, (qi, j) in enumerate(steps):
        s_cur, s_next = bufs[t % 2], bufs[(t + 1) % 2]
        if t + 1 < len(steps):
            qk(*steps[t + 1], s_next)
        for r0, nr, nk in row_blocks(qi, j):
            rows = slice(r0, r0 + nr)
            lane = lax.broadcasted_iota(jnp.int32, (nr, LANE), 1)
            outs = []
            for a in range(2):
                s = s_cur[a, rows, :nk]
                if j == qi:
                    tail = s[:, nk - td:] + bias_ref[...]
                    s = tail if nk == td else jnp.concatenate([s[:, :nk - td], tail], axis=1)
                v = v_ref[0, j * tq:j * tq + nk, a * LANE:(a + 1) * LANE]
                m_cur = jnp.max(s, axis=-1, keepdims=True)
                if j == 0:
                    m_new = jnp.broadcast_to(m_cur, (nr, LANE))
                    p = jnp.exp2(s - m_cur)
                    acc = _dot(p.astype(BF16), v)
                else:
                    m_prev = m_ref[a, rows]
                    m_new = jnp.maximum(m_prev, m_cur)
                    alpha = jnp.exp2(m_prev - m_new)
                    p = jnp.exp2(s - jnp.concatenate([m_new] * (nk // LANE), axis=1))
                    acc = alpha * acc_ref[a, rows] + _dot(p.astype(BF16), v)
                if j == qi:
                    outs.append(acc / pltpu.roll(acc, LANE // 2, axis=1))
                else:
                    acc_ref[a, rows] = acc
                    m_ref[a, rows] = m_new
            if j == qi:
                out = jnp.where(lane < LANE // 2, outs[0], outs[1]).astype(o_ref.dtype)
                o_ref[0, qi * tq + r0:qi * tq + r0 + nr, :] = out


def _attn_call(q, k, v, head_mask, chunk_causal, tq, name):
    B, S, _ = q.shape
    kern = functools.partial(_attn_kernel, tq=tq, chunk_causal=chunk_causal)
    td = _diag_tile(tq)
    return pl.pallas_call(
        kern,
        grid=(B, N_PAIRS),
        in_specs=[pl.BlockSpec((1, S, PAIR), lambda b, p: (b, 0, p)),
                  pl.BlockSpec((1, S, PAIR), lambda b, p: (b, 0, p)),
                  pl.BlockSpec((1, S, PAIR), lambda b, p: (b, 0, p)),
                  _full((2, PAIR))],
        out_specs=pl.BlockSpec((1, S, LANE), lambda b, p: (b, 0, p)),
        out_shape=jax.ShapeDtypeStruct((B, S, N_PAIRS * LANE), BF16),
        scratch_shapes=[pltpu.VMEM((2, S, PAIR), BF16), pltpu.VMEM((2, tq, tq), F32), pltpu.VMEM((2, tq, tq), F32),
                        pltpu.VMEM((td, td), F32), pltpu.VMEM((2, tq, LANE), F32), pltpu.VMEM((2, tq, LANE), F32)],
        compiler_params=_cparams(("parallel", "parallel")),
        name=name,
    )(q, k, v, head_mask)


def _head_masks():
    mla = np.zeros((2, PAIR), np.float32)
    fox = np.zeros((2, PAIR), np.float32)
    for a in range(2):
        mla[a, a * HEAD_PAD:(a + 1) * HEAD_PAD] = 1.0
        fox[a, a * FOX_HEAD_DIM:(a + 1) * FOX_HEAD_DIM] = 1.0
        fox[a, LANE + 2 * N_SPLIT * a:LANE + 2 * N_SPLIT * (a + 1)] = 1.0
    return jnp.asarray(mla, BF16), jnp.asarray(fox, BF16)


def _s5_param_kernel(lr_ref, li_ref, ldt_ref, br_ref, bi_ref, are_ref, aim_ref, bbr_ref, bbi_ref):
    lr = lr_ref[...]
    li = li_ref[...]
    dt = jnp.exp(ldt_ref[...])
    mag = jnp.exp(lr * dt)
    a_re = mag * jnp.cos(li * dt)
    a_im = mag * jnp.sin(li * dt)
    den = lr * lr + li * li
    f_re = ((a_re - 1.0) * lr + a_im * li) / den
    f_im = (a_im * lr - (a_re - 1.0) * li) / den
    br = br_ref[...]
    bi = bi_ref[...]
    are_ref[...] = a_re
    aim_ref[...] = a_im
    bbr_ref[...] = f_re * br - f_im * bi
    bbi_ref[...] = f_re * bi + f_im * br


def _s5_params(pr, n_batch):
    G, N, C = S5_GROUPS, S5_STATE, S5_GROUP
    L = pr["s5_lambda_re"].shape[0]
    rep = lambda a: jnp.repeat(a.reshape(L * G, N), C, axis=0)
    ldt = jnp.broadcast_to(pr["s5_log_dt"][:, :, None], (L, G, N))
    tr = lambda b: jnp.transpose(b, (0, 1, 3, 2)).reshape(L * G * C, N)
    shp = jax.ShapeDtypeStruct((L * G * C, N), F32)
    a_re, a_im, bb_re, bb_im = pl.pallas_call(
        _s5_param_kernel, out_shape=[shp] * 4, name="s5_params",
    )(rep(pr["s5_lambda_re"]), rep(pr["s5_lambda_im"]), rep(ldt), tr(pr["s5_b_re"]), tr(pr["s5_b_im"]))
    n_tiles = G * N // PAIR
    gpt = PAIR // N
    t_idx = np.arange(n_tiles)[:, None, None]
    mask_b = jnp.asarray(np.arange(LANE)[None, :, None] // C == gpt * (t_idx % 2) + np.arange(PAIR)[None, None, :] // N)
    mask_c = jnp.asarray(np.arange(PAIR)[None, None, :] // C
                         == gpt * (t_idx % S5_GROUP_TILES) + np.arange(PAIR)[None, :, None] // N)

    def group_major(re, im):
        split = lambda a: a.reshape(L, S5_SCAN_GROUPS, S5_GROUP_TILES, *a.shape[2:])
        both = jnp.concatenate([split(re), split(im)], axis=2)
        return both.reshape(L, 2 * n_tiles, *re.shape[2:]).astype(BF16)

    def b_tiles(bb):
        blocks = jnp.repeat(bb.reshape(L, G * C // LANE, LANE, N), 2, axis=1)
        return jnp.where(mask_b, jnp.tile(blocks, (1, 1, 1, gpt)), 0.0)

    def c_tiles(cc):
        rows = jnp.transpose(cc, (0, 1, 3, 2)).reshape(L, n_tiles, PAIR, C)
        return jnp.where(mask_c, jnp.tile(rows, (1, 1, 1, PAIR // C)), 0.0)

    w_b = group_major(b_tiles(bb_re), b_tiles(bb_im))
    w_c = group_major(c_tiles(pr["s5_c_re"]), -c_tiles(pr["s5_c_im"]))
    first = lambda a: a.reshape(L, G, C, N)[:, :, 0, :].reshape(L, S5_SCAN_GROUPS, S5_GROUP_TILES, 1, PAIR)
    a_rep = jnp.broadcast_to(jnp.concatenate([first(a_re), first(a_im)], axis=2),
                             (L, S5_SCAN_GROUPS, 2 * S5_GROUP_TILES, n_batch, PAIR))
    return dict(a=a_rep, w_b=w_b, w_c=w_c,
                d=pr["s5_d"][:, None, :], w_glu=pr["s5_w_glu"].astype(BF16), b_glu=pr["s5_b_glu"][:, None, :])


def _s5_kernel(u_ref, a_ref, wb_ref, wc_ref, d_ref, wg_ref, bg_ref, o_ref, st_ref, us_ref, zs_ref, ub_ref, uf_ref,
               y_ref, bu0_ref, bu1_ref, x0_ref, x1_ref, *, nb, ts, n_sub):
    @pl.when(pl.program_id(0) == 0)
    def _():
        st_ref[...] = jnp.zeros_like(st_ref)

    pitch = ts + S5_ROW_PAD
    nk = S5_WIDTH // LANE
    gt = S5_GROUP_TILES
    n_iter = ts // S5_SCAN_UNROLL
    tiles_per_iter = 2 * gt // n_iter
    steps_per_tile = S5_SCAN_UNROLL // tiles_per_iter
    bu = (bu0_ref, bu1_ref)
    xs = (x0_ref, x1_ref)
    everything = slice(None)

    def stage_u(c):
        for b in range(nb):
            for k in range(nk):
                us_ref[k, b * pitch:b * pitch + ts, :] = u_ref[b, c * ts:(c + 1) * ts, k * LANE:(k + 1) * LANE]
        u = jnp.concatenate(
            [jnp.concatenate([us_ref[k, pl.ds(t, nb, stride=pitch), :] for k in range(nk)], axis=1) for t in range(ts)],
            axis=0)
        uf_ref[c % 2] = u
        for k in range(nk):
            ub_ref[k] = u[:, k * LANE:(k + 1) * LANE].astype(BF16)

    def b_item(g, k, rows):
        cb = (gt * g) // 2 + (k % gt) // 2
        bu[g][k, rows, :] = _dot(ub_ref[cb, rows, :], wb_ref[2 * gt * g + k])

    def c_item(g, k, rows):
        y_ref[g, rows, :] += _dot(xs[g][k, rows, :].astype(BF16), wc_ref[2 * gt * g + k])

    def scan_phase(g, b_target, c_source):
        ar = a_ref[g, :gt]
        ai = a_ref[g, gt:]
        if c_source is not None:
            y_ref[c_source] = jnp.zeros(y_ref.shape[1:], F32)

        def body(i, carry):
            xr, xi = carry
            for j in range(tiles_per_iter):
                k = i * tiles_per_iter + j
                if b_target is not None:
                    b_item(b_target, k, everything)
                if c_source is not None:
                    c_item(c_source, k, everything)
                for tt in range(steps_per_tile):
                    t = i * S5_SCAN_UNROLL + j * steps_per_tile + tt
                    r = pl.ds(t * nb, nb) if n_iter == 1 else pl.ds(pl.multiple_of(t * nb, nb), nb)
                    nr = ar * xr - ai * xi + bu[g][:gt, r, :]
                    ni = ar * xi + ai * xr + bu[g][gt:, r, :]
                    xs[g][:gt, r, :] = nr
                    xs[g][gt:, r, :] = ni
                    xr, xi = nr, ni
            return xr, xi

        carry = (st_ref[g, :gt], st_ref[g, gt:])
        xr, xi = body(0, carry) if n_iter == 1 else lax.fori_loop(0, n_iter, body, carry)
        st_ref[g, :gt] = xr
        st_ref[g, gt:] = xi

    def epilogue(c):
        y = jnp.concatenate([y_ref[g] for g in range(S5_SCAN_GROUPS)], axis=1) + d_ref[...] * uf_ref[c % 2]
        z = jax.nn.gelu(y, approximate=True)
        z = z * jax.nn.sigmoid(_dot(z.astype(BF16), wg_ref[...]) + bg_ref[...])
        for t in range(ts):
            for k in range(nk):
                zs_ref[k, pl.ds(t, nb, stride=pitch), :] = z[t * nb:(t + 1) * nb, k * LANE:(k + 1) * LANE]
        for b in range(nb):
            rows = jnp.concatenate([zs_ref[k, b * pitch:b * pitch + ts, :] for k in range(nk)], axis=1)
            o_ref[b, c * ts:(c + 1) * ts, :] = rows.astype(o_ref.dtype)

    stage_u(0)
    for k in range(2 * gt):
        b_item(0, k, everything)
    for c in range(n_sub):
        scan_phase(0, 1, 1 if c > 0 else None)
        if c > 0:
            epilogue(c - 1)
        if c + 1 < n_sub:
            stage_u(c + 1)
        scan_phase(1, 0 if c + 1 < n_sub else None, 0)
    y_ref[1] = jnp.zeros(y_ref.shape[1:], F32)
    for k in range(2 * gt):
        c_item(1, k, everything)
    epilogue(n_sub - 1)


def _s5_call(u, p, l, ts, n_sub):
    B, S, _ = u.shape
    blk = ts * n_sub
    rows = ts * B
    args = [u, p["a"], p["w_b"], p["w_c"], p["d"], p["w_glu"], p["b_glu"]]
    kern = functools.partial(_s5_kernel, nb=B, ts=ts, n_sub=n_sub)
    once = lambda a: pl.BlockSpec((None,) + a.shape[1:], lambda i: (l,) + (0,) * (a.ndim - 1),
                                  pipeline_mode=pl.Buffered(1))
    slab_buf = pltpu.VMEM((S5_WIDTH // LANE, B * (ts + S5_ROW_PAD), LANE), F32)
    tile_buf = pltpu.VMEM((2 * S5_GROUP_TILES, rows, PAIR), F32)
    return pl.pallas_call(
        kern,
        grid=(S // blk,),
        in_specs=[pl.BlockSpec((B, blk, S5_WIDTH), lambda i: (0, i, 0))] + [once(a) for a in args[1:]],
        out_specs=pl.BlockSpec((B, blk, S5_WIDTH), lambda i: (0, i, 0)),
        out_shape=jax.ShapeDtypeStruct((B, S, S5_WIDTH), BF16),
        scratch_shapes=[pltpu.VMEM((S5_SCAN_GROUPS, 2 * S5_GROUP_TILES, B, PAIR), F32), slab_buf, slab_buf,
                        pltpu.VMEM((S5_WIDTH // LANE, rows, LANE), BF16), pltpu.VMEM((2, rows, S5_WIDTH), F32),
                        pltpu.VMEM((S5_SCAN_GROUPS, rows, PAIR), F32), tile_buf, tile_buf, tile_buf, tile_buf],
        compiler_params=_cparams(("arbitrary",)),
        name="s5",
    )(*args)


def _merge_kernel(x_ref, ym_ref, yf_ref, ys_ref, ng_ref, wgm_ref, wo_ref, wout_ref, o_ref):
    x = x_ref[0]
    h = _rms(x, ng_ref[...]).astype(BF16)
    merged = None
    for b, y_ref in enumerate((ym_ref, yf_ref, ys_ref)):
        g = _dot_nt(h, wgm_ref[b * MLA_WIDTH:(b + 1) * MLA_WIDTH, :])
        gated = (y_ref[0].astype(F32) * (g * jax.nn.sigmoid(g))).astype(BF16)
        o = _dot(gated, wo_ref[b * MLA_WIDTH:(b + 1) * MLA_WIDTH, :])
        m = _dot_nt(h, wgm_ref[BRANCH_WIDTH + b * D_MODEL:BRANCH_WIDTH + (b + 1) * D_MODEL, :])
        term = jax.nn.sigmoid(m) * o
        merged = term if merged is None else merged + term
    o_ref[0] = x + _dot(merged.astype(BF16), wout_ref[...])


def _merge_weights(pr, wgm):
    return [pr["norm_g"][:, None, :], wgm, pr["w_branch_out"].astype(BF16), pr["w_out"].astype(BF16)]


def _merge_call(x, y_mla, y_fox, y_s5, w, l, tm):
    B, S, _ = x.shape
    tok = lambda width: pl.BlockSpec((1, tm, width), lambda b, s: (b, s, 0))
    args = [x, y_mla, y_fox, y_s5] + w
    return pl.pallas_call(
        _merge_kernel,
        grid=(B, S // tm),
        in_specs=([tok(D_MODEL), tok(MLA_WIDTH), tok(FOX_WIDTH), tok(S5_WIDTH)]
                  + [_layer(a, l, single_buffer=True) for a in w]),
        out_specs=tok(D_MODEL),
        out_shape=jax.ShapeDtypeStruct(x.shape, x.dtype),
        compiler_params=_cparams(("parallel", "parallel")),
        name="merge",
    )(*args)


_LAYER_KEYS = ("norm_g", "w_in", "mla_q_a_norm", "mla_w_q_up", "mla_kv_a_norm", "mla_w_kv_up", "mla_q_norm",
               "mla_k_norm", "fox_b_f", "fox_q_norm", "fox_k_norm", "s5_lambda_re", "s5_lambda_im", "s5_log_dt",
               "s5_b_re", "s5_b_im", "s5_c_re", "s5_c_im", "s5_d", "s5_w_glu", "s5_b_glu", "w_branch_out", "w_out")


def _forward(x, positions, params):
    B, S, _ = x.shape
    tm = min(S, 512)
    tq = min(S, 512)
    ts = min(S, 64)
    depth = params["w_in"].shape[0]
    ca, sb = _rope_tables(positions)
    consts = _prep_consts(tm)
    mask_mla, mask_fox = _head_masks()
    wcat, wgm = _pack_w_in(params["w_in"])
    w_prep = _prep_weights(params, wcat)
    w_s5 = _s5_params(params, B)
    w_merge = _merge_weights(params, wgm)
    h = x
    for l in range(depth):
        qm, km, vm, qf, kf, vf, u = _prep_call(h, ca, sb, w_prep, consts, l, tm)
        y_mla = _attn_call(qm, km, vm, mask_mla, True, tq, "attn_mla")
        y_fox = _attn_call(qf, kf, vf, mask_fox, False, tq, "attn_fox")
        y_s5 = _s5_call(u, w_s5, l, ts, min(S // ts, 4))
        h = _merge_call(h, y_mla, y_fox, y_s5, w_merge, l, min(S, 2 * tm))
    return h


def kernel(x, positions, norm_g, w_in, mla_q_a_norm, mla_w_q_up, mla_kv_a_norm, mla_w_kv_up, mla_q_norm, mla_k_norm, fox_b_f, fox_q_norm, fox_k_norm, s5_lambda_re, s5_lambda_im, s5_log_dt, s5_b_re, s5_b_im, s5_c_re, s5_c_im, s5_d, s5_w_glu, s5_b_glu, w_branch_out, w_out):
    params = dict(zip(_LAYER_KEYS, (norm_g, w_in, mla_q_a_norm, mla_w_q_up, mla_kv_a_norm, mla_w_kv_up, mla_q_norm,
                                    mla_k_norm, fox_b_f, fox_q_norm, fox_k_norm, s5_lambda_re, s5_lambda_im,
                                    s5_log_dt, s5_b_re, s5_b_im, s5_c_re, s5_c_im, s5_d, s5_w_glu, s5_b_glu,
                                    w_branch_out, w_out)))
    return _forward(x, positions, params)
```

```python
import functools
import math

import numpy as np
import jax
import jax.numpy as jnp
from jax import lax
from jax.experimental import pallas as pl
from jax.experimental.pallas import tpu as pltpu

F32 = jnp.float32
BF16 = jnp.bfloat16

D_MODEL = 1024
CHUNK = 64
EPS = 1e-6

MLA_HEADS = 8
MLA_NOPE = 64
MLA_ROPE = 32
MLA_V = 64
MLA_Q_RANK = 256
MLA_KV_RANK = 128
MLA_WIDTH = MLA_HEADS * MLA_V
MLA_QK_DIM = MLA_NOPE + MLA_ROPE
ROPE_THETA = 10000.0

FOX_HEADS = 8
FOX_HEAD_DIM = 64
FOX_WIDTH = FOX_HEADS * FOX_HEAD_DIM

S5_WIDTH = 512
S5_GROUP = 16
S5_GROUPS = S5_WIDTH // S5_GROUP
S5_STATE = 64
S5_ROW_PAD = 8
S5_SCAN_GROUPS = 2
S5_GROUP_TILES = 4
S5_SCAN_UNROLL = 64

BRANCH_WIDTH = MLA_WIDTH + FOX_WIDTH + S5_WIDTH

LANE = 128
HEAD_PAD = 128
PAIR = 2 * HEAD_PAD
N_PAIRS = 4
N_SPLIT = 3
LOG2E = math.log2(math.e)

_OFF = np.cumsum([0, MLA_Q_RANK, MLA_KV_RANK, MLA_ROPE, FOX_WIDTH, FOX_WIDTH, FOX_WIDTH, FOX_HEADS,
                  S5_WIDTH, MLA_WIDTH, FOX_WIDTH, S5_WIDTH, D_MODEL, D_MODEL, D_MODEL]).tolist()
(O_CQ, O_CKV, O_KPE, O_FQ, O_FK, O_FV, O_FF, O_S5U, O_GATE, _o1, _o2, O_MERGE, _o3, _o4, O_END) = _OFF

C_CQ = 0
C_CKV = C_CQ + MLA_Q_RANK
C_KPA = C_CKV + MLA_KV_RANK
C_KPB = C_KPA + LANE
C_FQ = C_KPB + LANE
C_FK = C_FQ + FOX_WIDTH
C_FV = C_FK + FOX_WIDTH
C_FF = C_FV + FOX_WIDTH
C_S5 = C_FF + LANE
C_END = C_S5 + S5_WIDTH

VMEM_LIMIT = 56 * 1024 * 1024


def _cparams(sem):
    return pltpu.CompilerParams(dimension_semantics=sem, vmem_limit_bytes=VMEM_LIMIT)


def _full(shape):
    n = len(shape)
    return pl.BlockSpec(shape, lambda *_: (0,) * n)


def _rms(x, g):
    return x * lax.rsqrt(jnp.mean(x * x, axis=-1, keepdims=True) + EPS) * g


def _dot(a, b):
    return jnp.dot(a, b, preferred_element_type=F32)


def _dot_nt(a, b):
    return lax.dot_general(a, b, (((1,), (1,)), ((), ())), preferred_element_type=F32)


def _store_v_with_ones(v_ref, v):
    lane = lax.broadcasted_iota(jnp.int32, (v.shape[0], LANE), 1)
    for p in range(N_PAIRS):
        vp = v[:, p * LANE:(p + 1) * LANE]
        v_ref[0, :, p * PAIR:p * PAIR + LANE] = jnp.where(lane < LANE // 2, vp, 1.0).astype(v_ref.dtype)
        v_ref[0, :, p * PAIR + LANE:(p + 1) * PAIR] = jnp.where(lane < LANE // 2, 1.0, vp).astype(v_ref.dtype)


def _rope_kernel(pos_ref, inv_ref, cos_ref, sin_ref, nsin_ref):
    ang = pos_ref[...].astype(F32) * inv_ref[...]
    sin = jnp.sin(ang)
    cos_ref[...] = jnp.cos(ang)
    sin_ref[...] = sin
    nsin_ref[...] = -sin


def _rope_tables(positions):
    B, S = positions.shape
    half = MLA_ROPE // 2
    per_row = LANE // half
    inv = 1.0 / (ROPE_THETA ** (jnp.arange(0, MLA_ROPE, 2, dtype=F32) / MLA_ROPE))
    rows = B * S // per_row
    pos = jnp.broadcast_to(positions.reshape(rows, per_row, 1), (rows, per_row, half)).reshape(rows, LANE)
    tr = min(rows, 1024)
    blk = pl.BlockSpec((tr, LANE), lambda r: (r, 0))
    cos, sin, nsin = (t.reshape(B, S, half) for t in pl.pallas_call(
        _rope_kernel,
        grid=(rows // tr,),
        in_specs=[blk, _full((1, LANE))],
        out_specs=[blk, blk, blk],
        out_shape=[jax.ShapeDtypeStruct((rows, LANE), F32)] * 3,
        compiler_params=_cparams(("parallel",)),
        name="rope_tables",
    )(pos, jnp.tile(inv, per_row)[None]))
    lead = jnp.ones((B, S, MLA_NOPE), F32)
    tail = jnp.zeros((B, S, LANE - MLA_NOPE - MLA_ROPE), F32)
    ca = jnp.concatenate([lead, cos, cos, tail], axis=-1)
    sb = jnp.concatenate([jnp.zeros_like(lead), nsin, sin, tail], axis=-1)
    return ca, sb


def _split3(x):
    hi = x.astype(BF16)
    r1 = x - hi.astype(F32)
    mid = r1.astype(BF16)
    return hi, mid, (r1 - mid.astype(F32)).astype(BF16)


def _fox_qk_norm(x, bd, gain):
    sq = (x * x).astype(BF16)
    ss = jnp.concatenate([_dot(sq[:, c:c + PAIR], bd) for c in range(0, FOX_WIDTH, PAIR)], axis=1)
    return (x * lax.rsqrt(ss * (1.0 / FOX_HEAD_DIM) + EPS) * gain).astype(BF16)


def _prep_kernel(x_ref, ca_ref, sb_ref, ng_ref, wcat_ref, qan_ref, wqa_ref, wqb_ref, kvn_ref, wk_ref, wv_ref,
                 gqm_ref, gkm_ref, bf_ref, gqf_ref, gkf_ref, bd_ref, tri_ref,
                 eq_ref, ek_ref, oq_ref, ok_ref,
                 qm_ref, km_ref, vm_ref, qf_ref, kf_ref, vf_ref, u_ref, carry_ref):
    @pl.when(pl.program_id(1) == 0)
    def _():
        carry_ref[...] = jnp.zeros_like(carry_ref)

    x = x_ref[0]
    h = _rms(x, ng_ref[...]).astype(BF16)
    proj = _dot_nt(h, wcat_ref[...])
    ca = ca_ref[0]
    sb = sb_ref[0]

    cqn = _rms(proj[:, C_CQ:C_CQ + MLA_Q_RANK], qan_ref[...]).astype(BF16)
    qa = _dot(cqn, wqa_ref[...])
    qb = _dot(cqn, wqb_ref[...])
    ckvn = _rms(proj[:, C_CKV:C_CKV + MLA_KV_RANK], kvn_ref[...]).astype(BF16)
    kc = _dot(ckvn, wk_ref[...])
    _store_v_with_ones(vm_ref, _dot(ckvn, wv_ref[...]))
    kpe = proj[:, C_KPA:C_KPA + LANE] * ca + proj[:, C_KPB:C_KPB + LANE] * sb
    inv_d = 1.0 / MLA_QK_DIM
    for hd in range(MLA_HEADS):
        sl = slice(hd * HEAD_PAD, (hd + 1) * HEAD_PAD)
        qh = qa[:, sl] * ca + qb[:, sl] * sb
        ss = jnp.sum(qh * qh, axis=-1, keepdims=True)
        qm_ref[0, :, sl] = (qh * lax.rsqrt(ss * inv_d + EPS) * gqm_ref[...]).astype(BF16)
        kh = kc[:, sl] + kpe
        ss = jnp.sum(kh * kh, axis=-1, keepdims=True)
        km_ref[0, :, sl] = (kh * lax.rsqrt(ss * inv_d + EPS) * gkm_ref[...]).astype(BF16)

    fqn = _fox_qk_norm(proj[:, C_FQ:C_FQ + FOX_WIDTH], bd_ref[...], gqf_ref[...])
    fkn = _fox_qk_norm(proj[:, C_FK:C_FK + FOX_WIDTH], bd_ref[...], gkf_ref[...])
    _store_v_with_ones(vf_ref, proj[:, C_FV:C_FV + FOX_WIDTH])

    z = proj[:, C_FF:C_FF + LANE] + bf_ref[...]
    log_f = jnp.minimum(z, 0.0) - jnp.log1p(jnp.exp(-jnp.abs(z)))
    sums = _dot(tri_ref[...], jnp.concatenate(_split3(log_f), axis=1))
    cum = sums[:, :LANE] + sums[:, LANE:2 * LANE] + sums[:, 2 * LANE:] + carry_ref[...]
    carry_ref[...] = cum[cum.shape[0] - 1:, :]
    hi, mid, lo = _split3(cum * LOG2E)
    lane = lax.broadcasted_iota(jnp.int32, hi.shape, 1)
    pieces = jnp.where(lane < FOX_HEADS, hi, jnp.where(lane < 2 * FOX_HEADS, mid, lo))
    auxq = (_dot(pieces, eq_ref[...]) + oq_ref[...]).astype(BF16)
    auxk = (_dot(pieces, ek_ref[...]) + ok_ref[...]).astype(BF16)
    for p in range(N_PAIRS):
        src = slice(p * LANE, (p + 1) * LANE)
        qf_ref[0, :, p * PAIR:p * PAIR + LANE] = fqn[:, src]
        qf_ref[0, :, p * PAIR + LANE:(p + 1) * PAIR] = auxq[:, src]
        kf_ref[0, :, p * PAIR:p * PAIR + LANE] = fkn[:, src]
        kf_ref[0, :, p * PAIR + LANE:(p + 1) * PAIR] = auxk[:, src]

    u_ref[0] = proj[:, C_S5:C_S5 + S5_WIDTH]


def _layer(arr, l, single_buffer=False):
    n = arr.ndim - 1
    mode = dict(pipeline_mode=pl.Buffered(1)) if single_buffer else {}
    return pl.BlockSpec((None,) + arr.shape[1:], lambda *_: (l,) + (0,) * n, **mode)


def _pack_w_in(w_in):
    wt = jnp.transpose(w_in, (0, 2, 1))
    L = wt.shape[0]
    half = MLA_ROPE // 2
    zeros = lambda n: jnp.zeros((L, n, D_MODEL), F32)
    kpe_lo = wt[:, O_KPE:O_KPE + half]
    kpe_hi = wt[:, O_KPE + half:O_KPE + MLA_ROPE]
    tail = zeros(LANE - MLA_NOPE - MLA_ROPE)
    ff = wt[:, O_FF:O_FF + FOX_HEADS]
    pieces = [wt[:, O_CQ:O_CQ + MLA_Q_RANK + MLA_KV_RANK],
              zeros(MLA_NOPE), kpe_lo, kpe_hi, tail,
              zeros(MLA_NOPE), kpe_hi, kpe_lo, tail,
              wt[:, O_FQ:O_FQ + 3 * FOX_WIDTH],
              ff, ff, ff, zeros(LANE - N_SPLIT * FOX_HEADS),
              wt[:, O_S5U:O_S5U + S5_WIDTH]]
    return jnp.concatenate(pieces, axis=1).astype(BF16), wt[:, O_GATE:O_END].astype(BF16)


def _prep_weights(pr, wcat):
    L = wcat.shape[0]
    row = lambda a: a[:, None, :]
    half = MLA_ROPE // 2

    wq = pr["mla_w_q_up"].reshape(L, MLA_Q_RANK, MLA_HEADS, MLA_QK_DIM)
    nope, pe = wq[..., :MLA_NOPE], wq[..., MLA_NOPE:]
    pe_sw = jnp.concatenate([pe[..., half:], pe[..., :half]], axis=3)
    z32 = jnp.zeros((L, MLA_Q_RANK, MLA_HEADS, HEAD_PAD - MLA_QK_DIM), F32)
    wqa = jnp.concatenate([nope, pe, z32], axis=3).reshape(L, MLA_Q_RANK, MLA_HEADS * HEAD_PAD).astype(BF16)
    wqb = jnp.concatenate([jnp.zeros_like(nope), pe_sw, z32], axis=3).reshape(L, MLA_Q_RANK, -1).astype(BF16)

    wkv = pr["mla_w_kv_up"].reshape(L, MLA_KV_RANK, MLA_HEADS, MLA_NOPE + MLA_V)
    wk = jnp.concatenate([wkv[..., :MLA_NOPE], jnp.zeros((L, MLA_KV_RANK, MLA_HEADS, HEAD_PAD - MLA_NOPE), F32)],
                         axis=3).reshape(L, MLA_KV_RANK, -1).astype(BF16)
    wv = wkv[..., MLA_NOPE:].reshape(L, MLA_KV_RANK, MLA_WIDTH).astype(BF16)

    pad = jnp.zeros((L, HEAD_PAD - MLA_QK_DIM), F32)
    gqm = row(jnp.concatenate([pr["mla_q_norm"], pad], axis=1) * (LOG2E / math.sqrt(MLA_QK_DIM)))
    gkm = row(jnp.concatenate([pr["mla_k_norm"], pad], axis=1))
    bf = row(jnp.concatenate([pr["fox_b_f"]] * N_SPLIT + [jnp.zeros((L, LANE - N_SPLIT * FOX_HEADS), F32)], axis=1))
    gqf = row(jnp.tile(pr["fox_q_norm"], (1, FOX_HEADS)) * (LOG2E / math.sqrt(FOX_HEAD_DIM)))
    gkf = row(jnp.tile(pr["fox_k_norm"], (1, FOX_HEADS)))
    return dict(ng=row(pr["norm_g"]), wcat=wcat, qan=row(pr["mla_q_a_norm"]), wqa=wqa, wqb=wqb,
                kvn=row(pr["mla_kv_a_norm"]), wk=wk, wv=wv, gqm=gqm, gkm=gkm, bf=bf, gqf=gqf, gkf=gkf)


def _prep_consts(tm):
    bd = np.kron(np.eye(PAIR // FOX_HEAD_DIM, dtype=np.float32), np.ones((FOX_HEAD_DIM, FOX_HEAD_DIM), np.float32))
    tri = np.tril(np.ones((tm, tm), np.float32))
    eq = np.zeros((LANE, N_PAIRS * LANE), np.float32)
    ek = np.zeros_like(eq)
    oq = np.zeros((1, N_PAIRS * LANE), np.float32)
    ok = np.zeros_like(oq)
    for hd in range(FOX_HEADS):
        p, a = divmod(hd, 2)
        base = p * LANE + 2 * N_SPLIT * a
        for i in range(N_SPLIT):
            ek[i * FOX_HEADS + hd, base + i] = -1.0
            oq[0, base + i] = 1.0
            eq[i * FOX_HEADS + hd, base + N_SPLIT + i] = 1.0
            ok[0, base + N_SPLIT + i] = 1.0
    return dict(bd=jnp.asarray(bd, BF16), tri=jnp.asarray(tri, BF16),
                eq=jnp.asarray(eq, BF16), ek=jnp.asarray(ek, BF16), oq=jnp.asarray(oq), ok=jnp.asarray(ok))


def _prep_call(x, ca, sb, w, c, l, tm):
    B, S, _ = x.shape
    tok = lambda width: pl.BlockSpec((1, tm, width), lambda b, s: (b, s, 0))
    stacked = [w["ng"], w["wcat"], w["qan"], w["wqa"], w["wqb"], w["kvn"], w["wk"], w["wv"],
               w["gqm"], w["gkm"], w["bf"], w["gqf"], w["gkf"]]
    consts = [c["bd"], c["tri"], c["eq"], c["ek"], c["oq"], c["ok"]]
    args = [x, ca, sb] + stacked + consts
    in_specs = ([tok(D_MODEL), tok(LANE), tok(LANE)] + [_layer(a, l) for a in stacked]
                + [_full(a.shape) for a in consts])
    widths = [MLA_HEADS * HEAD_PAD, MLA_HEADS * HEAD_PAD, N_PAIRS * PAIR, N_PAIRS * PAIR, N_PAIRS * PAIR, N_PAIRS * PAIR]
    out_shape = [jax.ShapeDtypeStruct((B, S, wd), BF16) for wd in widths]
    out_shape.append(jax.ShapeDtypeStruct((B, S, S5_WIDTH), F32))
    return pl.pallas_call(
        _prep_kernel,
        grid=(B, S // tm),
        in_specs=in_specs,
        out_specs=[tok(wd) for wd in widths] + [tok(S5_WIDTH)],
        out_shape=out_shape,
        scratch_shapes=[pltpu.VMEM((1, LANE), F32)],
        compiler_params=_cparams(("parallel", "arbitrary")),
        name="prep",
    )(*args)


def _diag_tile(tq):
    return min(tq, PAIR)


def _attn_kernel(q_ref, k_ref, v_ref, hm_ref, o_ref, qm_ref, sa_ref, sb_ref, bias_ref, m_ref, acc_ref, *,
                 tq, chunk_causal):
    S = q_ref.shape[1]
    nq = S // tq
    td = _diag_tile(tq)
    steps = [(qi, j) for qi in range(nq) for j in range(qi + 1)]
    row = lax.broadcasted_iota(jnp.int32, (td, td), 0)
    col = lax.broadcasted_iota(jnp.int32, (td, td), 1)
    shift = int(math.log2(CHUNK)) if chunk_causal else 0
    bias_ref[...] = jnp.where((col >> shift) <= (row >> shift), 0.0, -jnp.inf)
    for a in range(2):
        qm_ref[a] = q_ref[0] * hm_ref[a:a + 1, :]

    def row_blocks(qi, j):
        if j < qi:
            return [(0, tq, tq)]
        return [(r * td, td, (r + 1) * td) for r in range(tq // td)]

    def qk(qi, j, s_ref):
        for r0, nr, nk in row_blocks(qi, j):
            k = k_ref[0, j * tq:j * tq + nk, :]
            for a in range(2):
                q = qm_ref[a, qi * tq + r0:qi * tq + r0 + nr, :]
                s_ref[a, r0:r0 + nr, :nk] = lax.dot_general(q, k, (((1,), (1,)), ((), ())),
                                                            preferred_element_type=F32)

    qk(0, 0, sa_ref)
    bufs = (sa_ref, sb_ref)
    for t, (qi, j) in enumerate(steps):
        s_cur, s_next = bufs[t % 2], bufs[(t + 1) % 2]
        if t + 1 < len(steps):
            qk(*steps[t + 1], s_next)
        for r0, nr, nk in row_blocks(qi, j):
            rows = slice(r0, r0 + nr)
            lane = lax.broadcasted_iota(jnp.int32, (nr, LANE), 1)
            outs = []
            for a in range(2):
                s = s_cur[a, rows, :nk]
                if j == qi:
                    tail = s[:, nk - td:] + bias_ref[...]
                    s = tail if nk == td else jnp.concatenate([s[:, :nk - td], tail], axis=1)
                v = v_ref[0, j * tq:j * tq + nk, a * LANE:(a + 1) * LANE]
                m_cur = jnp.max(s, axis=-1, keepdims=True)
                if j == 0:
                    m_new = jnp.broadcast_to(m_cur, (nr, LANE))
                    p = jnp.exp2(s - m_cur)
                    acc = _dot(p.astype(BF16), v)
                else:
                    m_prev = m_ref[a, rows]
                    m_new = jnp.maximum(m_prev, m_cur)
                    alpha = jnp.exp2(m_prev - m_new)
                    p = jnp.exp2(s - jnp.concatenate([m_new] * (nk // LANE), axis=1))
                    acc = alpha * acc_ref[a, rows] + _dot(p.astype(BF16), v)
                if j == qi:
                    outs.append(acc / pltpu.roll(acc, LANE // 2, axis=1))
                else:
                    acc_ref[a, rows] = acc
                    m_ref[a, rows] = m_new
            if j == qi:
                out = jnp.where(lane < LANE // 2, outs[0], outs[1]).astype(o_ref.dtype)
                o_ref[0, qi * tq + r0:qi * tq + r0 + nr, :] = out


def _attn_call(q, k, v, head_mask, chunk_causal, tq, name):
    B, S, _ = q.shape
    kern = functools.partial(_attn_kernel, tq=tq, chunk_causal=chunk_causal)
    td = _diag_tile(tq)
    return pl.pallas_call(
        kern,
        grid=(B, N_PAIRS),
        in_specs=[pl.BlockSpec((1, S, PAIR), lambda b, p: (b, 0, p)),
                  pl.BlockSpec((1, S, PAIR), lambda b, p: (b, 0, p)),
                  pl.BlockSpec((1, S, PAIR), lambda b, p: (b, 0, p)),
                  _full((2, PAIR))],
        out_specs=pl.BlockSpec((1, S, LANE), lambda b, p: (b, 0, p)),
        out_shape=jax.ShapeDtypeStruct((B, S, N_PAIRS * LANE), BF16),
        scratch_shapes=[pltpu.VMEM((2, S, PAIR), BF16), pltpu.VMEM((2, tq, tq), F32), pltpu.VMEM((2, tq, tq), F32),
                        pltpu.VMEM((td, td), F32), pltpu.VMEM((2, tq, LANE), F32), pltpu.VMEM((2, tq, LANE), F32)],
        compiler_params=_cparams(("parallel", "parallel")),
        name=name,
    )(q, k, v, head_mask)


def _head_masks():
    mla = np.zeros((2, PAIR), np.float32)
    fox = np.zeros((2, PAIR), np.float32)
    for a in range(2):
        mla[a, a * HEAD_PAD:(a + 1) * HEAD_PAD] = 1.0
        fox[a, a * FOX_HEAD_DIM:(a + 1) * FOX_HEAD_DIM] = 1.0
        fox[a, LANE + 2 * N_SPLIT * a:LANE + 2 * N_SPLIT * (a + 1)] = 1.0
    return jnp.asarray(mla, BF16), jnp.asarray(fox, BF16)


def _s5_param_kernel(lr_ref, li_ref, ldt_ref, br_ref, bi_ref, are_ref, aim_ref, bbr_ref, bbi_ref):
    lr = lr_ref[...]
    li = li_ref[...]
    dt = jnp.exp(ldt_ref[...])
    mag = jnp.exp(lr * dt)
    a_re = mag * jnp.cos(li * dt)
    a_im = mag * jnp.sin(li * dt)
    den = lr * lr + li * li
    f_re = ((a_re - 1.0) * lr + a_im * li) / den
    f_im = (a_im * lr - (a_re - 1.0) * li) / den
    br = br_ref[...]
    bi = bi_ref[...]
    are_ref[...] = a_re
    aim_ref[...] = a_im
    bbr_ref[...] = f_re * br - f_im * bi
    bbi_ref[...] = f_re * bi + f_im * br


def _s5_params(pr, n_batch):
    G, N, C = S5_GROUPS, S5_STATE, S5_GROUP
    L = pr["s5_lambda_re"].shape[0]
    rep = lambda a: jnp.repeat(a.reshape(L * G, N), C, axis=0)
    ldt = jnp.broadcast_to(pr["s5_log_dt"][:, :, None], (L, G, N))
    tr = lambda b: jnp.transpose(b, (0, 1, 3, 2)).reshape(L * G * C, N)
    shp = jax.ShapeDtypeStruct((L * G * C, N), F32)
    a_re, a_im, bb_re, bb_im = pl.pallas_call(
        _s5_param_kernel, out_shape=[shp] * 4, name="s5_params",
    )(rep(pr["s5_lambda_re"]), rep(pr["s5_lambda_im"]), rep(ldt), tr(pr["s5_b_re"]), tr(pr["s5_b_im"]))
    n_tiles = G * N // PAIR
    gpt = PAIR // N
    t_idx = np.arange(n_tiles)[:, None, None]
    mask_b = jnp.asarray(np.arange(LANE)[None, :, None] // C == gpt * (t_idx % 2) + np.arange(PAIR)[None, None, :] // N)
    mask_c = jnp.asarray(np.arange(PAIR)[None, None, :] // C
                         == gpt * (t_idx % S5_GROUP_TILES) + np.arange(PAIR)[None, :, None] // N)

    def group_major(re, im):
        split = lambda a: a.reshape(L, S5_SCAN_GROUPS, S5_GROUP_TILES, *a.shape[2:])
        both = jnp.concatenate([split(re), split(im)], axis=2)
        return both.reshape(L, 2 * n_tiles, *re.shape[2:]).astype(BF16)

    def b_tiles(bb):
        blocks = jnp.repeat(bb.reshape(L, G * C // LANE, LANE, N), 2, axis=1)
        return jnp.where(mask_b, jnp.tile(blocks, (1, 1, 1, gpt)), 0.0)

    def c_tiles(cc):
        rows = jnp.transpose(cc, (0, 1, 3, 2)).reshape(L, n_tiles, PAIR, C)
        return jnp.where(mask_c, jnp.tile(rows, (1, 1, 1, PAIR // C)), 0.0)

    w_b = group_major(b_tiles(bb_re), b_tiles(bb_im))
    w_c = group_major(c_tiles(pr["s5_c_re"]), -c_tiles(pr["s5_c_im"]))
    first = lambda a: a.reshape(L, G, C, N)[:, :, 0, :].reshape(L, S5_SCAN_GROUPS, S5_GROUP_TILES, 1, PAIR)
    a_rep = jnp.broadcast_to(jnp.concatenate([first(a_re), first(a_im)], axis=2),
                             (L, S5_SCAN_GROUPS, 2 * S5_GROUP_TILES, n_batch, PAIR))
    return dict(a=a_rep, w_b=w_b, w_c=w_c,
                d=pr["s5_d"][:, None, :], w_glu=pr["s5_w_glu"].astype(BF16), b_glu=pr["s5_b_glu"][:, None, :])


def _s5_kernel(u_ref, a_ref, wb_ref, wc_ref, d_ref, wg_ref, bg_ref, o_ref, st_ref, us_ref, zs_ref, ub_ref, uf_ref,
               y_ref, bu0_ref, bu1_ref, x0_ref, x1_ref, *, nb, ts, n_sub):
    @pl.when(pl.program_id(0) == 0)
    def _():
        st_ref[...] = jnp.zeros_like(st_ref)

    pitch = ts + S5_ROW_PAD
    nk = S5_WIDTH // LANE
    gt = S5_GROUP_TILES
    n_iter = ts // S5_SCAN_UNROLL
    tiles_per_iter = 2 * gt // n_iter
    steps_per_tile = S5_SCAN_UNROLL // tiles_per_iter
    bu = (bu0_ref, bu1_ref)
    xs = (x0_ref, x1_ref)
    everything = slice(None)

    def stage_u(c):
        for b in range(nb):
            for k in range(nk):
                us_ref[k, b * pitch:b * pitch + ts, :] = u_ref[b, c * ts:(c + 1) * ts, k * LANE:(k + 1) * LANE]
        u = jnp.concatenate(
            [jnp.concatenate([us_ref[k, pl.ds(t, nb, stride=pitch), :] for k in range(nk)], axis=1) for t in range(ts)],
            axis=0)
        uf_ref[c % 2] = u
        for k in range(nk):
            ub_ref[k] = u[:, k * LANE:(k + 1) * LANE].astype(BF16)

    def b_item(g, k, rows):
        cb = (gt * g) // 2 + (k % gt) // 2
        bu[g][k, rows, :] = _dot(ub_ref[cb, rows, :], wb_ref[2 * gt * g + k])

    def c_item(g, k, rows):
        y_ref[g, rows, :] += _dot(xs[g][k, rows, :].astype(BF16), wc_ref[2 * gt * g + k])

    def scan_phase(g, b_target, c_source):
        ar = a_ref[g, :gt]
        ai = a_ref[g, gt:]
        if c_source is not None:
            y_ref[c_source] = jnp.zeros(y_ref.shape[1:], F32)

        def body(i, carry):
            xr, xi = carry
            for j in range(tiles_per_iter):
                k = i * tiles_per_iter + j
                if b_target is not None:
                    b_item(b_target, k, everything)
                if c_source is not None:
                    c_item(c_source, k, everything)
                for tt in range(steps_per_tile):
                    t = i * S5_SCAN_UNROLL + j * steps_per_tile + tt
                    r = pl.ds(t * nb, nb) if n_iter == 1 else pl.ds(pl.multiple_of(t * nb, nb), nb)
                    nr = ar * xr - ai * xi + bu[g][:gt, r, :]
                    ni = ar * xi + ai * xr + bu[g][gt:, r, :]
                    xs[g][:gt, r, :] = nr
                    xs[g][gt:, r, :] = ni
                    xr, xi = nr, ni
            return xr, xi

        carry = (st_ref[g, :gt], st_ref[g, gt:])
        xr, xi = body(0, carry) if n_iter == 1 else lax.fori_loop(0, n_iter, body, carry)
        st_ref[g, :gt] = xr
        st_ref[g, gt:] = xi

    def epilogue(c):
        y = jnp.concatenate([y_ref[g] for g in range(S5_SCAN_GROUPS)], axis=1) + d_ref[...] * uf_ref[c % 2]
        z = jax.nn.gelu(y, approximate=True)
        z = z * jax.nn.sigmoid(_dot(z.astype(BF16), wg_ref[...]) + bg_ref[...])
        for t in range(ts):
            for k in range(nk):
                zs_ref[k, pl.ds(t, nb, stride=pitch), :] = z[t * nb:(t + 1) * nb, k * LANE:(k + 1) * LANE]
        for b in range(nb):
            rows = jnp.concatenate([zs_ref[k, b * pitch:b * pitch + ts, :] for k in range(nk)], axis=1)
            o_ref[b, c * ts:(c + 1) * ts, :] = rows.astype(o_ref.dtype)

    stage_u(0)
    for k in range(2 * gt):
        b_item(0, k, everything)
    for c in range(n_sub):
        scan_phase(0, 1, 1 if c > 0 else None)
        if c > 0:
            epilogue(c - 1)
        if c + 1 < n_sub:
            stage_u(c + 1)
        scan_phase(1, 0 if c + 1 < n_sub else None, 0)
    y_ref[1] = jnp.zeros(y_ref.shape[1:], F32)
    for k in range(2 * gt):
        c_item(1, k, everything)
    epilogue(n_sub - 1)


def _s5_call(u, p, l, ts, n_sub):
    B, S, _ = u.shape
    blk = ts * n_sub
    rows = ts * B
    args = [u, p["a"], p["w_b"], p["w_c"], p["d"], p["w_glu"], p["b_glu"]]
    kern = functools.partial(_s5_kernel, nb=B, ts=ts, n_sub=n_sub)
    once = lambda a: pl.BlockSpec((None,) + a.shape[1:], lambda i: (l,) + (0,) * (a.ndim - 1),
                                  pipeline_mode=pl.Buffered(1))
    slab_buf = pltpu.VMEM((S5_WIDTH // LANE, B * (ts + S5_ROW_PAD), LANE), F32)
    tile_buf = pltpu.VMEM((2 * S5_GROUP_TILES, rows, PAIR), F32)
    return pl.pallas_call(
        kern,
        grid=(S // blk,),
        in_specs=[pl.BlockSpec((B, blk, S5_WIDTH), lambda i: (0, i, 0))] + [once(a) for a in args[1:]],
        out_specs=pl.BlockSpec((B, blk, S5_WIDTH), lambda i: (0, i, 0)),
        out_shape=jax.ShapeDtypeStruct((B, S, S5_WIDTH), BF16),
        scratch_shapes=[pltpu.VMEM((S5_SCAN_GROUPS, 2 * S5_GROUP_TILES, B, PAIR), F32), slab_buf, slab_buf,
                        pltpu.VMEM((S5_WIDTH // LANE, rows, LANE), BF16), pltpu.VMEM((2, rows, S5_WIDTH), F32),
                        pltpu.VMEM((S5_SCAN_GROUPS, rows, PAIR), F32), tile_buf, tile_buf, tile_buf, tile_buf],
        compiler_params=_cparams(("arbitrary",)),
        name="s5",
    )(*args)


def _merge_kernel(x_ref, ym_ref, yf_ref, ys_ref, ng_ref, wgm_ref, wo_ref, wout_ref, o_ref):
    x = x_ref[0]
    h = _rms(x, ng_ref[...]).astype(BF16)
    merged = None
    for b, y_ref in enumerate((ym_ref, yf_ref, ys_ref)):
        g = _dot_nt(h, wgm_ref[b * MLA_WIDTH:(b + 1) * MLA_WIDTH, :])
        gated = (y_ref[0].astype(F32) * (g * jax.nn.sigmoid(g))).astype(BF16)
        o = _dot(gated, wo_ref[b * MLA_WIDTH:(b + 1) * MLA_WIDTH, :])
        m = _dot_nt(h, wgm_ref[BRANCH_WIDTH + b * D_MODEL:BRANCH_WIDTH + (b + 1) * D_MODEL, :])
        term = jax.nn.sigmoid(m) * o
        merged = term if merged is None else merged + term
    o_ref[0] = x + _dot(merged.astype(BF16), wout_ref[...])


def _merge_weights(pr, wgm):
    return [pr["norm_g"][:, None, :], wgm, pr["w_branch_out"].astype(BF16), pr["w_out"].astype(BF16)]


def _merge_call(x, y_mla, y_fox, y_s5, w, l, tm):
    B, S, _ = x.shape
    tok = lambda width: pl.BlockSpec((1, tm, width), lambda b, s: (b, s, 0))
    args = [x, y_mla, y_fox, y_s5] + w
    return pl.pallas_call(
        _merge_kernel,
        grid=(B, S // tm),
        in_specs=([tok(D_MODEL), tok(MLA_WIDTH), tok(FOX_WIDTH), tok(S5_WIDTH)]
                  + [_layer(a, l, single_buffer=True) for a in w]),
        out_specs=tok(D_MODEL),
        out_shape=jax.ShapeDtypeStruct(x.shape, x.dtype),
        compiler_params=_cparams(("parallel", "parallel")),
        name="merge",
    )(*args)


_LAYER_KEYS = ("norm_g", "w_in", "mla_q_a_norm", "mla_w_q_up", "mla_kv_a_norm", "mla_w_kv_up", "mla_q_norm",
               "mla_k_norm", "fox_b_f", "fox_q_norm", "fox_k_norm", "s5_lambda_re", "s5_lambda_im", "s5_log_dt",
               "s5_b_re", "s5_b_im", "s5_c_re", "s5_c_im", "s5_d", "s5_w_glu", "s5_b_glu", "w_branch_out", "w_out")


def _forward(x, positions, params):
    B, S, _ = x.shape
    tm = min(S, 512)
    tq = min(S, 1024)
    ts = min(S, 64)
    depth = params["w_in"].shape[0]
    ca, sb = _rope_tables(positions)
    consts = _prep_consts(tm)
    mask_mla, mask_fox = _head_masks()
    wcat, wgm = _pack_w_in(params["w_in"])
    w_prep = _prep_weights(params, wcat)
    w_s5 = _s5_params(params, B)
    w_merge = _merge_weights(params, wgm)
    h = x
    for l in range(depth):
        qm, km, vm, qf, kf, vf, u = _prep_call(h, ca, sb, w_prep, consts, l, tm)
        y_mla = _attn_call(qm, km, vm, mask_mla, True, tq, "attn_mla")
        y_fox = _attn_call(qf, kf, vf, mask_fox, False, tq, "attn_fox")
        y_s5 = _s5_call(u, w_s5, l, ts, min(S // ts, 4))
        h = _merge_call(h, y_mla, y_fox, y_s5, w_merge, l, min(S, 2 * tm))
    return h


def kernel(x, positions, norm_g, w_in, mla_q_a_norm, mla_w_q_up, mla_kv_a_norm, mla_w_kv_up, mla_q_norm, mla_k_norm, fox_b_f, fox_q_norm, fox_k_norm, s5_lambda_re, s5_lambda_im, s5_log_dt, s5_b_re, s5_b_im, s5_c_re, s5_c_im, s5_d, s5_w_glu, s5_b_glu, w_branch_out, w_out):
    params = dict(zip(_LAYER_KEYS, (norm_g, w_in, mla_q_a_norm, mla_w_q_up, mla_kv_a_norm, mla_w_kv_up, mla_q_norm,
                                    mla_k_norm, fox_b_f, fox_q_norm, fox_k_norm, s5_lambda_re, s5_lambda_im,
                                    s5_log_dt, s5_b_re, s5_b_im, s5_c_re, s5_c_im, s5_d, s5_w_glu, s5_b_glu,
                                    w_branch_out, w_out)))
    return _forward(x, positions, params)
```

```python
import functools
import math

import numpy as np
import jax
import jax.numpy as jnp
from jax import lax
from jax.experimental import pallas as pl
from jax.experimental.pallas import tpu as pltpu

F32 = jnp.float32
BF16 = jnp.bfloat16

D_MODEL = 1024
CHUNK = 64
EPS = 1e-6

MLA_HEADS = 8
MLA_NOPE = 64
MLA_ROPE = 32
MLA_V = 64
MLA_Q_RANK = 256
MLA_KV_RANK = 128
MLA_WIDTH = MLA_HEADS * MLA_V
MLA_QK_DIM = MLA_NOPE + MLA_ROPE
ROPE_THETA = 10000.0

FOX_HEADS = 8
FOX_HEAD_DIM = 64
FOX_WIDTH = FOX_HEADS * FOX_HEAD_DIM

S5_WIDTH = 512
S5_GROUP = 16
S5_GROUPS = S5_WIDTH // S5_GROUP
S5_STATE = 64
S5_ROW_PAD = 8
S5_SCAN_GROUPS = 2
S5_GROUP_TILES = 4
S5_SCAN_UNROLL = 64

BRANCH_WIDTH = MLA_WIDTH + FOX_WIDTH + S5_WIDTH

LANE = 128
HEAD_PAD = 128
PAIR = 2 * HEAD_PAD
N_PAIRS = 4
N_SPLIT = 3
LOG2E = math.log2(math.e)

_OFF = np.cumsum([0, MLA_Q_RANK, MLA_KV_RANK, MLA_ROPE, FOX_WIDTH, FOX_WIDTH, FOX_WIDTH, FOX_HEADS,
                  S5_WIDTH, MLA_WIDTH, FOX_WIDTH, S5_WIDTH, D_MODEL, D_MODEL, D_MODEL]).tolist()
(O_CQ, O_CKV, O_KPE, O_FQ, O_FK, O_FV, O_FF, O_S5U, O_GATE, _o1, _o2, O_MERGE, _o3, _o4, O_END) = _OFF

C_CQ = 0
C_CKV = C_CQ + MLA_Q_RANK
C_KPA = C_CKV + MLA_KV_RANK
C_KPB = C_KPA + LANE
C_FQ = C_KPB + LANE
C_FK = C_FQ + FOX_WIDTH
C_FV = C_FK + FOX_WIDTH
C_FF = C_FV + FOX_WIDTH
C_S5 = C_FF + LANE
C_END = C_S5 + S5_WIDTH

VMEM_LIMIT = 56 * 1024 * 1024


def _cparams(sem):
    return pltpu.CompilerParams(dimension_semantics=sem, vmem_limit_bytes=VMEM_LIMIT)


def _full(shape):
    n = len(shape)
    return pl.BlockSpec(shape, lambda *_: (0,) * n)


def _rms(x, g):
    return x * lax.rsqrt(jnp.mean(x * x, axis=-1, keepdims=True) + EPS) * g


def _dot(a, b):
    return jnp.dot(a, b, preferred_element_type=F32)


def _dot_nt(a, b):
    return lax.dot_general(a, b, (((1,), (1,)), ((), ())), preferred_element_type=F32)


def _store_v_with_ones(v_ref, v):
    lane = lax.broadcasted_iota(jnp.int32, (v.shape[0], LANE), 1)
    for p in range(N_PAIRS):
        vp = v[:, p * LANE:(p + 1) * LANE]
        v_ref[0, :, p * PAIR:p * PAIR + LANE] = jnp.where(lane < LANE // 2, vp, 1.0).astype(v_ref.dtype)
        v_ref[0, :, p * PAIR + LANE:(p + 1) * PAIR] = jnp.where(lane < LANE // 2, 1.0, vp).astype(v_ref.dtype)


def _rope_kernel(pos_ref, inv_ref, cos_ref, sin_ref, nsin_ref):
    ang = pos_ref[...].astype(F32) * inv_ref[...]
    sin = jnp.sin(ang)
    cos_ref[...] = jnp.cos(ang)
    sin_ref[...] = sin
    nsin_ref[...] = -sin


def _rope_tables(positions):
    B, S = positions.shape
    half = MLA_ROPE // 2
    per_row = LANE // half
    inv = 1.0 / (ROPE_THETA ** (jnp.arange(0, MLA_ROPE, 2, dtype=F32) / MLA_ROPE))
    rows = B * S // per_row
    pos = jnp.broadcast_to(positions.reshape(rows, per_row, 1), (rows, per_row, half)).reshape(rows, LANE)
    tr = min(rows, 1024)
    blk = pl.BlockSpec((tr, LANE), lambda r: (r, 0))
    cos, sin, nsin = (t.reshape(B, S, half) for t in pl.pallas_call(
        _rope_kernel,
        grid=(rows // tr,),
        in_specs=[blk, _full((1, LANE))],
        out_specs=[blk, blk, blk],
        out_shape=[jax.ShapeDtypeStruct((rows, LANE), F32)] * 3,
        compiler_params=_cparams(("parallel",)),
        name="rope_tables",
    )(pos, jnp.tile(inv, per_row)[None]))
    lead = jnp.ones((B, S, MLA_NOPE), F32)
    tail = jnp.zeros((B, S, LANE - MLA_NOPE - MLA_ROPE), F32)
    ca = jnp.concatenate([lead, cos, cos, tail], axis=-1)
    sb = jnp.concatenate([jnp.zeros_like(lead), nsin, sin, tail], axis=-1)
    return ca, sb


def _split3(x):
    hi = x.astype(BF16)
    r1 = x - hi.astype(F32)
    mid = r1.astype(BF16)
    return hi, mid, (r1 - mid.astype(F32)).astype(BF16)


def _fox_qk_norm(x, bd, gain):
    sq = (x * x).astype(BF16)
    ss = jnp.concatenate([_dot(sq[:, c:c + PAIR], bd) for c in range(0, FOX_WIDTH, PAIR)], axis=1)
    return (x * lax.rsqrt(ss * (1.0 / FOX_HEAD_DIM) + EPS) * gain).astype(BF16)


def _prep_kernel(x_ref, ca_ref, sb_ref, ng_ref, wcat_ref, qan_ref, wqa_ref, wqb_ref, kvn_ref, wk_ref, wv_ref,
                 gqm_ref, gkm_ref, bf_ref, gqf_ref, gkf_ref, bd_ref, tri_ref,
                 eq_ref, ek_ref, oq_ref, ok_ref,
                 qm_ref, km_ref, vm_ref, qf_ref, kf_ref, vf_ref, u_ref, carry_ref):
    @pl.when(pl.program_id(1) == 0)
    def _():
        carry_ref[...] = jnp.zeros_like(carry_ref)

    x = x_ref[0]
    h = _rms(x, ng_ref[...]).astype(BF16)
    proj = _dot_nt(h, wcat_ref[...])
    ca = ca_ref[0]
    sb = sb_ref[0]

    cqn = _rms(proj[:, C_CQ:C_CQ + MLA_Q_RANK], qan_ref[...]).astype(BF16)
    qa = _dot(cqn, wqa_ref[...])
    qb = _dot(cqn, wqb_ref[...])
    ckvn = _rms(proj[:, C_CKV:C_CKV + MLA_KV_RANK], kvn_ref[...]).astype(BF16)
    kc = _dot(ckvn, wk_ref[...])
    _store_v_with_ones(vm_ref, _dot(ckvn, wv_ref[...]))
    kpe = proj[:, C_KPA:C_KPA + LANE] * ca + proj[:, C_KPB:C_KPB + LANE] * sb
    inv_d = 1.0 / MLA_QK_DIM
    for hd in range(MLA_HEADS):
        sl = slice(hd * HEAD_PAD, (hd + 1) * HEAD_PAD)
        qh = qa[:, sl] * ca + qb[:, sl] * sb
        ss = jnp.sum(qh * qh, axis=-1, keepdims=True)
        qm_ref[0, :, sl] = (qh * lax.rsqrt(ss * inv_d + EPS) * gqm_ref[...]).astype(BF16)
        kh = kc[:, sl] + kpe
        ss = jnp.sum(kh * kh, axis=-1, keepdims=True)
        km_ref[0, :, sl] = (kh * lax.rsqrt(ss * inv_d + EPS) * gkm_ref[...]).astype(BF16)

    fqn = _fox_qk_norm(proj[:, C_FQ:C_FQ + FOX_WIDTH], bd_ref[...], gqf_ref[...])
    fkn = _fox_qk_norm(proj[:, C_FK:C_FK + FOX_WIDTH], bd_ref[...], gkf_ref[...])
    _store_v_with_ones(vf_ref, proj[:, C_FV:C_FV + FOX_WIDTH])

    z = proj[:, C_FF:C_FF + LANE] + bf_ref[...]
    log_f = jnp.minimum(z, 0.0) - jnp.log1p(jnp.exp(-jnp.abs(z)))
    sums = _dot(tri_ref[...], jnp.concatenate(_split3(log_f), axis=1))
    cum = sums[:, :LANE] + sums[:, LANE:2 * LANE] + sums[:, 2 * LANE:] + carry_ref[...]
    carry_ref[...] = cum[cum.shape[0] - 1:, :]
    hi, mid, lo = _split3(cum * LOG2E)
    lane = lax.broadcasted_iota(jnp.int32, hi.shape, 1)
    pieces = jnp.where(lane < FOX_HEADS, hi, jnp.where(lane < 2 * FOX_HEADS, mid, lo))
    auxq = (_dot(pieces, eq_ref[...]) + oq_ref[...]).astype(BF16)
    auxk = (_dot(pieces, ek_ref[...]) + ok_ref[...]).astype(BF16)
    for p in range(N_PAIRS):
        src = slice(p * LANE, (p + 1) * LANE)
        qf_ref[0, :, p * PAIR:p * PAIR + LANE] = fqn[:, src]
        qf_ref[0, :, p * PAIR + LANE:(p + 1) * PAIR] = auxq[:, src]
        kf_ref[0, :, p * PAIR:p * PAIR + LANE] = fkn[:, src]
        kf_ref[0, :, p * PAIR + LANE:(p + 1) * PAIR] = auxk[:, src]

    u_ref[0] = proj[:, C_S5:C_S5 + S5_WIDTH]


def _layer(arr, l, single_buffer=False):
    n = arr.ndim - 1
    mode = dict(pipeline_mode=pl.Buffered(1)) if single_buffer else {}
    return pl.BlockSpec((None,) + arr.shape[1:], lambda *_: (l,) + (0,) * n, **mode)


def _pack_w_in(w_in):
    wt = jnp.transpose(w_in, (0, 2, 1))
    L = wt.shape[0]
    half = MLA_ROPE // 2
    zeros = lambda n: jnp.zeros((L, n, D_MODEL), F32)
    kpe_lo = wt[:, O_KPE:O_KPE + half]
    kpe_hi = wt[:, O_KPE + half:O_KPE + MLA_ROPE]
    tail = zeros(LANE - MLA_NOPE - MLA_ROPE)
    ff = wt[:, O_FF:O_FF + FOX_HEADS]
    pieces = [wt[:, O_CQ:O_CQ + MLA_Q_RANK + MLA_KV_RANK],
              zeros(MLA_NOPE), kpe_lo, kpe_hi, tail,
              zeros(MLA_NOPE), kpe_hi, kpe_lo, tail,
              wt[:, O_FQ:O_FQ + 3 * FOX_WIDTH],
              ff, ff, ff, zeros(LANE - N_SPLIT * FOX_HEADS),
              wt[:, O_S5U:O_S5U + S5_WIDTH]]
    return jnp.concatenate(pieces, axis=1).astype(BF16), wt[:, O_GATE:O_END].astype(BF16)


def _prep_weights(pr, wcat):
    L = wcat.shape[0]
    row = lambda a: a[:, None, :]
    half = MLA_ROPE // 2

    wq = pr["mla_w_q_up"].reshape(L, MLA_Q_RANK, MLA_HEADS, MLA_QK_DIM)
    nope, pe = wq[..., :MLA_NOPE], wq[..., MLA_NOPE:]
    pe_sw = jnp.concatenate([pe[..., half:], pe[..., :half]], axis=3)
    z32 = jnp.zeros((L, MLA_Q_RANK, MLA_HEADS, HEAD_PAD - MLA_QK_DIM), F32)
    wqa = jnp.concatenate([nope, pe, z32], axis=3).reshape(L, MLA_Q_RANK, MLA_HEADS * HEAD_PAD).astype(BF16)
    wqb = jnp.concatenate([jnp.zeros_like(nope), pe_sw, z32], axis=3).reshape(L, MLA_Q_RANK, -1).astype(BF16)

    wkv = pr["mla_w_kv_up"].reshape(L, MLA_KV_RANK, MLA_HEADS, MLA_NOPE + MLA_V)
    wk = jnp.concatenate([wkv[..., :MLA_NOPE], jnp.zeros((L, MLA_KV_RANK, MLA_HEADS, HEAD_PAD - MLA_NOPE), F32)],
                         axis=3).reshape(L, MLA_KV_RANK, -1).astype(BF16)
    wv = wkv[..., MLA_NOPE:].reshape(L, MLA_KV_RANK, MLA_WIDTH).astype(BF16)

    pad = jnp.zeros((L, HEAD_PAD - MLA_QK_DIM), F32)
    gqm = row(jnp.concatenate([pr["mla_q_norm"], pad], axis=1) * (LOG2E / math.sqrt(MLA_QK_DIM)))
    gkm = row(jnp.concatenate([pr["mla_k_norm"], pad], axis=1))
    bf = row(jnp.concatenate([pr["fox_b_f"]] * N_SPLIT + [jnp.zeros((L, LANE - N_SPLIT * FOX_HEADS), F32)], axis=1))
    gqf = row(jnp.tile(pr["fox_q_norm"], (1, FOX_HEADS)) * (LOG2E / math.sqrt(FOX_HEAD_DIM)))
    gkf = row(jnp.tile(pr["fox_k_norm"], (1, FOX_HEADS)))
    return dict(ng=row(pr["norm_g"]), wcat=wcat, qan=row(pr["mla_q_a_norm"]), wqa=wqa, wqb=wqb,
                kvn=row(pr["mla_kv_a_norm"]), wk=wk, wv=wv, gqm=gqm, gkm=gkm, bf=bf, gqf=gqf, gkf=gkf)


def _prep_consts(tm):
    bd = np.kron(np.eye(PAIR // FOX_HEAD_DIM, dtype=np.float32), np.ones((FOX_HEAD_DIM, FOX_HEAD_DIM), np.float32))
    tri = np.tril(np.ones((tm, tm), np.float32))
    eq = np.zeros((LANE, N_PAIRS * LANE), np.float32)
    ek = np.zeros_like(eq)
    oq = np.zeros((1, N_PAIRS * LANE), np.float32)
    ok = np.zeros_like(oq)
    for hd in range(FOX_HEADS):
        p, a = divmod(hd, 2)
        base = p * LANE + 2 * N_SPLIT * a
        for i in range(N_SPLIT):
            ek[i * FOX_HEADS + hd, base + i] = -1.0
            oq[0, base + i] = 1.0
            eq[i * FOX_HEADS + hd, base + N_SPLIT + i] = 1.0
            ok[0, base + N_SPLIT + i] = 1.0
    return dict(bd=jnp.asarray(bd, BF16), tri=jnp.asarray(tri, BF16),
                eq=jnp.asarray(eq, BF16), ek=jnp.asarray(ek, BF16), oq=jnp.asarray(oq), ok=jnp.asarray(ok))


def _prep_call(x, ca, sb, w, c, l, tm):
    B, S, _ = x.shape
    tok = lambda width: pl.BlockSpec((1, tm, width), lambda b, s: (b, s, 0))
    stacked = [w["ng"], w["wcat"], w["qan"], w["wqa"], w["wqb"], w["kvn"], w["wk"], w["wv"],
               w["gqm"], w["gkm"], w["bf"], w["gqf"], w["gkf"]]
    consts = [c["bd"], c["tri"], c["eq"], c["ek"], c["oq"], c["ok"]]
    args = [x, ca, sb] + stacked + consts
    in_specs = ([tok(D_MODEL), tok(LANE), tok(LANE)] + [_layer(a, l) for a in stacked]
                + [_full(a.shape) for a in consts])
    widths = [MLA_HEADS * HEAD_PAD, MLA_HEADS * HEAD_PAD, N_PAIRS * PAIR, N_PAIRS * PAIR, N_PAIRS * PAIR, N_PAIRS * PAIR]
    out_shape = [jax.ShapeDtypeStruct((B, S, wd), BF16) for wd in widths]
    out_shape.append(jax.ShapeDtypeStruct((B, S, S5_WIDTH), F32))
    return pl.pallas_call(
        _prep_kernel,
        grid=(B, S // tm),
        in_specs=in_specs,
        out_specs=[tok(wd) for wd in widths] + [tok(S5_WIDTH)],
        out_shape=out_shape,
        scratch_shapes=[pltpu.VMEM((1, LANE), F32)],
        compiler_params=_cparams(("parallel", "arbitrary")),
        name="prep",
    )(*args)


def _diag_tile(tq):
    return min(tq, PAIR)


def _attn_kernel(q_ref, k_ref, v_ref, hm_ref, o_ref, qm_ref, sa_ref, sb_ref, bias_ref, m_ref, acc_ref, *,
                 tq, chunk_causal):
    S = q_ref.shape[1]
    nq = S // tq
    td = _diag_tile(tq)
    steps = [(qi, j) for qi in range(nq) for j in range(qi + 1)]
    row = lax.broadcasted_iota(jnp.int32, (td, td), 0)
    col = lax.broadcasted_iota(jnp.int32, (td, td), 1)
    shift = int(math.log2(CHUNK)) if chunk_causal else 0
    bias_ref[...] = jnp.where((col >> shift) <= (row >> shift), 0.0, -jnp.inf)
    for a in range(2):
        qm_ref[a] = q_ref[0] * hm_ref[a:a + 1, :]

    def row_blocks(qi, j):
        if j < qi:
            return [(0, tq, tq)]
        return [(r * td, td, (r + 1) * td) for r in range(tq // td)]

    def qk(qi, j, s_ref):
        for r0, nr, nk in row_blocks(qi, j):
            k = k_ref[0, j * tq:j * tq + nk, :]
            for a in range(2):
                q = qm_ref[a, qi * tq + r0:qi * tq + r0 + nr, :]
                s_ref[a, r0:r0 + nr, :nk] = lax.dot_general(q, k, (((1,), (1,)), ((), ())),
                                                            preferred_element_type=F32)

    qk(0, 0, sa_ref)
    bufs = (sa_ref, sb_ref)
    for t, (qi, j) in enumerate(steps):
        s_cur, s_next = bufs[t % 2], bufs[(t + 1) % 2]
        if t + 1 < len(steps):
            qk(*steps[t + 1], s_next)
        for r0, nr, nk in row_blocks(qi, j):
            rows = slice(r0, r0 + nr)
            lane = lax.broadcasted_iota(jnp.int32, (nr, LANE), 1)
            outs = []
            for a in range(2):
                s = s_cur[a, rows, :nk]
                if j == qi:
                    tail = s[:, nk - td:] + bias_ref[...]
                    s = tail if nk == td else jnp.concatenate([s[:, :nk - td], tail], axis=1)
                v = v_ref[0, j * tq:j * tq + nk, a * LANE:(a + 1) * LANE]
                m_cur = jnp.max(s, axis=-1, keepdims=True)
                if j == 0:
                    m_new = jnp.broadcast_to(m_cur, (nr, LANE))
                    p = jnp.exp2(s - m_cur)
                    acc = _dot(p.astype(BF16), v)
                else:
                    m_prev = m_ref[a, rows]
                    m_new = jnp.maximum(m_prev, m_cur)
                    alpha = jnp.exp2(m_prev - m_new)
                    p = jnp.exp2(s - jnp.concatenate([m_new] * (nk // LANE), axis=1))
                    acc = alpha * acc_ref[a, rows] + _dot(p.astype(BF16), v)
                if j == qi:
                    outs.append(acc / pltpu.roll(acc, LANE // 2, axis=1))
                else:
                    acc_ref[a, rows] = acc
                    m_ref[a, rows] = m_new
            if j == qi:
                out = jnp.where(lane < LANE // 2, outs[0], outs[1]).astype(o_ref.dtype)
                o_ref[0, qi * tq + r0:qi * tq + r0 + nr, :] = out


def _attn_call(q, k, v, head_mask, chunk_causal, tq, name):
    B, S, _ = q.shape
    kern = functools.partial(_attn_kernel, tq=tq, chunk_causal=chunk_causal)
    td = _diag_tile(tq)
    return pl.pallas_call(
        kern,
        grid=(B, N_PAIRS),
        in_specs=[pl.BlockSpec((1, S, PAIR), lambda b, p: (b, 0, p)),
                  pl.BlockSpec((1, S, PAIR), lambda b, p: (b, 0, p)),
                  pl.BlockSpec((1, S, PAIR), lambda b, p: (b, 0, p)),
                  _full((2, PAIR))],
        out_specs=pl.BlockSpec((1, S, LANE), lambda b, p: (b, 0, p)),
        out_shape=jax.ShapeDtypeStruct((B, S, N_PAIRS * LANE), BF16),
        scratch_shapes=[pltpu.VMEM((2, S, PAIR), BF16), pltpu.VMEM((2, tq, tq), F32), pltpu.VMEM((2, tq, tq), F32),
                        pltpu.VMEM((td, td), F32), pltpu.VMEM((2, tq, LANE), F32), pltpu.VMEM((2, tq, LANE), F32)],
        compiler_params=_cparams(("parallel", "parallel")),
        name=name,
    )(q, k, v, head_mask)


def _head_masks():
    mla = np.zeros((2, PAIR), np.float32)
    fox = np.zeros((2, PAIR), np.float32)
    for a in range(2):
        mla[a, a * HEAD_PAD:(a + 1) * HEAD_PAD] = 1.0
        fox[a, a * FOX_HEAD_DIM:(a + 1) * FOX_HEAD_DIM] = 1.0
        fox[a, LANE + 2 * N_SPLIT * a:LANE + 2 * N_SPLIT * (a + 1)] = 1.0
    return jnp.asarray(mla, BF16), jnp.asarray(fox, BF16)


def _s5_param_kernel(lr_ref, li_ref, ldt_ref, br_ref, bi_ref, are_ref, aim_ref, bbr_ref, bbi_ref):
    lr = lr_ref[...]
    li = li_ref[...]
    dt = jnp.exp(ldt_ref[...])
    mag = jnp.exp(lr * dt)
    a_re = mag * jnp.cos(li * dt)
    a_im = mag * jnp.sin(li * dt)
    den = lr * lr + li * li
    f_re = ((a_re - 1.0) * lr + a_im * li) / den
    f_im = (a_im * lr - (a_re - 1.0) * li) / den
    br = br_ref[...]
    bi = bi_ref[...]
    are_ref[...] = a_re
    aim_ref[...] = a_im
    bbr_ref[...] = f_re * br - f_im * bi
    bbi_ref[...] = f_re * bi + f_im * br


def _s5_params(pr, n_batch):
    G, N, C = S5_GROUPS, S5_STATE, S5_GROUP
    L = pr["s5_lambda_re"].shape[0]
    rep = lambda a: jnp.repeat(a.reshape(L * G, N), C, axis=0)
    ldt = jnp.broadcast_to(pr["s5_log_dt"][:, :, None], (L, G, N))
    tr = lambda b: jnp.transpose(b, (0, 1, 3, 2)).reshape(L * G * C, N)
    shp = jax.ShapeDtypeStruct((L * G * C, N), F32)
    a_re, a_im, bb_re, bb_im = pl.pallas_call(
        _s5_param_kernel, out_shape=[shp] * 4, name="s5_params",
    )(rep(pr["s5_lambda_re"]), rep(pr["s5_lambda_im"]), rep(ldt), tr(pr["s5_b_re"]), tr(pr["s5_b_im"]))
    n_tiles = G * N // PAIR
    gpt = PAIR // N
    t_idx = np.arange(n_tiles)[:, None, None]
    mask_b = jnp.asarray(np.arange(LANE)[None, :, None] // C == gpt * (t_idx % 2) + np.arange(PAIR)[None, None, :] // N)
    mask_c = jnp.asarray(np.arange(PAIR)[None, None, :] // C
                         == gpt * (t_idx % S5_GROUP_TILES) + np.arange(PAIR)[None, :, None] // N)

    def group_major(re, im):
        split = lambda a: a.reshape(L, S5_SCAN_GROUPS, S5_GROUP_TILES, *a.shape[2:])
        both = jnp.concatenate([split(re), split(im)], axis=2)
        return both.reshape(L, 2 * n_tiles, *re.shape[2:]).astype(BF16)

    def b_tiles(bb):
        blocks = jnp.repeat(bb.reshape(L, G * C // LANE, LANE, N), 2, axis=1)
        return jnp.where(mask_b, jnp.tile(blocks, (1, 1, 1, gpt)), 0.0)

    def c_tiles(cc):
        rows = jnp.transpose(cc, (0, 1, 3, 2)).reshape(L, n_tiles, PAIR, C)
        return jnp.where(mask_c, jnp.tile(rows, (1, 1, 1, PAIR // C)), 0.0)

    w_b = group_major(b_tiles(bb_re), b_tiles(bb_im))
    w_c = group_major(c_tiles(pr["s5_c_re"]), -c_tiles(pr["s5_c_im"]))
    first = lambda a: a.reshape(L, G, C, N)[:, :, 0, :].reshape(L, S5_SCAN_GROUPS, S5_GROUP_TILES, 1, PAIR)
    a_rep = jnp.broadcast_to(jnp.concatenate([first(a_re), first(a_im)], axis=2),
                             (L, S5_SCAN_GROUPS, 2 * S5_GROUP_TILES, n_batch, PAIR))
    return dict(a=a_rep, w_b=w_b, w_c=w_c,
                d=pr["s5_d"][:, None, :], w_glu=pr["s5_w_glu"].astype(BF16), b_glu=pr["s5_b_glu"][:, None, :])


def _s5_kernel(u_ref, a_ref, wb_ref, wc_ref, d_ref, wg_ref, bg_ref, o_ref, st_ref, us_ref, zs_ref, ub_ref, uf_ref,
               y_ref, bu0_ref, bu1_ref, x0_ref, x1_ref, *, nb, ts, n_sub):
    @pl.when(pl.program_id(0) == 0)
    def _():
        st_ref[...] = jnp.zeros_like(st_ref)

    pitch = ts + S5_ROW_PAD
    nk = S5_WIDTH // LANE
    gt = S5_GROUP_TILES
    n_iter = ts // S5_SCAN_UNROLL
    tiles_per_iter = 2 * gt // n_iter
    steps_per_tile = S5_SCAN_UNROLL // tiles_per_iter
    bu = (bu0_ref, bu1_ref)
    xs = (x0_ref, x1_ref)
    everything = slice(None)

    def stage_u(c):
        for b in range(nb):
            for k in range(nk):
                us_ref[k, b * pitch:b * pitch + ts, :] = u_ref[b, c * ts:(c + 1) * ts, k * LANE:(k + 1) * LANE]
        u = jnp.concatenate(
            [jnp.concatenate([us_ref[k, pl.ds(t, nb, stride=pitch), :] for k in range(nk)], axis=1) for t in range(ts)],
            axis=0)
        uf_ref[c % 2] = u
        for k in range(nk):
            ub_ref[k] = u[:, k * LANE:(k + 1) * LANE].astype(BF16)

    def b_item(g, k, rows):
        cb = (gt * g) // 2 + (k % gt) // 2
        bu[g][k, rows, :] = _dot(ub_ref[cb, rows, :], wb_ref[2 * gt * g + k])

    def c_item(g, k, rows):
        y_ref[g, rows, :] += _dot(xs[g][k, rows, :].astype(BF16), wc_ref[2 * gt * g + k])

    def scan_phase(g, b_target, c_source):
        ar = a_ref[g, :gt]
        ai = a_ref[g, gt:]
        if c_source is not None:
            y_ref[c_source] = jnp.zeros(y_ref.shape[1:], F32)

        def body(i, carry):
            xr, xi = carry
            for j in range(tiles_per_iter):
                k = i * tiles_per_iter + j
                if b_target is not None:
                    b_item(b_target, k, everything)
                if c_source is not None:
                    c_item(c_source, k, everything)
                for tt in range(steps_per_tile):
                    t = i * S5_SCAN_UNROLL + j * steps_per_tile + tt
                    r = pl.ds(t * nb, nb) if n_iter == 1 else pl.ds(pl.multiple_of(t * nb, nb), nb)
                    nr = ar * xr - ai * xi + bu[g][:gt, r, :]
                    ni = ar * xi + ai * xr + bu[g][gt:, r, :]
                    xs[g][:gt, r, :] = nr
                    xs[g][gt:, r, :] = ni
                    xr, xi = nr, ni
            return xr, xi

        carry = (st_ref[g, :gt], st_ref[g, gt:])
        xr, xi = body(0, carry) if n_iter == 1 else lax.fori_loop(0, n_iter, body, carry)
        st_ref[g, :gt] = xr
        st_ref[g, gt:] = xi

    def epilogue(c):
        y = jnp.concatenate([y_ref[g] for g in range(S5_SCAN_GROUPS)], axis=1) + d_ref[...] * uf_ref[c % 2]
        z = jax.nn.gelu(y, approximate=True)
        z = z * jax.nn.sigmoid(_dot(z.astype(BF16), wg_ref[...]) + bg_ref[...])
        for t in range(ts):
            for k in range(nk):
                zs_ref[k, pl.ds(t, nb, stride=pitch), :] = z[t * nb:(t + 1) * nb, k * LANE:(k + 1) * LANE]
        for b in range(nb):
            rows = jnp.concatenate([zs_ref[k, b * pitch:b * pitch + ts, :] for k in range(nk)], axis=1)
            o_ref[b, c * ts:(c + 1) * ts, :] = rows.astype(o_ref.dtype)

    stage_u(0)
    for k in range(2 * gt):
        b_item(0, k, everything)
    for c in range(n_sub):
        scan_phase(0, 1, 1 if c > 0 else None)
        if c > 0:
            epilogue(c - 1)
        if c + 1 < n_sub:
            stage_u(c + 1)
        scan_phase(1, 0 if c + 1 < n_sub else None, 0)
    y_ref[1] = jnp.zeros(y_ref.shape[1:], F32)
    for k in range(2 * gt):
        c_item(1, k, everything)
    epilogue(n_sub - 1)


def _s5_call(u, p, l, ts, n_sub):
    B, S, _ = u.shape
    blk = ts * n_sub
    rows = ts * B
    args = [u, p["a"], p["w_b"], p["w_c"], p["d"], p["w_glu"], p["b_glu"]]
    kern = functools.partial(_s5_kernel, nb=B, ts=ts, n_sub=n_sub)
    once = lambda a: pl.BlockSpec((None,) + a.shape[1:], lambda i: (l,) + (0,) * (a.ndim - 1),
                                  pipeline_mode=pl.Buffered(1))
    slab_buf = pltpu.VMEM((S5_WIDTH // LANE, B * (ts + S5_ROW_PAD), LANE), F32)
    tile_buf = pltpu.VMEM((2 * S5_GROUP_TILES, rows, PAIR), F32)
    return pl.pallas_call(
        kern,
        grid=(S // blk,),
        in_specs=[pl.BlockSpec((B, blk, S5_WIDTH), lambda i: (0, i, 0))] + [once(a) for a in args[1:]],
        out_specs=pl.BlockSpec((B, blk, S5_WIDTH), lambda i: (0, i, 0)),
        out_shape=jax.ShapeDtypeStruct((B, S, S5_WIDTH), BF16),
        scratch_shapes=[pltpu.VMEM((S5_SCAN_GROUPS, 2 * S5_GROUP_TILES, B, PAIR), F32), slab_buf, slab_buf,
                        pltpu.VMEM((S5_WIDTH // LANE, rows, LANE), BF16), pltpu.VMEM((2, rows, S5_WIDTH), F32),
                        pltpu.VMEM((S5_SCAN_GROUPS, rows, PAIR), F32), tile_buf, tile_buf, tile_buf, tile_buf],
        compiler_params=_cparams(("arbitrary",)),
        name="s5",
    )(*args)


def _merge_kernel(x_ref, ym_ref, yf_ref, ys_ref, ng_ref, wgm_ref, wo_ref, wout_ref, o_ref):
    x = x_ref[0]
    h = _rms(x, ng_ref[...]).astype(BF16)
    merged = None
    for b, y_ref in enumerate((ym_ref, yf_ref, ys_ref)):
        g = _dot_nt(h, wgm_ref[b * MLA_WIDTH:(b + 1) * MLA_WIDTH, :])
        gated = (y_ref[0].astype(F32) * (g * jax.nn.sigmoid(g))).astype(BF16)
        o = _dot(gated, wo_ref[b * MLA_WIDTH:(b + 1) * MLA_WIDTH, :])
        m = _dot_nt(h, wgm_ref[BRANCH_WIDTH + b * D_MODEL:BRANCH_WIDTH + (b + 1) * D_MODEL, :])
        term = jax.nn.sigmoid(m) * o
        merged = term if merged is None else merged + term
    o_ref[0] = x + _dot(merged.astype(BF16), wout_ref[...])


def _merge_weights(pr, wgm):
    return [pr["norm_g"][:, None, :], wgm, pr["w_branch_out"].astype(BF16), pr["w_out"].astype(BF16)]


def _merge_call(x, y_mla, y_fox, y_s5, w, l, tm):
    B, S, _ = x.shape
    tok = lambda width: pl.BlockSpec((1, tm, width), lambda b, s: (b, s, 0))
    args = [x, y_mla, y_fox, y_s5] + w
    return pl.pallas_call(
        _merge_kernel,
        grid=(B, S // tm),
        in_specs=([tok(D_MODEL), tok(MLA_WIDTH), tok(FOX_WIDTH), tok(S5_WIDTH)]
                  + [_layer(a, l, single_buffer=True) for a in w]),
        out_specs=tok(D_MODEL),
        out_shape=jax.ShapeDtypeStruct(x.shape, x.dtype),
        compiler_params=_cparams(("parallel", "parallel")),
        name="merge",
    )(*args)


_LAYER_KEYS = ("norm_g", "w_in", "mla_q_a_norm", "mla_w_q_up", "mla_kv_a_norm", "mla_w_kv_up", "mla_q_norm",
               "mla_k_norm", "fox_b_f", "fox_q_norm", "fox_k_norm", "s5_lambda_re", "s5_lambda_im", "s5_log_dt",
               "s5_b_re", "s5_b_im", "s5_c_re", "s5_c_im", "s5_d", "s5_w_glu", "s5_b_glu", "w_branch_out", "w_out")


def _forward(x, positions, params):
    B, S, _ = x.shape
    tm = min(S, 512)
    tq = min(S, 1024)
    ts = min(S, 64)
    depth = params["w_in"].shape[0]
    ca, sb = _rope_tables(positions)
    consts = _prep_consts(tm)
    mask_mla, mask_fox = _head_masks()
    wcat, wgm = _pack_w_in(params["w_in"])
    w_prep = _prep_weights(params, wcat)
    w_s5 = _s5_params(params, B)
    w_merge = _merge_weights(params, wgm)
    h = x
    for l in range(depth):
        qm, km, vm, qf, kf, vf, u = _prep_call(h, ca, sb, w_prep, consts, l, tm)
        y_mla = _attn_call(qm, km, vm, mask_mla, True, tq, "attn_mla")
        y_fox = _attn_call(qf, kf, vf, mask_fox, False, tq, "attn_fox")
        y_s5 = _s5_call(u, w_s5, l, ts, min(S // ts, 8))
        h = _merge_call(h, y_mla, y_fox, y_s5, w_merge, l, min(S, 2 * tm))
    return h


def kernel(x, positions, norm_g, w_in, mla_q_a_norm, mla_w_q_up, mla_kv_a_norm, mla_w_kv_up, mla_q_norm, mla_k_norm, fox_b_f, fox_q_norm, fox_k_norm, s5_lambda_re, s5_lambda_im, s5_log_dt, s5_b_re, s5_b_im, s5_c_re, s5_c_im, s5_d, s5_w_glu, s5_b_glu, w_branch_out, w_out):
    params = dict(zip(_LAYER_KEYS, (norm_g, w_in, mla_q_a_norm, mla_w_q_up, mla_kv_a_norm, mla_w_kv_up, mla_q_norm,
                                    mla_k_norm, fox_b_f, fox_q_norm, fox_k_norm, s5_lambda_re, s5_lambda_im,
                                    s5_log_dt, s5_b_re, s5_b_im, s5_c_re, s5_c_im, s5_d, s5_w_glu, s5_b_glu,
                                    w_branch_out, w_out)))
    return _forward(x, positions, params)
```
